```python
import jax, jax.numpy as jnp
from jax import lax
import numpy as np

D_MODEL = 2048
BATCH = 2
SEQ = 8192
DEPTH = 1

HEAD_DIM = 128
N_HEADS_DIL = 8
N_HEADS_SB = 8
DIL_PATTERNS = ((128, 1), (512, 4), (2048, 16))
BLOCK = 128
ROT_DIM = HEAD_DIM // 4
ROPE_THETA = 500000.0
D_FF = 4 * D_MODEL
PLE_DIM = 256
EPS = 1e-6
W_DIL = N_HEADS_DIL * HEAD_DIM
W_SB = N_HEADS_SB * HEAD_DIM
D_IN = 3 * W_DIL + 3 * W_SB + 2 * D_MODEL

kernel_name = "hybrid_dilated_stickbreaking_gated_block"


def rmsnorm(x, g):
    xf = x.astype(jnp.float32)
    y = xf * lax.rsqrt(jnp.mean(xf * xf, axis=-1, keepdims=True) + EPS)
    return (y * g.astype(jnp.float32)).astype(x.dtype)


def partial_rope(x, pos):
    half = ROT_DIM // 2
    inv = ROPE_THETA ** (-jnp.arange(0, ROT_DIM, 2, dtype=jnp.float32) / ROT_DIM)
    ang = pos[:, None] * inv[None, :]
    cos = jnp.cos(ang)[None, :, None, :]
    sin = jnp.sin(ang)[None, :, None, :]
    xr = x[..., :ROT_DIM].astype(jnp.float32)
    x1, x2 = xr[..., :half], xr[..., half:]
    rot = jnp.concatenate([x1 * cos - x2 * sin, x2 * cos + x1 * sin], axis=-1)
    return jnp.concatenate([rot.astype(x.dtype), x[..., ROT_DIM:]], axis=-1)


def dilated_window(q, k, v, window, dilation):
    B, S, H, Dh = q.shape
    W = window // dilation
    assert W <= BLOCK
    M = S // dilation
    Mp = -(-M // BLOCK) * BLOCK
    nb = Mp // BLOCK

    def to_blocks(t):
        t = t.astype(jnp.float32).reshape(B, M, dilation, H, Dh).transpose(0, 2, 3, 1, 4)
        t = jnp.pad(t, ((0, 0), (0, 0), (0, 0), (0, Mp - M), (0, 0)))
        return t.reshape(B, dilation, H, nb, BLOCK, Dh)

    qb, kb, vb = to_blocks(q), to_blocks(k), to_blocks(v)

    def with_prev(t):
        prev = jnp.pad(t, ((0, 0), (0, 0), (0, 0), (1, 0), (0, 0), (0, 0)))[:, :, :, :-1]
        return jnp.concatenate([prev, t], axis=-2)

    kw, vw = with_prev(kb), with_prev(vb)
    s = jnp.einsum('bdhnqe,bdhnke->bdhnqk', qb, kw) * (HEAD_DIM ** -0.5)
    n_i = jnp.arange(nb)[:, None, None]
    q_i = jnp.arange(BLOCK)[None, :, None]
    k_i = jnp.arange(2 * BLOCK)[None, None, :]
    dist = BLOCK + q_i - k_i
    valid = (dist >= 0) & (dist <= W) & ((n_i > 0) | (k_i >= BLOCK))
    s = jnp.where(valid, s, -jnp.inf)
    m = jnp.max(s, axis=-1, keepdims=True)
    e = jnp.exp(s - m)
    den = jnp.sum(e, axis=-1, keepdims=True)
    o = jnp.einsum('bdhnqk,bdhnke->bdhnqe', e, vw) / den
    lse = (m + jnp.log(den))[..., 0]

    o = o.reshape(B, dilation, H, Mp, Dh)[:, :, :, :M].transpose(0, 3, 1, 2, 4).reshape(B, S, H, Dh)
    lse = lse.reshape(B, dilation, H, Mp)[:, :, :, :M].transpose(0, 3, 1, 2).reshape(B, S, H)
    return o, lse


def dilated_mixture(q, k, v):
    outs, lses = [], []
    for window, dilation in DIL_PATTERNS:
        o, l = dilated_window(q, k, v, window, dilation)
        outs.append(o)
        lses.append(l)
    o = jnp.stack(outs, axis=0)
    w = jax.nn.softmax(jnp.stack(lses, axis=0), axis=0)
    return jnp.sum(w[..., None] * o, axis=0)


def stick_breaking(q, k, v):
    B, S, H, Dh = q.shape
    nq = S // BLOCK
    qf = q.astype(jnp.float32).transpose(0, 2, 1, 3)
    kf = k.astype(jnp.float32).transpose(0, 2, 1, 3)
    vf = v.astype(jnp.float32).transpose(0, 2, 1, 3)
    qblocks = qf.reshape(B, H, nq, BLOCK, Dh).transpose(2, 0, 1, 3, 4)
    key_pos = jnp.arange(S)

    def one_block(args):
        qb, n = args
        z = jnp.einsum('bhqe,bhke->bhqk', qb, kf) * (HEAD_DIM ** -0.5)
        q_pos = n * BLOCK + jnp.arange(BLOCK)
        causal = key_pos[None, :] < q_pos[:, None]
        log_1mb = jnp.where(causal, jax.nn.log_sigmoid(-z), 0.0)
        after = lax.cumsum(log_1mb, axis=3, reverse=True) - log_1mb
        a = jnp.where(causal, jnp.exp(jax.nn.log_sigmoid(z) + after), 0.0)
        return jnp.einsum('bhqk,bhke->bhqe', a, vf)

    o = lax.map(one_block, (qblocks, jnp.arange(nq)))
    return o.transpose(1, 0, 3, 2, 4).reshape(B, S, H, Dh)


def setup_inputs(seed: int = 0) -> dict:
    key = jax.random.key(seed)
    ks = jax.random.split(key, 16)

    def w(k, shape, fan_in):
        return jax.random.normal(k, shape, jnp.float32) * (fan_in ** -0.5)

    def gain(k, shape):
        return 1.0 + 0.05 * jax.random.normal(k, shape, jnp.float32)

    return {
        "x": jax.random.normal(ks[0], (BATCH, SEQ, D_MODEL), jnp.float32),
        "p": jax.random.normal(ks[1], (DEPTH, BATCH, SEQ, PLE_DIM), jnp.float32),
        "g_mix": gain(ks[2], (DEPTH, D_MODEL)),
        "w_in": w(ks[3], (DEPTH, D_MODEL, D_IN), D_MODEL),
        "qn_gain": gain(ks[4], (DEPTH, HEAD_DIM)),
        "kn_gain": gain(ks[5], (DEPTH, HEAD_DIM)),
        "w_branch_a": w(ks[6], (DEPTH, W_DIL, D_MODEL), W_DIL),
        "w_branch_b": w(ks[7], (DEPTH, W_SB, D_MODEL), W_SB),
        "w_out": w(ks[8], (DEPTH, D_MODEL, D_MODEL), D_MODEL),
        "g_mlp": gain(ks[9], (DEPTH, D_MODEL)),
        "w_up": w(ks[10], (DEPTH, D_MODEL, D_FF), D_MODEL),
        "w_down": w(ks[11], (DEPTH, D_FF, D_MODEL), D_FF),
        "g_ple": gain(ks[12], (DEPTH, D_MODEL)),
        "w_ple_gate": w(ks[13], (DEPTH, D_MODEL, D_MODEL), D_MODEL),
        "w_ple_proj": w(ks[14], (DEPTH, PLE_DIM, D_MODEL), PLE_DIM),
    }


def reference(x, p, g_mix, w_in, qn_gain, kn_gain, w_branch_a, w_branch_b, w_out,
              g_mlp, w_up, w_down, g_ple, w_ple_gate, w_ple_proj):
    B, S, _ = x.shape
    pos = jnp.arange(S, dtype=jnp.float32)
    splits = np.cumsum([W_DIL, W_DIL, W_DIL, W_SB, W_SB, W_SB, D_MODEL])
    for i in range(DEPTH):
        h = rmsnorm(x, g_mix[i])
        proj = h @ w_in[i]
        qa, ka, va, qb, kb, vb, ga, gb = jnp.split(proj, splits, axis=-1)
        qa = qa.reshape(B, S, N_HEADS_DIL, HEAD_DIM)
        ka = ka.reshape(B, S, N_HEADS_DIL, HEAD_DIM)
        va = va.reshape(B, S, N_HEADS_DIL, HEAD_DIM)
        qa = partial_rope(rmsnorm(qa, qn_gain[i]), pos)
        ka = partial_rope(rmsnorm(ka, kn_gain[i]), pos)
        ya = dilated_mixture(qa, ka, va).astype(x.dtype).reshape(B, S, W_DIL)

        qb = qb.reshape(B, S, N_HEADS_SB, HEAD_DIM)
        kb = kb.reshape(B, S, N_HEADS_SB, HEAD_DIM)
        vb = vb.reshape(B, S, N_HEADS_SB, HEAD_DIM)
        yb = stick_breaking(qb, kb, vb).astype(x.dtype).reshape(B, S, W_SB)

        merged = jax.nn.sigmoid(ga) * (ya @ w_branch_a[i]) + jax.nn.sigmoid(gb) * (yb @ w_branch_b[i])
        x = x + merged @ w_out[i]

        hm = rmsnorm(x, g_mlp[i])
        x = x + jnp.square(jax.nn.relu(hm @ w_up[i])) @ w_down[i]

        hp = rmsnorm(x, g_ple[i])
        x = x + (p[i] @ w_ple_proj[i]) * jax.nn.sigmoid(hp @ w_ple_gate[i])
    return x
```

```python
import functools

import jax
import jax.numpy as jnp
from jax import lax
from jax.experimental import pallas as pl
from jax.experimental.pallas import tpu as pltpu

D_MODEL = 2048
HEAD_DIM = 128
N_HEADS = 8
W_HEADS = N_HEADS * HEAD_DIM
DIL_PATTERNS = ((128, 1), (512, 4), (2048, 16))
BLOCK = 128
ROT_DIM = HEAD_DIM // 4
ROPE_THETA = 500000.0
D_FF = 4 * D_MODEL
EPS = 1e-6
D_IN = 6 * W_HEADS + 2 * D_MODEL
SCALE = HEAD_DIM ** -0.5

COL_QA, COL_KA, COL_VA, COL_QB, COL_KB, COL_VB = 0, 1, 2, 3, 4, 5
N_COL_TILES = D_IN // W_HEADS

F32 = jnp.float32
BF16 = jnp.bfloat16
NEG_INF = float("-inf")

MIB = 1024 * 1024


def _rms_scale(x, gain):
    ms = jnp.mean(x * x, axis=-1, keepdims=True)
    return x * lax.rsqrt(ms + EPS) * gain


INPROJ_TM = 512


def _inproj_kernel(x_ref, g_ref, w_ref, qn_ref, kn_ref, cos_ref, sa_ref, sb_ref, o_ref, hn_ref):
    j = pl.program_id(1)

    @pl.when(j == 0)
    def _():
        hn_ref[...] = _rms_scale(x_ref[...], g_ref[...]).astype(BF16)

    acc = jnp.dot(hn_ref[...], w_ref[...], preferred_element_type=F32)

    def qk_epilogue(gain_ref, scale):
        for h in range(N_HEADS):
            sl = slice(h * HEAD_DIM, (h + 1) * HEAD_DIM)
            y = _rms_scale(acc[:, sl], gain_ref[...])
            y = (y * cos_ref[...]
                 + pltpu.roll(y, HEAD_DIM - ROT_DIM // 2, 1) * sa_ref[...]
                 + pltpu.roll(y, ROT_DIM // 2, 1) * sb_ref[...])
            if scale is not None:
                y = y * scale
            o_ref[:, sl] = y.astype(BF16)

    @pl.when(j == COL_QA)
    def _():
        qk_epilogue(qn_ref, SCALE)

    @pl.when(j == COL_KA)
    def _():
        qk_epilogue(kn_ref, None)

    @pl.when(j == COL_QB)
    def _():
        o_ref[...] = (acc * SCALE).astype(BF16)

    @pl.when((j == COL_VA) | (j > COL_QB))
    def _():
        o_ref[...] = acc.astype(BF16)


def _inproj(x2d, g, w_bf16, qn, kn, cos_t, sa_t, sb_t, seq):
    t = x2d.shape[0]
    tm = INPROJ_TM
    nseq = seq // tm
    tab_spec = pl.BlockSpec((tm, HEAD_DIM), lambda i, j: (i % nseq, 0))
    vec_spec = pl.BlockSpec((1, HEAD_DIM), lambda i, j: (0, 0))
    return pl.pallas_call(
        _inproj_kernel,
        grid=(t // tm, N_COL_TILES),
        in_specs=[
            pl.BlockSpec((tm, D_MODEL), lambda i, j: (i, 0)),
            pl.BlockSpec((1, D_MODEL), lambda i, j: (0, 0)),
            pl.BlockSpec((D_MODEL, W_HEADS), lambda i, j: (0, j)),
            vec_spec, vec_spec, tab_spec, tab_spec, tab_spec,
        ],
        out_specs=pl.BlockSpec((tm, W_HEADS), lambda i, j: (i, j)),
        out_shape=jax.ShapeDtypeStruct((t, D_IN), BF16),
        scratch_shapes=[pltpu.VMEM((tm, D_MODEL), BF16)],
        compiler_params=pltpu.CompilerParams(
            dimension_semantics=("arbitrary", "arbitrary"),
            vmem_limit_bytes=40 * MIB),
        name="inproj",
    )(x2d, g, w_bf16, qn, kn, cos_t, sa_t, sb_t)


DIL_QB = 256
LSE_LANES = HEAD_DIM // N_HEADS


def _dilated_kernel(*refs, has_in, has_lse_out):
    q_ref, kp_ref, k_ref, vp_ref, v_ref = refs[:5]
    pos = 5
    if has_in:
        oin_ref, lin_ref = refs[pos:pos + 2]
        pos += 2
    o_ref = refs[pos]
    lout_ref = refs[pos + 1] if has_lse_out else None

    not_first = pl.program_id(2) > 0
    qi = lax.broadcasted_iota(jnp.int32, (BLOCK, BLOCK), 0)
    kj = lax.broadcasted_iota(jnp.int32, (BLOCK, BLOCK), 1)
    cur_ok = kj <= qi
    prev_ok = kj >= qi
    lane_head = lax.broadcasted_iota(jnp.int32, (BLOCK, HEAD_DIM), 1) // LSE_LANES
    dn = (((1,), (1,)), ((), ()))

    for sb in range(DIL_QB // BLOCK):
        rows = slice(sb * BLOCK, (sb + 1) * BLOCK)
        prows = slice((sb - 1) * BLOCK, sb * BLOCK)
        lse_pack = jnp.zeros((BLOCK, HEAD_DIM), F32)
        if has_in:
            lin_pack = lin_ref[rows, :]
        for h in range(N_HEADS):
            cols = slice(h * HEAD_DIM, (h + 1) * HEAD_DIM)
            q = q_ref[rows, cols]
            kc = k_ref[rows, cols]
            vc = v_ref[rows, cols]
            if sb == 0:
                kp = kp_ref[:, cols]
                vp = vp_ref[:, cols]
                p_ok = prev_ok & not_first
            else:
                kp = k_ref[prows, cols]
                vp = v_ref[prows, cols]
                p_ok = prev_ok
            s_c = jnp.where(cur_ok, lax.dot_general(q, kc, dn, preferred_element_type=F32), NEG_INF)
            s_p = jnp.where(p_ok, lax.dot_general(q, kp, dn, preferred_element_type=F32), NEG_INF)
            m = jnp.maximum(jnp.max(s_c, axis=-1, keepdims=True),
                            jnp.max(s_p, axis=-1, keepdims=True))
            e_c = jnp.exp(s_c - m)
            e_p = jnp.exp(s_p - m)
            den = jnp.sum(e_c, axis=-1, keepdims=True) + jnp.sum(e_p, axis=-1, keepdims=True)
            o = (jnp.dot(e_c.astype(BF16), vc, preferred_element_type=F32)
                 + jnp.dot(e_p.astype(BF16), vp, preferred_element_type=F32)) / den
            lse = m + jnp.log(den)
            if has_in:
                lin = jnp.max(jnp.where(lane_head == h, lin_pack, NEG_INF), axis=-1, keepdims=True)
                mm = jnp.maximum(lin, lse)
                w_in = jnp.exp(lin - mm)
                w_new = jnp.exp(lse - mm)
                tot = w_in + w_new
                o = (oin_ref[rows, cols].astype(F32) * w_in + o * w_new) / tot
                lse = mm + jnp.log(tot)
            o_ref[rows, cols] = o.astype(o_ref.dtype)
            if has_lse_out:
                lse_pack = jnp.where(lane_head == h, lse, lse_pack)
        if has_lse_out:
            lout_ref[rows, :] = lse_pack


def _dilated(proj3, dilation, o_in, lse_in, last):
    b, s, _ = proj3.shape
    d = dilation
    m = s // d
    has_in = o_in is not None
    proj_v = proj3.reshape(b, m, d * D_IN)
    nq = m // DIL_QB
    sub = DIL_QB // BLOCK

    def col(c):
        return lambda bi, r, mi: (bi, mi, r * N_COL_TILES + c)

    def prev_col(c):
        return lambda bi, r, mi: (bi, jnp.maximum(mi * sub - 1, 0), r * N_COL_TILES + c)

    in_specs = [
        pl.BlockSpec((None, DIL_QB, W_HEADS), col(COL_QA)),
        pl.BlockSpec((None, BLOCK, W_HEADS), prev_col(COL_KA)),
        pl.BlockSpec((None, DIL_QB, W_HEADS), col(COL_KA)),
        pl.BlockSpec((None, BLOCK, W_HEADS), prev_col(COL_VA)),
        pl.BlockSpec((None, DIL_QB, W_HEADS), col(COL_VA)),
    ]
    args = [proj_v] * 5
    o_spec = pl.BlockSpec((None, DIL_QB, W_HEADS), lambda bi, r, mi: (bi, mi, r))
    l_spec = pl.BlockSpec((None, DIL_QB, HEAD_DIM), lambda bi, r, mi: (bi, mi, r))
    if has_in:
        in_specs += [o_spec, l_spec]
        args += [o_in.reshape(b, m, d * W_HEADS), lse_in.reshape(b, m, d * HEAD_DIM)]
    out_shape = [jax.ShapeDtypeStruct((b, m, d * W_HEADS), BF16)]
    out_specs = [o_spec]
    if not last:
        out_shape.append(jax.ShapeDtypeStruct((b, m, d * HEAD_DIM), F32))
        out_specs.append(l_spec)
    outs = pl.pallas_call(
        functools.partial(_dilated_kernel, has_in=has_in, has_lse_out=not last),
        grid=(b, d, nq),
        in_specs=in_specs,
        out_specs=out_specs,
        out_shape=out_shape,
        compiler_params=pltpu.CompilerParams(
            dimension_semantics=("arbitrary", "arbitrary", "arbitrary"),
            vmem_limit_bytes=32 * MIB),
        name=f"dilated_d{d}",
    )(*args)
    o = outs[0].reshape(b, s, W_HEADS)
    lse = None if last else outs[1].reshape(b, s, HEAD_DIM)
    return o, lse


SB_BLK = 256


def _stickbreak_kernel(q_ref, k_ref, v_ref, o_ref, acc_ref, tot_ref):
    qi = pl.program_id(2)
    q = q_ref[...]
    row = lax.broadcasted_iota(jnp.int32, (SB_BLK, SB_BLK), 0)
    col = lax.broadcasted_iota(jnp.int32, (SB_BLK, SB_BLK), 1)
    later = (row > col).astype(BF16)
    causal = col < row
    dn = (((1,), (1,)), ((), ()))

    acc_ref[...] = jnp.zeros_like(acc_ref)
    tot_ref[...] = jnp.zeros_like(tot_ref)

    def step(kb, diag):
        off = pl.multiple_of(kb * SB_BLK, SB_BLK)
        k = k_ref[pl.ds(off, SB_BLK), :]
        v = v_ref[pl.ds(off, SB_BLK), :]
        z = lax.dot_general(q, k, dn, preferred_element_type=F32)
        sp = jnp.maximum(z, 0.0) + jnp.log(1.0 + jnp.exp(-jnp.abs(z)))
        if diag:
            sp = jnp.where(causal, sp, 0.0)
        cs = jnp.dot(sp.astype(BF16), later, preferred_element_type=F32)
        arg = (z - sp) - cs - tot_ref[...]
        if diag:
            arg = jnp.where(causal, arg, NEG_INF)
        a = jnp.exp(arg)
        acc_ref[...] += jnp.dot(a.astype(BF16), v, preferred_element_type=F32)
        tot_ref[...] += cs[:, 0:1] + sp[:, 0:1]

    step(qi, True)

    def body(i, carry):
        step(qi - 1 - i, False)
        return carry

    lax.fori_loop(0, qi, body, 0)
    o_ref[...] = acc_ref[...].astype(o_ref.dtype)


def _stickbreak(proj3):
    b, s, _ = proj3.shape
    cq = COL_QB * N_HEADS
    ck = COL_KB * N_HEADS
    cv = COL_VB * N_HEADS
    return pl.pallas_call(
        _stickbreak_kernel,
        grid=(b, N_HEADS, s // SB_BLK),
        in_specs=[
            pl.BlockSpec((None, SB_BLK, HEAD_DIM), lambda bi, h, qi: (bi, qi, cq + h)),
            pl.BlockSpec((None, s, HEAD_DIM), lambda bi, h, qi: (bi, 0, ck + h)),
            pl.BlockSpec((None, s, HEAD_DIM), lambda bi, h, qi: (bi, 0, cv + h)),
        ],
        out_specs=pl.BlockSpec((None, SB_BLK, HEAD_DIM), lambda bi, h, qi: (bi, qi, h)),
        out_shape=jax.ShapeDtypeStruct((b, s, W_HEADS), BF16),
        scratch_shapes=[pltpu.VMEM((SB_BLK, HEAD_DIM), F32), pltpu.VMEM((SB_BLK, 1), F32)],
        compiler_params=pltpu.CompilerParams(
            dimension_semantics=("arbitrary", "arbitrary", "arbitrary"),
            vmem_limit_bytes=32 * MIB),
        name="stickbreak",
    )(proj3, proj3, proj3)


MIX_TM = 256


def _mixout_kernel(x_ref, ya_ref, yb_ref, ga_ref, gb_ref, wa_ref, wb_ref, wo_ref, o_ref):
    ta = jnp.dot(ya_ref[...], wa_ref[...], preferred_element_type=F32)
    tb = jnp.dot(yb_ref[...], wb_ref[...], preferred_element_type=F32)
    merged = (jax.nn.sigmoid(ga_ref[...].astype(F32)) * ta
              + jax.nn.sigmoid(gb_ref[...].astype(F32)) * tb)
    o_ref[...] = x_ref[...] + jnp.dot(merged.astype(BF16), wo_ref[...], preferred_element_type=F32)


def _resident(shape):
    return pl.BlockSpec(shape, lambda *_: (0,) * len(shape), pipeline_mode=pl.Buffered(1))


def _mixout(x2d, ya2d, yb2d, proj2d, wa, wb, wo):
    t = x2d.shape[0]
    tm = MIX_TM
    ga_col = (6 * W_HEADS) // D_MODEL
    gb_col = ga_col + 1
    return pl.pallas_call(
        _mixout_kernel,
        grid=(t // tm,),
        in_specs=[
            pl.BlockSpec((tm, D_MODEL), lambda i: (i, 0)),
            pl.BlockSpec((tm, W_HEADS), lambda i: (i, 0)),
            pl.BlockSpec((tm, W_HEADS), lambda i: (i, 0)),
            pl.BlockSpec((tm, D_MODEL), lambda i: (i, ga_col)),
            pl.BlockSpec((tm, D_MODEL), lambda i: (i, gb_col)),
            _resident((W_HEADS, D_MODEL)),
            _resident((W_HEADS, D_MODEL)),
            _resident((D_MODEL, D_MODEL)),
        ],
        out_specs=pl.BlockSpec((tm, D_MODEL), lambda i: (i, 0)),
        out_shape=jax.ShapeDtypeStruct((t, D_MODEL), F32),
        compiler_params=pltpu.CompilerParams(
            dimension_semantics=("arbitrary",),
            vmem_limit_bytes=48 * MIB),
        name="mixout",
    )(x2d, ya2d, yb2d, proj2d, proj2d, wa, wb, wo)


MLP_TM = 512
MLP_TF = 512


def _mlp_kernel(x_ref, g_ref, wu_ref, wd_ref, o_ref, hn_ref, acc_ref):
    f = pl.program_id(1)

    @pl.when(f == 0)
    def _():
        x = x_ref[...]
        hn_ref[...] = _rms_scale(x, g_ref[...]).astype(BF16)
        acc_ref[...] = x

    u = jnp.dot(hn_ref[...], wu_ref[...], preferred_element_type=F32)
    a = jnp.square(jnp.maximum(u, 0.0)).astype(BF16)
    acc_ref[...] += jnp.dot(a, wd_ref[...], preferred_element_type=F32)

    @pl.when(f == pl.num_programs(1) - 1)
    def _():
        o_ref[...] = acc_ref[...]


def _mlp(x2d, g, wu, wd):
    t = x2d.shape[0]
    tm, tf = MLP_TM, MLP_TF
    return pl.pallas_call(
        _mlp_kernel,
        grid=(t // tm, D_FF // tf),
        in_specs=[
            pl.BlockSpec((tm, D_MODEL), lambda i, f: (i, 0)),
            pl.BlockSpec((1, D_MODEL), lambda i, f: (0, 0)),
            pl.BlockSpec((D_MODEL, tf), lambda i, f: (0, f)),
            pl.BlockSpec((tf, D_MODEL), lambda i, f: (f, 0)),
        ],
        out_specs=pl.BlockSpec((tm, D_MODEL), lambda i, f: (i, 0)),
        out_shape=jax.ShapeDtypeStruct((t, D_MODEL), F32),
        scratch_shapes=[pltpu.VMEM((tm, D_MODEL), BF16), pltpu.VMEM((tm, D_MODEL), F32)],
        compiler_params=pltpu.CompilerParams(
            dimension_semantics=("arbitrary", "arbitrary"),
            vmem_limit_bytes=48 * MIB),
        name="mlp",
    )(x2d, g, wu, wd)


PLE_TM = 256


def _ple_kernel(x_ref, p_ref, g_ref, wg_ref, wp_ref, o_ref):
    x = x_ref[...]
    hn = _rms_scale(x, g_ref[...]).astype(BF16)
    gate = jnp.dot(hn, wg_ref[...], preferred_element_type=F32)
    pp = jnp.dot(p_ref[...].astype(BF16), wp_ref[...], preferred_element_type=F32)
    o_ref[...] = x + pp * jax.nn.sigmoid(gate)


def _ple(x2d, p2d, g, wg, wp):
    t = x2d.shape[0]
    tm = PLE_TM
    ple_dim = p2d.shape[1]
    return pl.pallas_call(
        _ple_kernel,
        grid=(t // tm,),
        in_specs=[
            pl.BlockSpec((tm, D_MODEL), lambda i: (i, 0)),
            pl.BlockSpec((tm, ple_dim), lambda i: (i, 0)),
            pl.BlockSpec((1, D_MODEL), lambda i: (0, 0)),
            _resident((D_MODEL, D_MODEL)),
            _resident((ple_dim, D_MODEL)),
        ],
        out_specs=pl.BlockSpec((tm, D_MODEL), lambda i: (i, 0)),
        out_shape=jax.ShapeDtypeStruct((t, D_MODEL), F32),
        compiler_params=pltpu.CompilerParams(
            dimension_semantics=("arbitrary",),
            vmem_limit_bytes=40 * MIB),
        name="ple",
    )(x2d, p2d, g, wg, wp)


def _rope_tables(seq):
    half = ROT_DIM // 2
    pos = jnp.arange(seq, dtype=F32)
    inv = ROPE_THETA ** (-jnp.arange(0, ROT_DIM, 2, dtype=F32) / ROT_DIM)
    ang = pos[:, None] * inv[None, :]
    cos, sin = jnp.cos(ang), jnp.sin(ang)
    rest = HEAD_DIM - ROT_DIM
    cos_t = jnp.concatenate([cos, cos, jnp.ones((seq, rest), F32)], axis=-1)
    zeros_h = jnp.zeros((seq, half), F32)
    zeros_r = jnp.zeros((seq, rest), F32)
    sa_t = jnp.concatenate([-sin, zeros_h, zeros_r], axis=-1)
    sb_t = jnp.concatenate([zeros_h, sin, zeros_r], axis=-1)
    return cos_t, sa_t, sb_t


def kernel(x, p, g_mix, w_in, qn_gain, kn_gain, w_branch_a, w_branch_b, w_out,
           g_mlp, w_up, w_down, g_ple, w_ple_gate, w_ple_proj):
    b, s, d = x.shape
    t = b * s
    depth = w_in.shape[0]
    cos_t, sa_t, sb_t = _rope_tables(s)
    x2d = x.reshape(t, d)
    for i in range(depth):
        proj2d = _inproj(x2d, g_mix[i][None, :], w_in[i].astype(BF16),
                         qn_gain[i][None, :], kn_gain[i][None, :], cos_t, sa_t, sb_t, s)
        proj3 = proj2d.reshape(b, s, D_IN)

        o, lse = None, None
        for n, (_, dilation) in enumerate(DIL_PATTERNS):
            o, lse = _dilated(proj3, dilation, o, lse, last=(n == len(DIL_PATTERNS) - 1))
        ya2d = o.reshape(t, W_HEADS)
        yb2d = _stickbreak(proj3).reshape(t, W_HEADS)

        x2d = _mixout(x2d, ya2d, yb2d, proj2d, w_branch_a[i].astype(BF16),
                      w_branch_b[i].astype(BF16), w_out[i].astype(BF16))
        x2d = _mlp(x2d, g_mlp[i][None, :], w_up[i].astype(BF16), w_down[i].astype(BF16))
        x2d = _ple(x2d, p[i].reshape(t, -1), g_ple[i][None, :],
                   w_ple_gate[i].astype(BF16), w_ple_proj[i].astype(BF16))
    return x2d.reshape(b, s, d)
```

```python
import functools

import jax
import jax.numpy as jnp
from jax import lax
from jax.experimental import pallas as pl
from jax.experimental.pallas import tpu as pltpu

D_MODEL = 2048
HEAD_DIM = 128
N_HEADS = 8
W_HEADS = N_HEADS * HEAD_DIM
DIL_PATTERNS = ((128, 1), (512, 4), (2048, 16))
BLOCK = 128
ROT_DIM = HEAD_DIM // 4
ROPE_THETA = 500000.0
D_FF = 4 * D_MODEL
EPS = 1e-6
D_IN = 6 * W_HEADS + 2 * D_MODEL
SCALE = HEAD_DIM ** -0.5

COL_QA, COL_KA, COL_VA, COL_QB, COL_KB, COL_VB = 0, 1, 2, 3, 4, 5
N_COL_TILES = D_IN // W_HEADS

F32 = jnp.float32
BF16 = jnp.bfloat16
NEG_INF = float("-inf")

MIB = 1024 * 1024


def _rms_scale(x, gain):
    ms = jnp.mean(x * x, axis=-1, keepdims=True)
    return x * lax.rsqrt(ms + EPS) * gain


INPROJ_TM = 512


def _inproj_kernel(x_ref, g_ref, w_ref, qn_ref, kn_ref, cos_ref, sa_ref, sb_ref, o_ref, hn_ref):
    j = pl.program_id(1)

    @pl.when(j == 0)
    def _():
        hn_ref[...] = _rms_scale(x_ref[...], g_ref[...]).astype(BF16)

    acc = jnp.dot(hn_ref[...], w_ref[...], preferred_element_type=F32)

    def qk_epilogue(gain_ref, scale):
        for h in range(N_HEADS):
            sl = slice(h * HEAD_DIM, (h + 1) * HEAD_DIM)
            y = _rms_scale(acc[:, sl], gain_ref[...])
            y = (y * cos_ref[...]
                 + pltpu.roll(y, HEAD_DIM - ROT_DIM // 2, 1) * sa_ref[...]
                 + pltpu.roll(y, ROT_DIM // 2, 1) * sb_ref[...])
            if scale is not None:
                y = y * scale
            o_ref[:, sl] = y.astype(BF16)

    @pl.when(j == COL_QA)
    def _():
        qk_epilogue(qn_ref, SCALE)

    @pl.when(j == COL_KA)
    def _():
        qk_epilogue(kn_ref, None)

    @pl.when(j == COL_QB)
    def _():
        o_ref[...] = (acc * SCALE).astype(BF16)

    @pl.when((j == COL_VA) | (j > COL_QB))
    def _():
        o_ref[...] = acc.astype(BF16)


def _inproj(x2d, g, w_bf16, qn, kn, cos_t, sa_t, sb_t, seq):
    t = x2d.shape[0]
    tm = INPROJ_TM
    nseq = seq // tm
    tab_spec = pl.BlockSpec((tm, HEAD_DIM), lambda i, j: (i % nseq, 0))
    vec_spec = pl.BlockSpec((1, HEAD_DIM), lambda i, j: (0, 0))
    return pl.pallas_call(
        _inproj_kernel,
        grid=(t // tm, N_COL_TILES),
        in_specs=[
            pl.BlockSpec((tm, D_MODEL), lambda i, j: (i, 0)),
            pl.BlockSpec((1, D_MODEL), lambda i, j: (0, 0)),
            pl.BlockSpec((D_MODEL, W_HEADS), lambda i, j: (0, j)),
            vec_spec, vec_spec, tab_spec, tab_spec, tab_spec,
        ],
        out_specs=pl.BlockSpec((tm, W_HEADS), lambda i, j: (i, j)),
        out_shape=jax.ShapeDtypeStruct((t, D_IN), BF16),
        scratch_shapes=[pltpu.VMEM((tm, D_MODEL), BF16)],
        compiler_params=pltpu.CompilerParams(
            dimension_semantics=("arbitrary", "arbitrary"),
            vmem_limit_bytes=40 * MIB),
        name="inproj",
    )(x2d, g, w_bf16, qn, kn, cos_t, sa_t, sb_t)


DIL_QB = 256
LSE_LANES = HEAD_DIM // N_HEADS


def _dilated_kernel(*refs, has_in, has_lse_out):
    q_ref, kp_ref, k_ref, vp_ref, v_ref = refs[:5]
    pos = 5
    if has_in:
        oin_ref, lin_ref = refs[pos:pos + 2]
        pos += 2
    o_ref = refs[pos]
    lout_ref = refs[pos + 1] if has_lse_out else None

    not_first = pl.program_id(2) > 0
    qi = lax.broadcasted_iota(jnp.int32, (BLOCK, BLOCK), 0)
    kj = lax.broadcasted_iota(jnp.int32, (BLOCK, BLOCK), 1)
    cur_ok = kj <= qi
    prev_ok = kj >= qi
    lane_head = lax.broadcasted_iota(jnp.int32, (BLOCK, HEAD_DIM), 1) // LSE_LANES
    dn = (((1,), (1,)), ((), ()))

    for sb in range(DIL_QB // BLOCK):
        rows = slice(sb * BLOCK, (sb + 1) * BLOCK)
        prows = slice((sb - 1) * BLOCK, sb * BLOCK)
        lse_pack = jnp.zeros((BLOCK, HEAD_DIM), F32)
        if has_in:
            lin_pack = lin_ref[rows, :]
        for h in range(N_HEADS):
            cols = slice(h * HEAD_DIM, (h + 1) * HEAD_DIM)
            q = q_ref[rows, cols]
            kc = k_ref[rows, cols]
            vc = v_ref[rows, cols]
            if sb == 0:
                kp = kp_ref[:, cols]
                vp = vp_ref[:, cols]
                p_ok = prev_ok & not_first
            else:
                kp = k_ref[prows, cols]
                vp = v_ref[prows, cols]
                p_ok = prev_ok
            s_c = jnp.where(cur_ok, lax.dot_general(q, kc, dn, preferred_element_type=F32), NEG_INF)
            s_p = jnp.where(p_ok, lax.dot_general(q, kp, dn, preferred_element_type=F32), NEG_INF)
            m = jnp.maximum(jnp.max(s_c, axis=-1, keepdims=True),
                            jnp.max(s_p, axis=-1, keepdims=True))
            e_c = jnp.exp(s_c - m)
            e_p = jnp.exp(s_p - m)
            den = jnp.sum(e_c, axis=-1, keepdims=True) + jnp.sum(e_p, axis=-1, keepdims=True)
            o = (jnp.dot(e_c.astype(BF16), vc, preferred_element_type=F32)
                 + jnp.dot(e_p.astype(BF16), vp, preferred_element_type=F32)) / den
            lse = m + jnp.log(den)
            if has_in:
                lin = jnp.max(jnp.where(lane_head == h, lin_pack, NEG_INF), axis=-1, keepdims=True)
                mm = jnp.maximum(lin, lse)
                w_in = jnp.exp(lin - mm)
                w_new = jnp.exp(lse - mm)
                tot = w_in + w_new
                o = (oin_ref[rows, cols].astype(F32) * w_in + o * w_new) / tot
                lse = mm + jnp.log(tot)
            o_ref[rows, cols] = o.astype(o_ref.dtype)
            if has_lse_out:
                lse_pack = jnp.where(lane_head == h, lse, lse_pack)
        if has_lse_out:
            lout_ref[rows, :] = lse_pack


def _dilated(proj3, dilation, o_in, lse_in, last):
    b, s, _ = proj3.shape
    d = dilation
    m = s // d
    has_in = o_in is not None
    proj_v = proj3.reshape(b, m, d * D_IN)
    nq = m // DIL_QB
    sub = DIL_QB // BLOCK

    def col(c):
        return lambda bi, r, mi: (bi, mi, r * N_COL_TILES + c)

    def prev_col(c):
        return lambda bi, r, mi: (bi, jnp.maximum(mi * sub - 1, 0), r * N_COL_TILES + c)

    in_specs = [
        pl.BlockSpec((None, DIL_QB, W_HEADS), col(COL_QA)),
        pl.BlockSpec((None, BLOCK, W_HEADS), prev_col(COL_KA)),
        pl.BlockSpec((None, DIL_QB, W_HEADS), col(COL_KA)),
        pl.BlockSpec((None, BLOCK, W_HEADS), prev_col(COL_VA)),
        pl.BlockSpec((None, DIL_QB, W_HEADS), col(COL_VA)),
    ]
    args = [proj_v] * 5
    o_spec = pl.BlockSpec((None, DIL_QB, W_HEADS), lambda bi, r, mi: (bi, mi, r))
    l_spec = pl.BlockSpec((None, DIL_QB, HEAD_DIM), lambda bi, r, mi: (bi, mi, r))
    if has_in:
        in_specs += [o_spec, l_spec]
        args += [o_in.reshape(b, m, d * W_HEADS), lse_in.reshape(b, m, d * HEAD_DIM)]
    out_shape = [jax.ShapeDtypeStruct((b, m, d * W_HEADS), BF16)]
    out_specs = [o_spec]
    if not last:
        out_shape.append(jax.ShapeDtypeStruct((b, m, d * HEAD_DIM), F32))
        out_specs.append(l_spec)
    outs = pl.pallas_call(
        functools.partial(_dilated_kernel, has_in=has_in, has_lse_out=not last),
        grid=(b, d, nq),
        in_specs=in_specs,
        out_specs=out_specs,
        out_shape=out_shape,
        compiler_params=pltpu.CompilerParams(
            dimension_semantics=("arbitrary", "arbitrary", "arbitrary"),
            vmem_limit_bytes=32 * MIB),
        name=f"dilated_d{d}",
    )(*args)
    o = outs[0].reshape(b, s, W_HEADS)
    lse = None if last else outs[1].reshape(b, s, HEAD_DIM)
    return o, lse


SB_BLK = 256
SB_DONE = 110.0


def _sb_block(q, k, v, tot, later, causal):
    dn = (((1,), (1,)), ((), ()))
    z = lax.dot_general(q, k, dn, preferred_element_type=F32)
    sp = jnp.maximum(z, 0.0) + jnp.log(1.0 + jnp.exp(-jnp.abs(z)))
    if causal is not None:
        sp = jnp.where(causal, sp, 0.0)
    cs = jnp.dot(sp.astype(BF16), later, preferred_element_type=F32)
    arg = (z - sp) - cs - jnp.concatenate([tot, tot], axis=1)
    if causal is not None:
        arg = jnp.where(causal, arg, NEG_INF)
    pv = jnp.dot(jnp.exp(arg).astype(BF16), v, preferred_element_type=F32)
    rs = jnp.sum(sp, axis=1, keepdims=True)
    return pv, jnp.broadcast_to(rs, (SB_BLK, HEAD_DIM))


def _stickbreak_kernel(q_ref, k_ref, v_ref, o_ref, acc_ref, tot_ref):
    qi = pl.program_id(1)
    row = lax.broadcasted_iota(jnp.int32, (SB_BLK, SB_BLK), 0)
    col = lax.broadcasted_iota(jnp.int32, (SB_BLK, SB_BLK), 1)
    later = (row > col).astype(BF16)
    causal = col < row
    heads = [slice(h * HEAD_DIM, (h + 1) * HEAD_DIM) for h in range(N_HEADS)]

    def key_rows(kb):
        return pl.ds(pl.multiple_of(kb * SB_BLK, SB_BLK), SB_BLK)

    rows = key_rows(qi)
    zero = jnp.zeros((SB_BLK, HEAD_DIM), F32)
    for hs in heads:
        pv, rs = _sb_block(q_ref[:, hs], k_ref[rows, hs], v_ref[rows, hs], zero, later, causal)
        acc_ref[:, hs] = pv
        tot_ref[:, hs] = rs

    def earlier_block(kb):
        rows = key_rows(kb)
        for hs in heads:
            pv, rs = _sb_block(q_ref[:, hs], k_ref[rows, hs], v_ref[rows, hs],
                               tot_ref[:, hs], later, None)
            acc_ref[:, hs] += pv
            tot_ref[:, hs] += rs

    @pl.when(qi > 0)
    def _():
        earlier_block(qi - 1)

    def cond(c):
        kb, consumed = c
        return (kb >= 0) & (consumed < SB_DONE)

    def body(c):
        kb, _ = c
        earlier_block(kb)
        return kb - 1, jnp.min(tot_ref[...])

    lax.while_loop(cond, body, (qi - 2, jnp.min(tot_ref[...])))
    o_ref[...] = acc_ref[...].astype(o_ref.dtype)


def _stickbreak(proj3):
    b, s, _ = proj3.shape
    return pl.pallas_call(
        _stickbreak_kernel,
        grid=(b, s // SB_BLK),
        in_specs=[
            pl.BlockSpec((None, SB_BLK, W_HEADS), lambda bi, qi: (bi, qi, COL_QB)),
            pl.BlockSpec((None, s, W_HEADS), lambda bi, qi: (bi, 0, COL_KB),
                         pipeline_mode=pl.Buffered(1)),
            pl.BlockSpec((None, s, W_HEADS), lambda bi, qi: (bi, 0, COL_VB),
                         pipeline_mode=pl.Buffered(1)),
        ],
        out_specs=pl.BlockSpec((None, SB_BLK, W_HEADS), lambda bi, qi: (bi, qi, 0)),
        out_shape=jax.ShapeDtypeStruct((b, s, W_HEADS), BF16),
        scratch_shapes=[pltpu.VMEM((SB_BLK, W_HEADS), F32), pltpu.VMEM((SB_BLK, W_HEADS), F32)],
        compiler_params=pltpu.CompilerParams(
            dimension_semantics=("arbitrary", "arbitrary"),
            vmem_limit_bytes=48 * MIB),
        name="stickbreak",
    )(proj3, proj3, proj3)


MIX_TM = 256


def _mixout_kernel(x_ref, ya_ref, yb_ref, ga_ref, gb_ref, wa_ref, wb_ref, wo_ref, o_ref):
    ta = jnp.dot(ya_ref[...], wa_ref[...], preferred_element_type=F32)
    tb = jnp.dot(yb_ref[...], wb_ref[...], preferred_element_type=F32)
    merged = (jax.nn.sigmoid(ga_ref[...].astype(F32)) * ta
              + jax.nn.sigmoid(gb_ref[...].astype(F32)) * tb)
    o_ref[...] = x_ref[...] + jnp.dot(merged.astype(BF16), wo_ref[...], preferred_element_type=F32)


def _resident(shape):
    return pl.BlockSpec(shape, lambda *_: (0,) * len(shape), pipeline_mode=pl.Buffered(1))


def _mixout(x2d, ya2d, yb2d, proj2d, wa, wb, wo):
    t = x2d.shape[0]
    tm = MIX_TM
    ga_col = (6 * W_HEADS) // D_MODEL
    gb_col = ga_col + 1
    return pl.pallas_call(
        _mixout_kernel,
        grid=(t // tm,),
        in_specs=[
            pl.BlockSpec((tm, D_MODEL), lambda i: (i, 0)),
            pl.BlockSpec((tm, W_HEADS), lambda i: (i, 0)),
            pl.BlockSpec((tm, W_HEADS), lambda i: (i, 0)),
            pl.BlockSpec((tm, D_MODEL), lambda i: (i, ga_col)),
            pl.BlockSpec((tm, D_MODEL), lambda i: (i, gb_col)),
            _resident((W_HEADS, D_MODEL)),
            _resident((W_HEADS, D_MODEL)),
            _resident((D_MODEL, D_MODEL)),
        ],
        out_specs=pl.BlockSpec((tm, D_MODEL), lambda i: (i, 0)),
        out_shape=jax.ShapeDtypeStruct((t, D_MODEL), F32),
        compiler_params=pltpu.CompilerParams(
            dimension_semantics=("arbitrary",),
            vmem_limit_bytes=48 * MIB),
        name="mixout",
    )(x2d, ya2d, yb2d, proj2d, proj2d, wa, wb, wo)


MLP_TM = 512
MLP_TF = 512


def _mlp_kernel(x_ref, g_ref, wu_ref, wd_ref, o_ref, hn_ref, acc_ref):
    f = pl.program_id(1)

    @pl.when(f == 0)
    def _():
        x = x_ref[...]
        hn_ref[...] = _rms_scale(x, g_ref[...]).astype(BF16)
        acc_ref[...] = x

    u = jnp.dot(hn_ref[...], wu_ref[...], preferred_element_type=F32)
    a = jnp.square(jnp.maximum(u, 0.0)).astype(BF16)
    acc_ref[...] += jnp.dot(a, wd_ref[...], preferred_element_type=F32)

    @pl.when(f == pl.num_programs(1) - 1)
    def _():
        o_ref[...] = acc_ref[...]


def _mlp(x2d, g, wu, wd):
    t = x2d.shape[0]
    tm, tf = MLP_TM, MLP_TF
    return pl.pallas_call(
        _mlp_kernel,
        grid=(t // tm, D_FF // tf),
        in_specs=[
            pl.BlockSpec((tm, D_MODEL), lambda i, f: (i, 0)),
            pl.BlockSpec((1, D_MODEL), lambda i, f: (0, 0)),
            pl.BlockSpec((D_MODEL, tf), lambda i, f: (0, f)),
            pl.BlockSpec((tf, D_MODEL), lambda i, f: (f, 0)),
        ],
        out_specs=pl.BlockSpec((tm, D_MODEL), lambda i, f: (i, 0)),
        out_shape=jax.ShapeDtypeStruct((t, D_MODEL), F32),
        scratch_shapes=[pltpu.VMEM((tm, D_MODEL), BF16), pltpu.VMEM((tm, D_MODEL), F32)],
        compiler_params=pltpu.CompilerParams(
            dimension_semantics=("arbitrary", "arbitrary"),
            vmem_limit_bytes=48 * MIB),
        name="mlp",
    )(x2d, g, wu, wd)


PLE_TM = 256


def _ple_kernel(x_ref, p_ref, g_ref, wg_ref, wp_ref, o_ref):
    x = x_ref[...]
    hn = _rms_scale(x, g_ref[...]).astype(BF16)
    gate = jnp.dot(hn, wg_ref[...], preferred_element_type=F32)
    pp = jnp.dot(p_ref[...].astype(BF16), wp_ref[...], preferred_element_type=F32)
    o_ref[...] = x + pp * jax.nn.sigmoid(gate)


def _ple(x2d, p2d, g, wg, wp):
    t = x2d.shape[0]
    tm = PLE_TM
    ple_dim = p2d.shape[1]
    return pl.pallas_call(
        _ple_kernel,
        grid=(t // tm,),
        in_specs=[
            pl.BlockSpec((tm, D_MODEL), lambda i: (i, 0)),
            pl.BlockSpec((tm, ple_dim), lambda i: (i, 0)),
            pl.BlockSpec((1, D_MODEL), lambda i: (0, 0)),
            _resident((D_MODEL, D_MODEL)),
            _resident((ple_dim, D_MODEL)),
        ],
        out_specs=pl.BlockSpec((tm, D_MODEL), lambda i: (i, 0)),
        out_shape=jax.ShapeDtypeStruct((t, D_MODEL), F32),
        compiler_params=pltpu.CompilerParams(
            dimension_semantics=("arbitrary",),
            vmem_limit_bytes=40 * MIB),
        name="ple",
    )(x2d, p2d, g, wg, wp)


def _rope_tables(seq):
    half = ROT_DIM // 2
    pos = jnp.arange(seq, dtype=F32)
    inv = ROPE_THETA ** (-jnp.arange(0, ROT_DIM, 2, dtype=F32) / ROT_DIM)
    ang = pos[:, None] * inv[None, :]
    cos, sin = jnp.cos(ang), jnp.sin(ang)
    rest = HEAD_DIM - ROT_DIM
    cos_t = jnp.concatenate([cos, cos, jnp.ones((seq, rest), F32)], axis=-1)
    zeros_h = jnp.zeros((seq, half), F32)
    zeros_r = jnp.zeros((seq, rest), F32)
    sa_t = jnp.concatenate([-sin, zeros_h, zeros_r], axis=-1)
    sb_t = jnp.concatenate([zeros_h, sin, zeros_r], axis=-1)
    return cos_t, sa_t, sb_t


def kernel(x, p, g_mix, w_in, qn_gain, kn_gain, w_branch_a, w_branch_b, w_out,
           g_mlp, w_up, w_down, g_ple, w_ple_gate, w_ple_proj):
    b, s, d = x.shape
    t = b * s
    depth = w_in.shape[0]
    cos_t, sa_t, sb_t = _rope_tables(s)
    x2d = x.reshape(t, d)
    for i in range(depth):
        proj2d = _inproj(x2d, g_mix[i][None, :], w_in[i].astype(BF16),
                         qn_gain[i][None, :], kn_gain[i][None, :], cos_t, sa_t, sb_t, s)
        proj3 = proj2d.reshape(b, s, D_IN)

        o, lse = None, None
        for n, (_, dilation) in enumerate(DIL_PATTERNS):
            o, lse = _dilated(proj3, dilation, o, lse, last=(n == len(DIL_PATTERNS) - 1))
        ya2d = o.reshape(t, W_HEADS)
        yb2d = _stickbreak(proj3).reshape(t, W_HEADS)

        x2d = _mixout(x2d, ya2d, yb2d, proj2d, w_branch_a[i].astype(BF16),
                      w_branch_b[i].astype(BF16), w_out[i].astype(BF16))
        x2d = _mlp(x2d, g_mlp[i][None, :], w_up[i].astype(BF16), w_down[i].astype(BF16))
        x2d = _ple(x2d, p[i].reshape(t, -1), g_ple[i][None, :],
                   w_ple_gate[i].astype(BF16), w_ple_proj[i].astype(BF16))
    return x2d.reshape(b, s, d)
```

```python
import jax
import jax.numpy as jnp
from jax import lax
from jax.experimental import pallas as pl
from jax.experimental.pallas import tpu as pltpu

D_MODEL = 2048
HEAD_DIM = 128
N_HEADS = 8
W_HEADS = N_HEADS * HEAD_DIM
DIL_PATTERNS = ((128, 1), (512, 4), (2048, 16))
BLOCK = 128
ROT_DIM = HEAD_DIM // 4
ROPE_THETA = 500000.0
D_FF = 4 * D_MODEL
EPS = 1e-6
D_IN = 6 * W_HEADS + 2 * D_MODEL
SCALE = HEAD_DIM ** -0.5

COL_QA, COL_KA, COL_VA, COL_QB, COL_KB, COL_VB = 0, 1, 2, 3, 4, 5
N_COL_TILES = D_IN // W_HEADS

F32 = jnp.float32
BF16 = jnp.bfloat16
NEG_INF = float("-inf")

MIB = 1024 * 1024


def _rms_scale(x, gain):
    ms = jnp.mean(x * x, axis=-1, keepdims=True)
    return x * lax.rsqrt(ms + EPS) * gain


INPROJ_TM = 512


FAR_DIL = DIL_PATTERNS[2][1]
PH_ROWS = INPROJ_TM // FAR_DIL
PH_HEAD = FAR_DIL * HEAD_DIM
PH_SEG = N_HEADS * PH_HEAD


def _inproj_kernel(x_ref, g_ref, w_ref, qn_ref, kn_ref, cos_ref, sa_ref, sb_ref,
                   o_ref, ph_ref, hn_ref, ys_ref):
    j = pl.program_id(1)

    @pl.when(j == 0)
    def _():
        hn_ref[...] = _rms_scale(x_ref[...], g_ref[...]).astype(BF16)

    acc = jnp.dot(hn_ref[...], w_ref[...], preferred_element_type=F32)

    def mixer_a_epilogue(gain_ref, scale):
        for h in range(N_HEADS):
            sl = slice(h * HEAD_DIM, (h + 1) * HEAD_DIM)
            y = acc[:, sl]
            if gain_ref is not None:
                y = _rms_scale(y, gain_ref[...])
                y = (y * cos_ref[...]
                     + pltpu.roll(y, HEAD_DIM - ROT_DIM // 2, 1) * sa_ref[...]
                     + pltpu.roll(y, ROT_DIM // 2, 1) * sb_ref[...])
            if scale is not None:
                y = y * scale
            o_ref[:, sl] = y.astype(BF16)
            ys_ref[h] = y
            for r in range(FAR_DIL):
                lanes = slice(h * PH_HEAD + r * HEAD_DIM, h * PH_HEAD + (r + 1) * HEAD_DIM)
                ph_ref[:, lanes] = ys_ref[h, pl.ds(r, PH_ROWS, stride=FAR_DIL), :].astype(BF16)

    @pl.when(j == COL_QA)
    def _():
        mixer_a_epilogue(qn_ref, SCALE)

    @pl.when(j == COL_KA)
    def _():
        mixer_a_epilogue(kn_ref, None)

    @pl.when(j == COL_VA)
    def _():
        mixer_a_epilogue(None, None)

    @pl.when(j == COL_QB)
    def _():
        o_ref[...] = (acc * SCALE).astype(BF16)

    @pl.when(j > COL_QB)
    def _():
        o_ref[...] = acc.astype(BF16)


def _inproj(x2d, g, w_bf16, qn, kn, cos_t, sa_t, sb_t, seq):
    t = x2d.shape[0]
    tm = INPROJ_TM
    nseq = seq // tm
    tab_spec = pl.BlockSpec((tm, HEAD_DIM), lambda i, j: (i % nseq, 0))
    vec_spec = pl.BlockSpec((1, HEAD_DIM), lambda i, j: (0, 0))
    return pl.pallas_call(
        _inproj_kernel,
        grid=(t // tm, N_COL_TILES),
        in_specs=[
            pl.BlockSpec((tm, D_MODEL), lambda i, j: (i, 0)),
            pl.BlockSpec((1, D_MODEL), lambda i, j: (0, 0)),
            pl.BlockSpec((D_MODEL, W_HEADS), lambda i, j: (0, j)),
            vec_spec, vec_spec, tab_spec, tab_spec, tab_spec,
        ],
        out_specs=[
            pl.BlockSpec((tm, W_HEADS), lambda i, j: (i, j)),
            pl.BlockSpec((PH_ROWS, PH_SEG), lambda i, j: (i, jnp.minimum(j, COL_VA))),
        ],
        out_shape=[
            jax.ShapeDtypeStruct((t, D_IN), BF16),
            jax.ShapeDtypeStruct((t // FAR_DIL, 3 * PH_SEG), BF16),
        ],
        scratch_shapes=[pltpu.VMEM((tm, D_MODEL), BF16), pltpu.VMEM((N_HEADS, tm, HEAD_DIM), F32)],
        compiler_params=pltpu.CompilerParams(
            dimension_semantics=("arbitrary", "arbitrary"),
            vmem_limit_bytes=40 * MIB),
        name="inproj",
    )(x2d, g, w_bf16, qn, kn, cos_t, sa_t, sb_t)


NEAR_QB = 512
NEAR_WIN = DIL_PATTERNS[1][0] + BLOCK
LOG2 = 0.6931471805599453


def _near_kernel(q_ref, kp_ref, k_ref, vp_ref, v_ref, o_ref, l_ref, kcat_ref, vcat_ref):
    first = pl.program_id(1) == 0
    kcat_ref[:NEAR_QB, :] = kp_ref[...]
    kcat_ref[NEAR_QB:, :] = k_ref[...]
    vcat_ref[:NEAR_QB, :] = vp_ref[...]
    vcat_ref[NEAR_QB:, :] = v_ref[...]

    qi = lax.broadcasted_iota(jnp.int32, (BLOCK, NEAR_WIN), 0)
    col = lax.broadcasted_iota(jnp.int32, (BLOCK, NEAR_WIN), 1)
    dist = qi + (NEAR_WIN - BLOCK) - col
    (w1, _), (w4, d4) = DIL_PATTERNS[0], DIL_PATTERNS[1]
    assert d4 & (d4 - 1) == 0
    in1 = (dist >= 0) & (dist <= w1)
    in4 = (dist >= 0) & (dist <= w4) & ((dist & (d4 - 1)) == 0)
    bias = jnp.where(in1 & in4, LOG2, jnp.where(in1 | in4, 0.0, NEG_INF))
    dn = (((1,), (1,)), ((), ()))

    for sb in range(NEAR_QB // BLOCK):
        rows = slice(sb * BLOCK, (sb + 1) * BLOCK)
        win = slice(sb * BLOCK, sb * BLOCK + NEAR_WIN)
        bias_sb = jnp.where(first & (col < NEAR_QB - sb * BLOCK), NEG_INF, bias)
        for h in range(N_HEADS):
            cols = slice(h * HEAD_DIM, (h + 1) * HEAD_DIM)
            s = lax.dot_general(q_ref[rows, cols], kcat_ref[win, cols], dn,
                                preferred_element_type=F32) + bias_sb
            m = jnp.max(s, axis=-1, keepdims=True)
            e = jnp.exp(s - m)
            den = jnp.sum(e, axis=-1, keepdims=True)
            o = jnp.dot(e.astype(BF16), vcat_ref[win, cols], preferred_element_type=F32) / den
            o_ref[rows, cols] = o
            l_ref[rows, cols] = jnp.broadcast_to(m + jnp.log(den), (BLOCK, HEAD_DIM))


def _near(proj3):
    b, s, _ = proj3.shape

    def cur(c):
        return lambda bi, qi: (bi, qi, c)

    def prev(c):
        return lambda bi, qi: (bi, jnp.maximum(qi - 1, 0), c)

    blk = (None, NEAR_QB, W_HEADS)
    out_spec = pl.BlockSpec(blk, lambda bi, qi: (bi, qi, 0))
    return pl.pallas_call(
        _near_kernel,
        grid=(b, s // NEAR_QB),
        in_specs=[
            pl.BlockSpec(blk, cur(COL_QA)),
            pl.BlockSpec(blk, prev(COL_KA)), pl.BlockSpec(blk, cur(COL_KA)),
            pl.BlockSpec(blk, prev(COL_VA)), pl.BlockSpec(blk, cur(COL_VA)),
        ],
        out_specs=[out_spec, out_spec],
        out_shape=[jax.ShapeDtypeStruct((b, s, W_HEADS), F32)] * 2,
        scratch_shapes=[pltpu.VMEM((2 * NEAR_QB, W_HEADS), BF16)] * 2,
        compiler_params=pltpu.CompilerParams(
            dimension_semantics=("arbitrary", "arbitrary"),
            vmem_limit_bytes=40 * MIB),
        name="dilated_near",
    )(proj3, proj3, proj3, proj3, proj3)


FAR_ROWS = BLOCK * FAR_DIL


def _far_kernel(q_ref, kp_ref, k_ref, vp_ref, v_ref, on_ref, ln_ref, y_ref):
    not_first = pl.program_id(1) > 0
    qi = lax.broadcasted_iota(jnp.int32, (BLOCK, BLOCK), 0)
    kj = lax.broadcasted_iota(jnp.int32, (BLOCK, BLOCK), 1)
    cur_ok = kj <= qi
    prev_ok = (kj >= qi) & not_first
    dn = (((1,), (1,)), ((), ()))

    for r in range(FAR_DIL):
        lanes = slice(r * HEAD_DIM, (r + 1) * HEAD_DIM)
        tokens = pl.ds(r, BLOCK, stride=FAR_DIL)
        q = q_ref[:, lanes]
        s_c = jnp.where(cur_ok, lax.dot_general(q, k_ref[:, lanes], dn, preferred_element_type=F32),
                        NEG_INF)
        s_p = jnp.where(prev_ok, lax.dot_general(q, kp_ref[:, lanes], dn, preferred_element_type=F32),
                        NEG_INF)
        m = jnp.maximum(jnp.max(s_c, axis=-1, keepdims=True), jnp.max(s_p, axis=-1, keepdims=True))
        e_c = jnp.exp(s_c - m)
        e_p = jnp.exp(s_p - m)
        den = jnp.sum(e_c, axis=-1, keepdims=True) + jnp.sum(e_p, axis=-1, keepdims=True)
        o = (jnp.dot(e_c.astype(BF16), v_ref[:, lanes], preferred_element_type=F32)
             + jnp.dot(e_p.astype(BF16), vp_ref[:, lanes], preferred_element_type=F32)) / den
        lse = m + jnp.log(den)
        l_near = ln_ref[tokens, :]
        mm = jnp.maximum(l_near, lse)
        w_near = jnp.exp(l_near - mm)
        w_far = jnp.exp(lse - mm)
        y_ref[tokens, :] = (on_ref[tokens, :] * w_near + o * w_far) / (w_near + w_far)


def _far(ph3, o_near, l_near):
    b, m, _ = ph3.shape
    s = m * FAR_DIL

    def cur(seg):
        return lambda bi, mi, h: (bi, mi, seg * N_HEADS + h)

    def prev(seg):
        return lambda bi, mi, h: (bi, jnp.maximum(mi - 1, 0), seg * N_HEADS + h)

    blk = (None, BLOCK, PH_HEAD)
    tok_spec = pl.BlockSpec((None, FAR_ROWS, HEAD_DIM), lambda bi, mi, h: (bi, mi, h))
    return pl.pallas_call(
        _far_kernel,
        grid=(b, m // BLOCK, N_HEADS),
        in_specs=[
            pl.BlockSpec(blk, cur(0)),
            pl.BlockSpec(blk, prev(1)), pl.BlockSpec(blk, cur(1)),
            pl.BlockSpec(blk, prev(2)), pl.BlockSpec(blk, cur(2)),
            tok_spec, tok_spec,
        ],
        out_specs=tok_spec,
        out_shape=jax.ShapeDtypeStruct((b, s, W_HEADS), F32),
        compiler_params=pltpu.CompilerParams(
            dimension_semantics=("arbitrary", "arbitrary", "arbitrary"),
            vmem_limit_bytes=32 * MIB),
        name="dilated_far",
    )(ph3, ph3, ph3, ph3, ph3, o_near, l_near)


SB_BLK = 256
SB_DONE = 110.0


def _sb_block(q, k, v, tot, later, causal):
    dn = (((1,), (1,)), ((), ()))
    z = lax.dot_general(q, k, dn, preferred_element_type=F32)
    sp = jnp.maximum(z, 0.0) + jnp.log(1.0 + jnp.exp(-jnp.abs(z)))
    if causal is not None:
        sp = jnp.where(causal, sp, 0.0)
    cs = jnp.dot(sp.astype(BF16), later, preferred_element_type=F32)
    arg = (z - sp) - cs - jnp.concatenate([tot, tot], axis=1)
    if causal is not None:
        arg = jnp.where(causal, arg, NEG_INF)
    pv = jnp.dot(jnp.exp(arg).astype(BF16), v, preferred_element_type=F32)
    rs = jnp.sum(sp, axis=1, keepdims=True)
    return pv, jnp.broadcast_to(rs, (SB_BLK, HEAD_DIM))


def _stickbreak_kernel(q_ref, k_ref, v_ref, o_ref, acc_ref, tot_ref):
    qi = pl.program_id(1)
    row = lax.broadcasted_iota(jnp.int32, (SB_BLK, SB_BLK), 0)
    col = lax.broadcasted_iota(jnp.int32, (SB_BLK, SB_BLK), 1)
    later = (row > col).astype(BF16)
    causal = col < row
    heads = [slice(h * HEAD_DIM, (h + 1) * HEAD_DIM) for h in range(N_HEADS)]

    def key_rows(kb):
        return pl.ds(pl.multiple_of(kb * SB_BLK, SB_BLK), SB_BLK)

    rows = key_rows(qi)
    zero = jnp.zeros((SB_BLK, HEAD_DIM), F32)
    for hs in heads:
        pv, rs = _sb_block(q_ref[:, hs], k_ref[rows, hs], v_ref[rows, hs], zero, later, causal)
        acc_ref[:, hs] = pv
        tot_ref[:, hs] = rs

    def earlier_block(kb):
        rows = key_rows(kb)
        for hs in heads:
            pv, rs = _sb_block(q_ref[:, hs], k_ref[rows, hs], v_ref[rows, hs],
                               tot_ref[:, hs], later, None)
            acc_ref[:, hs] += pv
            tot_ref[:, hs] += rs

    @pl.when(qi > 0)
    def _():
        earlier_block(qi - 1)

    def cond(c):
        kb, consumed = c
        return (kb >= 0) & (consumed < SB_DONE)

    def body(c):
        kb, _ = c
        earlier_block(kb)
        return kb - 1, jnp.min(tot_ref[...])

    lax.while_loop(cond, body, (qi - 2, jnp.min(tot_ref[...])))
    o_ref[...] = acc_ref[...].astype(o_ref.dtype)


def _stickbreak(proj3):
    b, s, _ = proj3.shape
    return pl.pallas_call(
        _stickbreak_kernel,
        grid=(b, s // SB_BLK),
        in_specs=[
            pl.BlockSpec((None, SB_BLK, W_HEADS), lambda bi, qi: (bi, qi, COL_QB)),
            pl.BlockSpec((None, s, W_HEADS), lambda bi, qi: (bi, 0, COL_KB),
                         pipeline_mode=pl.Buffered(1)),
            pl.BlockSpec((None, s, W_HEADS), lambda bi, qi: (bi, 0, COL_VB),
                         pipeline_mode=pl.Buffered(1)),
        ],
        out_specs=pl.BlockSpec((None, SB_BLK, W_HEADS), lambda bi, qi: (bi, qi, 0)),
        out_shape=jax.ShapeDtypeStruct((b, s, W_HEADS), BF16),
        scratch_shapes=[pltpu.VMEM((SB_BLK, W_HEADS), F32), pltpu.VMEM((SB_BLK, W_HEADS), F32)],
        compiler_params=pltpu.CompilerParams(
            dimension_semantics=("arbitrary", "arbitrary"),
            vmem_limit_bytes=48 * MIB),
        name="stickbreak",
    )(proj3, proj3, proj3)


MIX_TM = 256


def _mixout_kernel(x_ref, ya_ref, yb_ref, ga_ref, gb_ref, wa_ref, wb_ref, wo_ref, o_ref):
    ta = jnp.dot(ya_ref[...].astype(BF16), wa_ref[...], preferred_element_type=F32)
    tb = jnp.dot(yb_ref[...], wb_ref[...], preferred_element_type=F32)
    merged = (jax.nn.sigmoid(ga_ref[...].astype(F32)) * ta
              + jax.nn.sigmoid(gb_ref[...].astype(F32)) * tb)
    o_ref[...] = x_ref[...] + jnp.dot(merged.astype(BF16), wo_ref[...], preferred_element_type=F32)


def _resident(shape):
    return pl.BlockSpec(shape, lambda *_: (0,) * len(shape), pipeline_mode=pl.Buffered(1))


def _mixout(x2d, ya2d, yb2d, proj2d, wa, wb, wo):
    t = x2d.shape[0]
    tm = MIX_TM
    ga_col = (6 * W_HEADS) // D_MODEL
    gb_col = ga_col + 1
    return pl.pallas_call(
        _mixout_kernel,
        grid=(t // tm,),
        in_specs=[
            pl.BlockSpec((tm, D_MODEL), lambda i: (i, 0)),
            pl.BlockSpec((tm, W_HEADS), lambda i: (i, 0)),
            pl.BlockSpec((tm, W_HEADS), lambda i: (i, 0)),
            pl.BlockSpec((tm, D_MODEL), lambda i: (i, ga_col)),
            pl.BlockSpec((tm, D_MODEL), lambda i: (i, gb_col)),
            _resident((W_HEADS, D_MODEL)),
            _resident((W_HEADS, D_MODEL)),
            _resident((D_MODEL, D_MODEL)),
        ],
        out_specs=pl.BlockSpec((tm, D_MODEL), lambda i: (i, 0)),
        out_shape=jax.ShapeDtypeStruct((t, D_MODEL), F32),
        compiler_params=pltpu.CompilerParams(
            dimension_semantics=("arbitrary",),
            vmem_limit_bytes=48 * MIB),
        name="mixout",
    )(x2d, ya2d, yb2d, proj2d, proj2d, wa, wb, wo)


MLP_TM = 512
MLP_TF = 512


def _mlp_kernel(x_ref, g_ref, wu_ref, wd_ref, o_ref, hn_ref, acc_ref):
    f = pl.program_id(1)

    @pl.when(f == 0)
    def _():
        x = x_ref[...]
        hn_ref[...] = _rms_scale(x, g_ref[...]).astype(BF16)
        acc_ref[...] = x

    u = jnp.dot(hn_ref[...], wu_ref[...], preferred_element_type=F32)
    a = jnp.square(jnp.maximum(u, 0.0)).astype(BF16)
    acc_ref[...] += jnp.dot(a, wd_ref[...], preferred_element_type=F32)

    @pl.when(f == pl.num_programs(1) - 1)
    def _():
        o_ref[...] = acc_ref[...]


def _mlp(x2d, g, wu, wd):
    t = x2d.shape[0]
    tm, tf = MLP_TM, MLP_TF
    return pl.pallas_call(
        _mlp_kernel,
        grid=(t // tm, D_FF // tf),
        in_specs=[
            pl.BlockSpec((tm, D_MODEL), lambda i, f: (i, 0)),
            pl.BlockSpec((1, D_MODEL), lambda i, f: (0, 0)),
            pl.BlockSpec((D_MODEL, tf), lambda i, f: (0, f)),
            pl.BlockSpec((tf, D_MODEL), lambda i, f: (f, 0)),
        ],
        out_specs=pl.BlockSpec((tm, D_MODEL), lambda i, f: (i, 0)),
        out_shape=jax.ShapeDtypeStruct((t, D_MODEL), F32),
        scratch_shapes=[pltpu.VMEM((tm, D_MODEL), BF16), pltpu.VMEM((tm, D_MODEL), F32)],
        compiler_params=pltpu.CompilerParams(
            dimension_semantics=("arbitrary", "arbitrary"),
            vmem_limit_bytes=48 * MIB),
        name="mlp",
    )(x2d, g, wu, wd)


PLE_TM = 256


def _ple_kernel(x_ref, p_ref, g_ref, wg_ref, wp_ref, o_ref):
    x = x_ref[...]
    hn = _rms_scale(x, g_ref[...]).astype(BF16)
    gate = jnp.dot(hn, wg_ref[...], preferred_element_type=F32)
    pp = jnp.dot(p_ref[...].astype(BF16), wp_ref[...], preferred_element_type=F32)
    o_ref[...] = x + pp * jax.nn.sigmoid(gate)


def _ple(x2d, p2d, g, wg, wp):
    t = x2d.shape[0]
    tm = PLE_TM
    ple_dim = p2d.shape[1]
    return pl.pallas_call(
        _ple_kernel,
        grid=(t // tm,),
        in_specs=[
            pl.BlockSpec((tm, D_MODEL), lambda i: (i, 0)),
            pl.BlockSpec((tm, ple_dim), lambda i: (i, 0)),
            pl.BlockSpec((1, D_MODEL), lambda i: (0, 0)),
            _resident((D_MODEL, D_MODEL)),
            _resident((ple_dim, D_MODEL)),
        ],
        out_specs=pl.BlockSpec((tm, D_MODEL), lambda i: (i, 0)),
        out_shape=jax.ShapeDtypeStruct((t, D_MODEL), F32),
        compiler_params=pltpu.CompilerParams(
            dimension_semantics=("arbitrary",),
            vmem_limit_bytes=40 * MIB),
        name="ple",
    )(x2d, p2d, g, wg, wp)


def _rope_tables(seq):
    half = ROT_DIM // 2
    pos = jnp.arange(seq, dtype=F32)
    inv = ROPE_THETA ** (-jnp.arange(0, ROT_DIM, 2, dtype=F32) / ROT_DIM)
    ang = pos[:, None] * inv[None, :]
    cos, sin = jnp.cos(ang), jnp.sin(ang)
    rest = HEAD_DIM - ROT_DIM
    cos_t = jnp.concatenate([cos, cos, jnp.ones((seq, rest), F32)], axis=-1)
    zeros_h = jnp.zeros((seq, half), F32)
    zeros_r = jnp.zeros((seq, rest), F32)
    sa_t = jnp.concatenate([-sin, zeros_h, zeros_r], axis=-1)
    sb_t = jnp.concatenate([zeros_h, sin, zeros_r], axis=-1)
    return cos_t, sa_t, sb_t


def kernel(x, p, g_mix, w_in, qn_gain, kn_gain, w_branch_a, w_branch_b, w_out,
           g_mlp, w_up, w_down, g_ple, w_ple_gate, w_ple_proj):
    b, s, d = x.shape
    t = b * s
    depth = w_in.shape[0]
    cos_t, sa_t, sb_t = _rope_tables(s)
    x2d = x.reshape(t, d)
    for i in range(depth):
        proj2d, ph2d = _inproj(x2d, g_mix[i][None, :], w_in[i].astype(BF16),
                               qn_gain[i][None, :], kn_gain[i][None, :], cos_t, sa_t, sb_t, s)
        proj3 = proj2d.reshape(b, s, D_IN)

        o_near, l_near = _near(proj3)
        ya = _far(ph2d.reshape(b, s // FAR_DIL, 3 * PH_SEG), o_near, l_near)
        ya2d = ya.reshape(t, W_HEADS)
        yb2d = _stickbreak(proj3).reshape(t, W_HEADS)

        x2d = _mixout(x2d, ya2d, yb2d, proj2d, w_branch_a[i].astype(BF16),
                      w_branch_b[i].astype(BF16), w_out[i].astype(BF16))
        x2d = _mlp(x2d, g_mlp[i][None, :], w_up[i].astype(BF16), w_down[i].astype(BF16))
        x2d = _ple(x2d, p[i].reshape(t, -1), g_ple[i][None, :],
                   w_ple_gate[i].astype(BF16), w_ple_proj[i].astype(BF16))
    return x2d.reshape(b, s, d)
```

```python
import jax
import jax.numpy as jnp
from jax import lax
from jax.experimental import pallas as pl
from jax.experimental.pallas import tpu as pltpu

D_MODEL = 2048
HEAD_DIM = 128
N_HEADS = 8
W_HEADS = N_HEADS * HEAD_DIM
DIL_PATTERNS = ((128, 1), (512, 4), (2048, 16))
BLOCK = 128
ROT_DIM = HEAD_DIM // 4
ROPE_THETA = 500000.0
D_FF = 4 * D_MODEL
EPS = 1e-6
D_IN = 6 * W_HEADS + 2 * D_MODEL
LOG2E = 1.4426950408889634
Q_SCALE = HEAD_DIM ** -0.5 * LOG2E

COL_QA, COL_KA, COL_VA, COL_QB, COL_KB, COL_VB = 0, 1, 2, 3, 4, 5
N_COL_TILES = D_IN // W_HEADS

F32 = jnp.float32
BF16 = jnp.bfloat16
NEG_INF = float("-inf")

MIB = 1024 * 1024


def _rms_scale(x, gain):
    ms = jnp.mean(x * x, axis=-1, keepdims=True)
    return x * lax.rsqrt(ms + EPS) * gain


INPROJ_TM = 512


FAR_DIL = DIL_PATTERNS[2][1]
PH_ROWS = INPROJ_TM // FAR_DIL
PH_HEAD = FAR_DIL * HEAD_DIM
PH_SEG = N_HEADS * PH_HEAD


ROW_CHUNK = 128
SUB_DIL = 4
ROLL_DN = HEAD_DIM - ROT_DIM // 2
ROLL_UP = ROT_DIM // 2


N_MIXA = 3


def _inproj_kernel(x_ref, g_ref, w_ref, qn_ref, kn_ref, cos_ref, sa_ref, sb_ref,
                   o_ref, nat_ref, ph_ref, hn_ref, tab_ref, raw_ref, ys_ref, yq_ref):
    j = pl.program_id(1)
    tm = x_ref.shape[0]

    @pl.when(j == 0)
    def _():
        hn_ref[...] = _rms_scale(x_ref[...], g_ref[...]).astype(BF16)
        for n, (gain_ref, scale) in enumerate(((qn_ref, Q_SCALE), (kn_ref, 1.0))):
            g = jnp.broadcast_to(gain_ref[...] * scale, (tm, HEAD_DIM))
            tab_ref[3 * n] = cos_ref[...] * g
            tab_ref[3 * n + 1] = sa_ref[...] * pltpu.roll(g, ROLL_DN, 1)
            tab_ref[3 * n + 2] = sb_ref[...] * pltpu.roll(g, ROLL_UP, 1)

    def project():
        return jnp.dot(hn_ref[...], w_ref[...], preferred_element_type=F32)

    def finish_mixer_a(tile):
        slot = tile % 2
        tab = 3 * tile
        for h in range(N_HEADS):
            sl = slice(h * HEAD_DIM, (h + 1) * HEAD_DIM)
            for c in range(tm // ROW_CHUNK):
                rows = slice(c * ROW_CHUNK, (c + 1) * ROW_CHUNK)
                a = raw_ref[slot, rows, sl]
                if tile < N_MIXA - 1:
                    rs = lax.rsqrt(jnp.mean(a * a, axis=-1, keepdims=True) + EPS)
                    a = (a * tab_ref[tab, rows, :]
                         + pltpu.roll(a, ROLL_DN, 1) * tab_ref[tab + 1, rows, :]
                         + pltpu.roll(a, ROLL_UP, 1) * tab_ref[tab + 2, rows, :]) * rs
                nat_ref[rows, sl] = a.astype(BF16)
                ys_ref[h, rows, :] = a
            for r_lo in range(SUB_DIL):
                yq_ref[h, r_lo] = ys_ref[h, pl.ds(r_lo, tm // SUB_DIL, stride=SUB_DIL), :]
                for r_hi in range(FAR_DIL // SUB_DIL):
                    r = r_lo + SUB_DIL * r_hi
                    lanes = slice(h * PH_HEAD + r * HEAD_DIM, h * PH_HEAD + (r + 1) * HEAD_DIM)
                    ph_ref[:, lanes] = yq_ref[h, r_lo, pl.ds(r_hi, PH_ROWS, stride=SUB_DIL), :].astype(BF16)

    @pl.when(j == 0)
    def _():
        acc = project()
        raw_ref[0] = acc
        o_ref[...] = acc.astype(BF16)

    for tile in range(N_MIXA):
        @pl.when(j == tile + 1)
        def _(tile=tile):
            acc = project()
            if tile + 1 < N_MIXA:
                raw_ref[(tile + 1) % 2] = acc
            if tile + 1 == COL_QB:
                acc = acc * Q_SCALE
            o_ref[...] = acc.astype(BF16)
            finish_mixer_a(tile)

    @pl.when(j > N_MIXA)
    def _():
        o_ref[...] = project().astype(BF16)


def _inproj(x2d, g, w_bf16, qn, kn, cos_t, sa_t, sb_t, seq):
    t = x2d.shape[0]
    tm = INPROJ_TM
    nseq = seq // tm
    tab_spec = pl.BlockSpec((tm, HEAD_DIM), lambda i, j: (i % nseq, 0))
    vec_spec = pl.BlockSpec((1, HEAD_DIM), lambda i, j: (0, 0))
    return pl.pallas_call(
        _inproj_kernel,
        grid=(t // tm, N_COL_TILES),
        in_specs=[
            pl.BlockSpec((tm, D_MODEL), lambda i, j: (i, 0)),
            pl.BlockSpec((1, D_MODEL), lambda i, j: (0, 0)),
            pl.BlockSpec((D_MODEL, W_HEADS), lambda i, j: (0, j)),
            vec_spec, vec_spec, tab_spec, tab_spec, tab_spec,
        ],
        out_specs=[
            pl.BlockSpec((tm, W_HEADS), lambda i, j: (i, j)),
            pl.BlockSpec((tm, W_HEADS), lambda i, j: (i, jnp.clip(j - 1, 0, N_MIXA - 1))),
            pl.BlockSpec((PH_ROWS, PH_SEG), lambda i, j: (i, jnp.clip(j - 1, 0, N_MIXA - 1))),
        ],
        out_shape=[
            jax.ShapeDtypeStruct((t, D_IN), BF16),
            jax.ShapeDtypeStruct((t, N_MIXA * W_HEADS), BF16),
            jax.ShapeDtypeStruct((t // FAR_DIL, N_MIXA * PH_SEG), BF16),
        ],
        scratch_shapes=[
            pltpu.VMEM((tm, D_MODEL), BF16),
            pltpu.VMEM((6, tm, HEAD_DIM), F32),
            pltpu.VMEM((2, tm, W_HEADS), F32),
            pltpu.VMEM((N_HEADS, tm, HEAD_DIM), F32),
            pltpu.VMEM((N_HEADS, SUB_DIL, tm // SUB_DIL, HEAD_DIM), F32),
        ],
        compiler_params=pltpu.CompilerParams(
            dimension_semantics=("arbitrary", "arbitrary"),
            vmem_limit_bytes=48 * MIB),
        name="inproj",
    )(x2d, g, w_bf16, qn, kn, cos_t, sa_t, sb_t)


NEAR_QB = 512
NEAR_WIN = DIL_PATTERNS[1][0] + BLOCK


def _near_kernel(q_ref, kp_ref, k_ref, vp_ref, v_ref, o_ref, l_ref, kcat_ref, vcat_ref):
    first = pl.program_id(1) == 0
    kcat_ref[:NEAR_QB, :] = kp_ref[...]
    kcat_ref[NEAR_QB:, :] = k_ref[...]
    vcat_ref[:NEAR_QB, :] = vp_ref[...]
    vcat_ref[NEAR_QB:, :] = v_ref[...]

    qi = lax.broadcasted_iota(jnp.int32, (BLOCK, NEAR_WIN), 0)
    col = lax.broadcasted_iota(jnp.int32, (BLOCK, NEAR_WIN), 1)
    dist = qi + (NEAR_WIN - BLOCK) - col
    (w1, _), (w4, d4) = DIL_PATTERNS[0], DIL_PATTERNS[1]
    assert d4 & (d4 - 1) == 0
    in1 = (dist >= 0) & (dist <= w1)
    in4 = (dist >= 0) & (dist <= w4) & ((dist & (d4 - 1)) == 0)
    bias = jnp.where(in1 & in4, 1.0, jnp.where(in1 | in4, 0.0, NEG_INF))
    dn = (((1,), (1,)), ((), ()))

    for sb in range(NEAR_QB // BLOCK):
        rows = slice(sb * BLOCK, (sb + 1) * BLOCK)
        win = slice(sb * BLOCK, sb * BLOCK + NEAR_WIN)
        bias_sb = jnp.where(first & (col < NEAR_QB - sb * BLOCK), NEG_INF, bias)
        for h in range(N_HEADS):
            cols = slice(h * HEAD_DIM, (h + 1) * HEAD_DIM)
            s = lax.dot_general(q_ref[rows, cols], kcat_ref[win, cols], dn,
                                preferred_element_type=F32) + bias_sb
            m = jnp.max(s, axis=-1, keepdims=True)
            e = jnp.exp2(s - m)
            den = jnp.sum(e, axis=-1, keepdims=True)
            o = jnp.dot(e.astype(BF16), vcat_ref[win, cols], preferred_element_type=F32) / den
            o_ref[rows, cols] = o
            l_ref[rows, cols] = jnp.broadcast_to(m + jnp.log2(den), (BLOCK, HEAD_DIM))


def _near(nat3):
    b, s, _ = nat3.shape

    def cur(c):
        return lambda bi, qi: (bi, qi, c)

    def prev(c):
        return lambda bi, qi: (bi, jnp.maximum(qi - 1, 0), c)

    blk = (None, NEAR_QB, W_HEADS)
    out_spec = pl.BlockSpec(blk, lambda bi, qi: (bi, qi, 0))
    return pl.pallas_call(
        _near_kernel,
        grid=(b, s // NEAR_QB),
        in_specs=[
            pl.BlockSpec(blk, cur(COL_QA)),
            pl.BlockSpec(blk, prev(COL_KA)), pl.BlockSpec(blk, cur(COL_KA)),
            pl.BlockSpec(blk, prev(COL_VA)), pl.BlockSpec(blk, cur(COL_VA)),
        ],
        out_specs=[out_spec, out_spec],
        out_shape=[jax.ShapeDtypeStruct((b, s, W_HEADS), F32)] * 2,
        scratch_shapes=[pltpu.VMEM((2 * NEAR_QB, W_HEADS), BF16)] * 2,
        compiler_params=pltpu.CompilerParams(
            dimension_semantics=("arbitrary", "arbitrary"),
            vmem_limit_bytes=40 * MIB),
        name="dilated_near",
    )(nat3, nat3, nat3, nat3, nat3)


FAR_ROWS = BLOCK * FAR_DIL


def _far_kernel(q_ref, kp_ref, k_ref, vp_ref, v_ref, on_ref, ln_ref, y_ref):
    not_first = pl.program_id(1) > 0
    qi = lax.broadcasted_iota(jnp.int32, (BLOCK, BLOCK), 0)
    kj = lax.broadcasted_iota(jnp.int32, (BLOCK, BLOCK), 1)
    cur_ok = kj <= qi
    prev_ok = (kj >= qi) & not_first
    dn = (((1,), (1,)), ((), ()))

    for r in range(FAR_DIL):
        lanes = slice(r * HEAD_DIM, (r + 1) * HEAD_DIM)
        tokens = pl.ds(r, BLOCK, stride=FAR_DIL)
        q = q_ref[:, lanes]
        s_c = jnp.where(cur_ok, lax.dot_general(q, k_ref[:, lanes], dn, preferred_element_type=F32),
                        NEG_INF)
        s_p = jnp.where(prev_ok, lax.dot_general(q, kp_ref[:, lanes], dn, preferred_element_type=F32),
                        NEG_INF)
        m = jnp.maximum(jnp.max(s_c, axis=-1, keepdims=True), jnp.max(s_p, axis=-1, keepdims=True))
        e_c = jnp.exp2(s_c - m)
        e_p = jnp.exp2(s_p - m)
        den = jnp.sum(e_c, axis=-1, keepdims=True) + jnp.sum(e_p, axis=-1, keepdims=True)
        o = (jnp.dot(e_c.astype(BF16), v_ref[:, lanes], preferred_element_type=F32)
             + jnp.dot(e_p.astype(BF16), vp_ref[:, lanes], preferred_element_type=F32)) / den
        lse = m + jnp.log2(den)
        l_near = ln_ref[tokens, :]
        mm = jnp.maximum(l_near, lse)
        w_near = jnp.exp2(l_near - mm)
        w_far = jnp.exp2(lse - mm)
        y_ref[tokens, :] = (on_ref[tokens, :] * w_near + o * w_far) / (w_near + w_far)


def _far(ph3, o_near, l_near):
    b, m, _ = ph3.shape
    s = m * FAR_DIL

    def cur(seg):
        return lambda bi, mi, h: (bi, mi, seg * N_HEADS + h)

    def prev(seg):
        return lambda bi, mi, h: (bi, jnp.maximum(mi - 1, 0), seg * N_HEADS + h)

    blk = (None, BLOCK, PH_HEAD)
    tok_spec = pl.BlockSpec((None, FAR_ROWS, HEAD_DIM), lambda bi, mi, h: (bi, mi, h))
    return pl.pallas_call(
        _far_kernel,
        grid=(b, m // BLOCK, N_HEADS),
        in_specs=[
            pl.BlockSpec(blk, cur(0)),
            pl.BlockSpec(blk, prev(1)), pl.BlockSpec(blk, cur(1)),
            pl.BlockSpec(blk, prev(2)), pl.BlockSpec(blk, cur(2)),
            tok_spec, tok_spec,
        ],
        out_specs=tok_spec,
        out_shape=jax.ShapeDtypeStruct((b, s, W_HEADS), F32),
        compiler_params=pltpu.CompilerParams(
            dimension_semantics=("arbitrary", "arbitrary", "arbitrary"),
            vmem_limit_bytes=32 * MIB),
        name="dilated_far",
    )(ph3, ph3, ph3, ph3, ph3, o_near, l_near)


SB_BLK = 256
SB_DONE = 160.0


def _sb_block(q, k, v, tot, later, causal):
    dn = (((1,), (1,)), ((), ()))
    z = lax.dot_general(q, k, dn, preferred_element_type=F32)
    sp = jnp.maximum(z, 0.0) + jnp.log2(1.0 + jnp.exp2(-jnp.abs(z)))
    if causal is not None:
        sp = jnp.where(causal, sp, 0.0)
    cs = jnp.dot(sp.astype(BF16), later, preferred_element_type=F32)
    arg = (z - sp) - cs - jnp.concatenate([tot, tot], axis=1)
    if causal is not None:
        arg = jnp.where(causal, arg, NEG_INF)
    pv = jnp.dot(jnp.exp2(arg).astype(BF16), v, preferred_element_type=F32)
    rs = jnp.sum(sp, axis=1, keepdims=True)
    return pv, jnp.broadcast_to(rs, (SB_BLK, HEAD_DIM))


def _stickbreak_kernel(q_ref, k_ref, v_ref, o_ref, acc_ref, tot_ref):
    qi = pl.program_id(1)
    row = lax.broadcasted_iota(jnp.int32, (SB_BLK, SB_BLK), 0)
    col = lax.broadcasted_iota(jnp.int32, (SB_BLK, SB_BLK), 1)
    later = (row > col).astype(BF16)
    causal = col < row
    heads = [slice(h * HEAD_DIM, (h + 1) * HEAD_DIM) for h in range(N_HEADS)]

    def key_rows(kb):
        return pl.ds(pl.multiple_of(kb * SB_BLK, SB_BLK), SB_BLK)

    rows = key_rows(qi)
    zero = jnp.zeros((SB_BLK, HEAD_DIM), F32)
    for hs in heads:
        pv, rs = _sb_block(q_ref[:, hs], k_ref[rows, hs], v_ref[rows, hs], zero, later, causal)
        acc_ref[:, hs] = pv
        tot_ref[:, hs] = rs

    def earlier_block(kb):
        rows = key_rows(kb)
        for hs in heads:
            pv, rs = _sb_block(q_ref[:, hs], k_ref[rows, hs], v_ref[rows, hs],
                               tot_ref[:, hs], later, None)
            acc_ref[:, hs] += pv
            tot_ref[:, hs] += rs

    @pl.when(qi > 0)
    def _():
        earlier_block(qi - 1)

    def cond(c):
        kb, consumed = c
        return (kb >= 0) & (consumed < SB_DONE)

    def body(c):
        kb, _ = c
        earlier_block(kb)
        return kb - 1, jnp.min(tot_ref[...])

    lax.while_loop(cond, body, (qi - 2, jnp.min(tot_ref[...])))
    o_ref[...] = acc_ref[...].astype(o_ref.dtype)


def _stickbreak(proj3):
    b, s, _ = proj3.shape
    return pl.pallas_call(
        _stickbreak_kernel,
        grid=(b, s // SB_BLK),
        in_specs=[
            pl.BlockSpec((None, SB_BLK, W_HEADS), lambda bi, qi: (bi, qi, COL_QB)),
            pl.BlockSpec((None, s, W_HEADS), lambda bi, qi: (bi, 0, COL_KB),
                         pipeline_mode=pl.Buffered(1)),
            pl.BlockSpec((None, s, W_HEADS), lambda bi, qi: (bi, 0, COL_VB),
                         pipeline_mode=pl.Buffered(1)),
        ],
        out_specs=pl.BlockSpec((None, SB_BLK, W_HEADS), lambda bi, qi: (bi, qi, 0)),
        out_shape=jax.ShapeDtypeStruct((b, s, W_HEADS), BF16),
        scratch_shapes=[pltpu.VMEM((SB_BLK, W_HEADS), F32), pltpu.VMEM((SB_BLK, W_HEADS), F32)],
        compiler_params=pltpu.CompilerParams(
            dimension_semantics=("arbitrary", "arbitrary"),
            vmem_limit_bytes=48 * MIB),
        name="stickbreak",
    )(proj3, proj3, proj3)


MIX_TM = 256


def _mixout_kernel(x_ref, ya_ref, yb_ref, ga_ref, gb_ref, wa_ref, wb_ref, wo_ref, o_ref):
    ta = jnp.dot(ya_ref[...].astype(BF16), wa_ref[...], preferred_element_type=F32)
    tb = jnp.dot(yb_ref[...], wb_ref[...], preferred_element_type=F32)
    merged = (jax.nn.sigmoid(ga_ref[...].astype(F32)) * ta
              + jax.nn.sigmoid(gb_ref[...].astype(F32)) * tb)
    o_ref[...] = x_ref[...] + jnp.dot(merged.astype(BF16), wo_ref[...], preferred_element_type=F32)


def _resident(shape):
    return pl.BlockSpec(shape, lambda *_: (0,) * len(shape), pipeline_mode=pl.Buffered(1))


def _mixout(x2d, ya2d, yb2d, proj2d, wa, wb, wo):
    t = x2d.shape[0]
    tm = MIX_TM
    ga_col = (6 * W_HEADS) // D_MODEL
    gb_col = ga_col + 1
    return pl.pallas_call(
        _mixout_kernel,
        grid=(t // tm,),
        in_specs=[
            pl.BlockSpec((tm, D_MODEL), lambda i: (i, 0)),
            pl.BlockSpec((tm, W_HEADS), lambda i: (i, 0)),
            pl.BlockSpec((tm, W_HEADS), lambda i: (i, 0)),
            pl.BlockSpec((tm, D_MODEL), lambda i: (i, ga_col)),
            pl.BlockSpec((tm, D_MODEL), lambda i: (i, gb_col)),
            _resident((W_HEADS, D_MODEL)),
            _resident((W_HEADS, D_MODEL)),
            _resident((D_MODEL, D_MODEL)),
        ],
        out_specs=pl.BlockSpec((tm, D_MODEL), lambda i: (i, 0)),
        out_shape=jax.ShapeDtypeStruct((t, D_MODEL), F32),
        compiler_params=pltpu.CompilerParams(
            dimension_semantics=("arbitrary",),
            vmem_limit_bytes=48 * MIB),
        name="mixout",
    )(x2d, ya2d, yb2d, proj2d, proj2d, wa, wb, wo)


MLP_TM = 512
MLP_TF = 512


def _mlp_kernel(x_ref, g_ref, wu_ref, wd_ref, o_ref, hn_ref, acc_ref):
    f = pl.program_id(1)

    @pl.when(f == 0)
    def _():
        x = x_ref[...]
        hn_ref[...] = _rms_scale(x, g_ref[...]).astype(BF16)
        acc_ref[...] = x

    u = jnp.dot(hn_ref[...], wu_ref[...], preferred_element_type=F32)
    a = jnp.square(jnp.maximum(u, 0.0)).astype(BF16)
    acc_ref[...] += jnp.dot(a, wd_ref[...], preferred_element_type=F32)

    @pl.when(f == pl.num_programs(1) - 1)
    def _():
        o_ref[...] = acc_ref[...]


def _mlp(x2d, g, wu, wd):
    t = x2d.shape[0]
    tm, tf = MLP_TM, MLP_TF
    return pl.pallas_call(
        _mlp_kernel,
        grid=(t // tm, D_FF // tf),
        in_specs=[
            pl.BlockSpec((tm, D_MODEL), lambda i, f: (i, 0)),
            pl.BlockSpec((1, D_MODEL), lambda i, f: (0, 0)),
            pl.BlockSpec((D_MODEL, tf), lambda i, f: (0, f)),
            pl.BlockSpec((tf, D_MODEL), lambda i, f: (f, 0)),
        ],
        out_specs=pl.BlockSpec((tm, D_MODEL), lambda i, f: (i, 0)),
        out_shape=jax.ShapeDtypeStruct((t, D_MODEL), F32),
        scratch_shapes=[pltpu.VMEM((tm, D_MODEL), BF16), pltpu.VMEM((tm, D_MODEL), F32)],
        compiler_params=pltpu.CompilerParams(
            dimension_semantics=("arbitrary", "arbitrary"),
            vmem_limit_bytes=48 * MIB),
        name="mlp",
    )(x2d, g, wu, wd)


PLE_TM = 256


def _ple_kernel(x_ref, p_ref, g_ref, wg_ref, wp_ref, o_ref):
    x = x_ref[...]
    hn = _rms_scale(x, g_ref[...]).astype(BF16)
    gate = jnp.dot(hn, wg_ref[...], preferred_element_type=F32)
    pp = jnp.dot(p_ref[...].astype(BF16), wp_ref[...], preferred_element_type=F32)
    o_ref[...] = x + pp * jax.nn.sigmoid(gate)


def _ple(x2d, p2d, g, wg, wp):
    t = x2d.shape[0]
    tm = PLE_TM
    ple_dim = p2d.shape[1]
    return pl.pallas_call(
        _ple_kernel,
        grid=(t // tm,),
        in_specs=[
            pl.BlockSpec((tm, D_MODEL), lambda i: (i, 0)),
            pl.BlockSpec((tm, ple_dim), lambda i: (i, 0)),
            pl.BlockSpec((1, D_MODEL), lambda i: (0, 0)),
            _resident((D_MODEL, D_MODEL)),
            _resident((ple_dim, D_MODEL)),
        ],
        out_specs=pl.BlockSpec((tm, D_MODEL), lambda i: (i, 0)),
        out_shape=jax.ShapeDtypeStruct((t, D_MODEL), F32),
        compiler_params=pltpu.CompilerParams(
            dimension_semantics=("arbitrary",),
            vmem_limit_bytes=40 * MIB),
        name="ple",
    )(x2d, p2d, g, wg, wp)


def _rope_tables(seq):
    half = ROT_DIM // 2
    pos = jnp.arange(seq, dtype=F32)
    inv = ROPE_THETA ** (-jnp.arange(0, ROT_DIM, 2, dtype=F32) / ROT_DIM)
    ang = pos[:, None] * inv[None, :]
    cos, sin = jnp.cos(ang), jnp.sin(ang)
    rest = HEAD_DIM - ROT_DIM
    cos_t = jnp.concatenate([cos, cos, jnp.ones((seq, rest), F32)], axis=-1)
    zeros_h = jnp.zeros((seq, half), F32)
    zeros_r = jnp.zeros((seq, rest), F32)
    sa_t = jnp.concatenate([-sin, zeros_h, zeros_r], axis=-1)
    sb_t = jnp.concatenate([zeros_h, sin, zeros_r], axis=-1)
    return cos_t, sa_t, sb_t


def kernel(x, p, g_mix, w_in, qn_gain, kn_gain, w_branch_a, w_branch_b, w_out,
           g_mlp, w_up, w_down, g_ple, w_ple_gate, w_ple_proj):
    b, s, d = x.shape
    t = b * s
    depth = w_in.shape[0]
    cos_t, sa_t, sb_t = _rope_tables(s)
    x2d = x.reshape(t, d)
    for i in range(depth):
        proj2d, nat2d, ph2d = _inproj(x2d, g_mix[i][None, :], w_in[i].astype(BF16),
                                      qn_gain[i][None, :], kn_gain[i][None, :], cos_t, sa_t, sb_t, s)
        proj3 = proj2d.reshape(b, s, D_IN)

        o_near, l_near = _near(nat2d.reshape(b, s, N_MIXA * W_HEADS))
        ya = _far(ph2d.reshape(b, s // FAR_DIL, N_MIXA * PH_SEG), o_near, l_near)
        ya2d = ya.reshape(t, W_HEADS)
        yb2d = _stickbreak(proj3).reshape(t, W_HEADS)

        x2d = _mixout(x2d, ya2d, yb2d, proj2d, w_branch_a[i].astype(BF16),
                      w_branch_b[i].astype(BF16), w_out[i].astype(BF16))
        x2d = _mlp(x2d, g_mlp[i][None, :], w_up[i].astype(BF16), w_down[i].astype(BF16))
        x2d = _ple(x2d, p[i].reshape(t, -1), g_ple[i][None, :],
                   w_ple_gate[i].astype(BF16), w_ple_proj[i].astype(BF16))
    return x2d.reshape(b, s, d)
```

```python
import jax
import jax.numpy as jnp
from jax import lax
from jax.experimental import pallas as pl
from jax.experimental.pallas import tpu as pltpu

D_MODEL = 2048
HEAD_DIM = 128
N_HEADS = 8
W_HEADS = N_HEADS * HEAD_DIM
DIL_PATTERNS = ((128, 1), (512, 4), (2048, 16))
BLOCK = 128
ROT_DIM = HEAD_DIM // 4
ROPE_THETA = 500000.0
D_FF = 4 * D_MODEL
EPS = 1e-6
D_IN = 6 * W_HEADS + 2 * D_MODEL
LOG2E = 1.4426950408889634
Q_SCALE = HEAD_DIM ** -0.5 * LOG2E

COL_QA, COL_KA, COL_VA, COL_QB, COL_KB, COL_VB = 0, 1, 2, 3, 4, 5
N_COL_TILES = D_IN // W_HEADS

F32 = jnp.float32
BF16 = jnp.bfloat16
NEG_INF = float("-inf")

MIB = 1024 * 1024


def _rms_scale(x, gain):
    ms = jnp.mean(x * x, axis=-1, keepdims=True)
    return x * lax.rsqrt(ms + EPS) * gain


INPROJ_TM = 512


FAR_DIL = DIL_PATTERNS[2][1]
PH_ROWS = INPROJ_TM // FAR_DIL
PH_HEAD = FAR_DIL * HEAD_DIM
PH_SEG = N_HEADS * PH_HEAD


ROW_CHUNK = 128
SUB_DIL = 4
ROLL_DN = HEAD_DIM - ROT_DIM // 2
ROLL_UP = ROT_DIM // 2


N_MIXA = 3


def _inproj_kernel(x_ref, g_ref, w_ref, qn_ref, kn_ref, cos_ref, sa_ref, sb_ref,
                   o_ref, nat_ref, ph_ref, hn_ref, tab_ref, raw_ref, ys_ref, yq_ref):
    j = pl.program_id(1)
    tm = x_ref.shape[0]

    @pl.when(j == 0)
    def _():
        hn_ref[...] = _rms_scale(x_ref[...], g_ref[...]).astype(BF16)
        for n, (gain_ref, scale) in enumerate(((qn_ref, Q_SCALE), (kn_ref, 1.0))):
            g = jnp.broadcast_to(gain_ref[...] * scale, (tm, HEAD_DIM))
            tab_ref[3 * n] = cos_ref[...] * g
            tab_ref[3 * n + 1] = sa_ref[...] * pltpu.roll(g, ROLL_DN, 1)
            tab_ref[3 * n + 2] = sb_ref[...] * pltpu.roll(g, ROLL_UP, 1)

    def project():
        return jnp.dot(hn_ref[...], w_ref[...], preferred_element_type=F32)

    def finish_mixer_a(tile):
        slot = tile % 2
        tab = 3 * tile
        for h in range(N_HEADS):
            sl = slice(h * HEAD_DIM, (h + 1) * HEAD_DIM)
            for c in range(tm // ROW_CHUNK):
                rows = slice(c * ROW_CHUNK, (c + 1) * ROW_CHUNK)
                a = raw_ref[slot, rows, sl]
                if tile < N_MIXA - 1:
                    rs = lax.rsqrt(jnp.mean(a * a, axis=-1, keepdims=True) + EPS)
                    a = (a * tab_ref[tab, rows, :]
                         + pltpu.roll(a, ROLL_DN, 1) * tab_ref[tab + 1, rows, :]
                         + pltpu.roll(a, ROLL_UP, 1) * tab_ref[tab + 2, rows, :]) * rs
                nat_ref[rows, sl] = a.astype(BF16)
                ys_ref[h, rows, :] = a
            for r_lo in range(SUB_DIL):
                yq_ref[h, r_lo] = ys_ref[h, pl.ds(r_lo, tm // SUB_DIL, stride=SUB_DIL), :]
                for r_hi in range(FAR_DIL // SUB_DIL):
                    r = r_lo + SUB_DIL * r_hi
                    lanes = slice(h * PH_HEAD + r * HEAD_DIM, h * PH_HEAD + (r + 1) * HEAD_DIM)
                    ph_ref[:, lanes] = yq_ref[h, r_lo, pl.ds(r_hi, PH_ROWS, stride=SUB_DIL), :].astype(BF16)

    @pl.when(j == 0)
    def _():
        acc = project()
        raw_ref[0] = acc
        o_ref[...] = acc.astype(BF16)

    for tile in range(N_MIXA):
        @pl.when(j == tile + 1)
        def _(tile=tile):
            acc = project()
            if tile + 1 < N_MIXA:
                raw_ref[(tile + 1) % 2] = acc
            if tile + 1 == COL_QB:
                acc = acc * Q_SCALE
            o_ref[...] = acc.astype(BF16)
            finish_mixer_a(tile)

    @pl.when(j > N_MIXA)
    def _():
        o_ref[...] = project().astype(BF16)


def _inproj(x2d, g, w_bf16, qn, kn, cos_t, sa_t, sb_t, seq):
    t = x2d.shape[0]
    tm = INPROJ_TM
    nseq = seq // tm
    tab_spec = pl.BlockSpec((tm, HEAD_DIM), lambda i, j: (i % nseq, 0))
    vec_spec = pl.BlockSpec((1, HEAD_DIM), lambda i, j: (0, 0))
    return pl.pallas_call(
        _inproj_kernel,
        grid=(t // tm, N_COL_TILES),
        in_specs=[
            pl.BlockSpec((tm, D_MODEL), lambda i, j: (i, 0)),
            pl.BlockSpec((1, D_MODEL), lambda i, j: (0, 0)),
            pl.BlockSpec((D_MODEL, W_HEADS), lambda i, j: (0, j)),
            vec_spec, vec_spec, tab_spec, tab_spec, tab_spec,
        ],
        out_specs=[
            pl.BlockSpec((tm, W_HEADS), lambda i, j: (i, j)),
            pl.BlockSpec((tm, W_HEADS), lambda i, j: (i, jnp.clip(j - 1, 0, N_MIXA - 1))),
            pl.BlockSpec((PH_ROWS, PH_SEG), lambda i, j: (i, jnp.clip(j - 1, 0, N_MIXA - 1))),
        ],
        out_shape=[
            jax.ShapeDtypeStruct((t, D_IN), BF16),
            jax.ShapeDtypeStruct((t, N_MIXA * W_HEADS), BF16),
            jax.ShapeDtypeStruct((t // FAR_DIL, N_MIXA * PH_SEG), BF16),
        ],
        scratch_shapes=[
            pltpu.VMEM((tm, D_MODEL), BF16),
            pltpu.VMEM((6, tm, HEAD_DIM), F32),
            pltpu.VMEM((2, tm, W_HEADS), F32),
            pltpu.VMEM((N_HEADS, tm, HEAD_DIM), F32),
            pltpu.VMEM((N_HEADS, SUB_DIL, tm // SUB_DIL, HEAD_DIM), F32),
        ],
        compiler_params=pltpu.CompilerParams(
            dimension_semantics=("arbitrary", "arbitrary"),
            vmem_limit_bytes=48 * MIB),
        name="inproj",
    )(x2d, g, w_bf16, qn, kn, cos_t, sa_t, sb_t)


NEAR_QB = 512
NEAR_SUB = 256
NEAR_WIN = DIL_PATTERNS[1][0] + NEAR_SUB


def _near_kernel(q_ref, kp_ref, k_ref, vp_ref, v_ref, o_ref, l_ref, kcat_ref, vcat_ref):
    first = pl.program_id(1) == 0
    kcat_ref[:NEAR_QB, :] = kp_ref[...]
    kcat_ref[NEAR_QB:, :] = k_ref[...]
    vcat_ref[:NEAR_QB, :] = vp_ref[...]
    vcat_ref[NEAR_QB:, :] = v_ref[...]

    qi = lax.broadcasted_iota(jnp.int32, (NEAR_SUB, NEAR_WIN), 0)
    col = lax.broadcasted_iota(jnp.int32, (NEAR_SUB, NEAR_WIN), 1)
    dist = qi + (NEAR_WIN - NEAR_SUB) - col
    (w1, _), (w4, d4) = DIL_PATTERNS[0], DIL_PATTERNS[1]
    assert d4 & (d4 - 1) == 0
    in1 = (dist >= 0) & (dist <= w1)
    in4 = (dist >= 0) & (dist <= w4) & ((dist & (d4 - 1)) == 0)
    bias = jnp.where(in1 & in4, 1.0, jnp.where(in1 | in4, 0.0, NEG_INF))
    dn = (((1,), (1,)), ((), ()))

    for sb in range(NEAR_QB // NEAR_SUB):
        rows = slice(sb * NEAR_SUB, (sb + 1) * NEAR_SUB)
        win = slice(sb * NEAR_SUB, sb * NEAR_SUB + NEAR_WIN)
        bias_sb = jnp.where(first & (col < NEAR_QB - sb * NEAR_SUB), NEG_INF, bias)
        for h in range(N_HEADS):
            cols = slice(h * HEAD_DIM, (h + 1) * HEAD_DIM)
            s = lax.dot_general(q_ref[rows, cols], kcat_ref[win, cols], dn,
                                preferred_element_type=F32) + bias_sb
            m = jnp.max(s, axis=-1, keepdims=True)
            e = jnp.exp2(s - m)
            den = jnp.sum(e, axis=-1, keepdims=True)
            o = jnp.dot(e.astype(BF16), vcat_ref[win, cols], preferred_element_type=F32) / den
            o_ref[rows, cols] = o
            l_ref[rows, cols] = jnp.broadcast_to(m + jnp.log2(den), (NEAR_SUB, HEAD_DIM))


def _near(nat3):
    b, s, _ = nat3.shape

    def cur(c):
        return lambda bi, qi: (bi, qi, c)

    def prev(c):
        return lambda bi, qi: (bi, jnp.maximum(qi - 1, 0), c)

    blk = (None, NEAR_QB, W_HEADS)
    out_spec = pl.BlockSpec(blk, lambda bi, qi: (bi, qi, 0))
    return pl.pallas_call(
        _near_kernel,
        grid=(b, s // NEAR_QB),
        in_specs=[
            pl.BlockSpec(blk, cur(COL_QA)),
            pl.BlockSpec(blk, prev(COL_KA)), pl.BlockSpec(blk, cur(COL_KA)),
            pl.BlockSpec(blk, prev(COL_VA)), pl.BlockSpec(blk, cur(COL_VA)),
        ],
        out_specs=[out_spec, out_spec],
        out_shape=[jax.ShapeDtypeStruct((b, s, W_HEADS), F32)] * 2,
        scratch_shapes=[pltpu.VMEM((2 * NEAR_QB, W_HEADS), BF16)] * 2,
        compiler_params=pltpu.CompilerParams(
            dimension_semantics=("arbitrary", "arbitrary"),
            vmem_limit_bytes=40 * MIB),
        name="dilated_near",
    )(nat3, nat3, nat3, nat3, nat3)


FAR_ROWS = BLOCK * FAR_DIL


def _far_kernel(q_ref, kp_ref, k_ref, vp_ref, v_ref, on_ref, ln_ref, y_ref, kcat_ref, vcat_ref):
    first = pl.program_id(1) == 0
    kcat_ref[:BLOCK, :] = kp_ref[...]
    kcat_ref[BLOCK:, :] = k_ref[...]
    vcat_ref[:BLOCK, :] = vp_ref[...]
    vcat_ref[BLOCK:, :] = v_ref[...]
    qi = lax.broadcasted_iota(jnp.int32, (BLOCK, 2 * BLOCK), 0)
    col = lax.broadcasted_iota(jnp.int32, (BLOCK, 2 * BLOCK), 1)
    dist = qi + BLOCK - col
    bias = jnp.where((dist >= 0) & (dist <= BLOCK), 0.0, NEG_INF)
    bias = jnp.where(first & (col < BLOCK), NEG_INF, bias)
    dn = (((1,), (1,)), ((), ()))

    for r in range(FAR_DIL):
        lanes = slice(r * HEAD_DIM, (r + 1) * HEAD_DIM)
        tokens = pl.ds(r, BLOCK, stride=FAR_DIL)
        s = lax.dot_general(q_ref[:, lanes], kcat_ref[:, lanes], dn, preferred_element_type=F32) + bias
        m = jnp.max(s, axis=-1, keepdims=True)
        e = jnp.exp2(s - m)
        den = jnp.sum(e, axis=-1, keepdims=True)
        o = jnp.dot(e.astype(BF16), vcat_ref[:, lanes], preferred_element_type=F32) / den
        lse = m + jnp.log2(den)
        l_near = ln_ref[tokens, :]
        mm = jnp.maximum(l_near, lse)
        w_near = jnp.exp2(l_near - mm)
        w_far = jnp.exp2(lse - mm)
        y_ref[tokens, :] = (on_ref[tokens, :] * w_near + o * w_far) / (w_near + w_far)


def _far(ph3, o_near, l_near):
    b, m, _ = ph3.shape
    s = m * FAR_DIL

    def cur(seg):
        return lambda bi, mi, h: (bi, mi, seg * N_HEADS + h)

    def prev(seg):
        return lambda bi, mi, h: (bi, jnp.maximum(mi - 1, 0), seg * N_HEADS + h)

    blk = (None, BLOCK, PH_HEAD)
    tok_spec = pl.BlockSpec((None, FAR_ROWS, HEAD_DIM), lambda bi, mi, h: (bi, mi, h))
    return pl.pallas_call(
        _far_kernel,
        grid=(b, m // BLOCK, N_HEADS),
        in_specs=[
            pl.BlockSpec(blk, cur(0)),
            pl.BlockSpec(blk, prev(1)), pl.BlockSpec(blk, cur(1)),
            pl.BlockSpec(blk, prev(2)), pl.BlockSpec(blk, cur(2)),
            tok_spec, tok_spec,
        ],
        out_specs=tok_spec,
        out_shape=jax.ShapeDtypeStruct((b, s, W_HEADS), F32),
        scratch_shapes=[pltpu.VMEM((2 * BLOCK, PH_HEAD), BF16)] * 2,
        compiler_params=pltpu.CompilerParams(
            dimension_semantics=("arbitrary", "arbitrary", "arbitrary"),
            vmem_limit_bytes=32 * MIB),
        name="dilated_far",
    )(ph3, ph3, ph3, ph3, ph3, o_near, l_near)


SB_BLK = 256
SB_DONE = 160.0


def _sb_block(q, k, v, tot, later, causal):
    dn = (((1,), (1,)), ((), ()))
    z = lax.dot_general(q, k, dn, preferred_element_type=F32)
    sp = jnp.maximum(z, 0.0) + jnp.log2(1.0 + jnp.exp2(-jnp.abs(z)))
    if causal is not None:
        sp = jnp.where(causal, sp, 0.0)
    cs = jnp.dot(sp.astype(BF16), later, preferred_element_type=F32)
    arg = (z - sp) - cs - jnp.concatenate([tot, tot], axis=1)
    if causal is not None:
        arg = jnp.where(causal, arg, NEG_INF)
    pv = jnp.dot(jnp.exp2(arg).astype(BF16), v, preferred_element_type=F32)
    rs = jnp.sum(sp, axis=1, keepdims=True)
    return pv, jnp.broadcast_to(rs, (SB_BLK, HEAD_DIM))


def _sb_first_two(q, k2, v2, later, valid):
    dn = (((1,), (1,)), ((), ()))
    z = lax.dot_general(q, k2, dn, preferred_element_type=F32)
    sp = jnp.where(valid, jnp.maximum(z, 0.0) + jnp.log2(1.0 + jnp.exp2(-jnp.abs(z))), 0.0)
    sp_prev, sp_diag = sp[:, :SB_BLK], sp[:, SB_BLK:]
    cs = jnp.concatenate([jnp.dot(sp_prev.astype(BF16), later, preferred_element_type=F32),
                          jnp.dot(sp_diag.astype(BF16), later, preferred_element_type=F32)], axis=1)
    rs_diag = jnp.broadcast_to(jnp.sum(sp_diag, axis=1, keepdims=True), (SB_BLK, HEAD_DIM))
    rs_prev = jnp.broadcast_to(jnp.sum(sp_prev, axis=1, keepdims=True), (SB_BLK, HEAD_DIM))
    zero = jnp.zeros((SB_BLK, HEAD_DIM), F32)
    behind = jnp.concatenate([rs_diag, rs_diag, zero, zero], axis=1)
    arg = jnp.where(valid, (z - sp) - cs - behind, NEG_INF)
    pv = jnp.dot(jnp.exp2(arg).astype(BF16), v2, preferred_element_type=F32)
    return pv, rs_diag + rs_prev


def _stickbreak_kernel(q_ref, k_ref, v_ref, o_ref, acc_ref, tot_ref):
    qi = pl.program_id(1)
    row = lax.broadcasted_iota(jnp.int32, (SB_BLK, SB_BLK), 0)
    col = lax.broadcasted_iota(jnp.int32, (SB_BLK, SB_BLK), 1)
    later = (row > col).astype(BF16)
    causal = col < row
    heads = [slice(h * HEAD_DIM, (h + 1) * HEAD_DIM) for h in range(N_HEADS)]

    def key_rows(kb, nblk=1):
        return pl.ds(pl.multiple_of(kb * SB_BLK, SB_BLK), nblk * SB_BLK)

    @pl.when(qi == 0)
    def _():
        rows = key_rows(qi)
        zero = jnp.zeros((SB_BLK, HEAD_DIM), F32)
        for hs in heads:
            pv, rs = _sb_block(q_ref[:, hs], k_ref[rows, hs], v_ref[rows, hs], zero, later, causal)
            acc_ref[:, hs] = pv
            tot_ref[:, hs] = rs

    @pl.when(qi > 0)
    def _():
        rows = key_rows(qi - 1, 2)
        row2 = lax.broadcasted_iota(jnp.int32, (SB_BLK, 2 * SB_BLK), 0)
        col2 = lax.broadcasted_iota(jnp.int32, (SB_BLK, 2 * SB_BLK), 1)
        valid = col2 < row2 + SB_BLK
        for hs in heads:
            pv, rs = _sb_first_two(q_ref[:, hs], k_ref[rows, hs], v_ref[rows, hs], later, valid)
            acc_ref[:, hs] = pv
            tot_ref[:, hs] = rs

    def earlier_block(kb):
        rows = key_rows(kb)
        for hs in heads:
            pv, rs = _sb_block(q_ref[:, hs], k_ref[rows, hs], v_ref[rows, hs],
                               tot_ref[:, hs], later, None)
            acc_ref[:, hs] += pv
            tot_ref[:, hs] += rs

    def cond(c):
        kb, consumed = c
        return (kb >= 0) & (consumed < SB_DONE)

    def body(c):
        kb, _ = c
        earlier_block(kb)
        return kb - 1, jnp.min(tot_ref[...])

    lax.while_loop(cond, body, (qi - 2, jnp.min(tot_ref[...])))
    o_ref[...] = acc_ref[...].astype(o_ref.dtype)


def _stickbreak(proj3):
    b, s, _ = proj3.shape
    return pl.pallas_call(
        _stickbreak_kernel,
        grid=(b, s // SB_BLK),
        in_specs=[
            pl.BlockSpec((None, SB_BLK, W_HEADS), lambda bi, qi: (bi, qi, COL_QB)),
            pl.BlockSpec((None, s, W_HEADS), lambda bi, qi: (bi, 0, COL_KB),
                         pipeline_mode=pl.Buffered(1)),
            pl.BlockSpec((None, s, W_HEADS), lambda bi, qi: (bi, 0, COL_VB),
                         pipeline_mode=pl.Buffered(1)),
        ],
        out_specs=pl.BlockSpec((None, SB_BLK, W_HEADS), lambda bi, qi: (bi, qi, 0)),
        out_shape=jax.ShapeDtypeStruct((b, s, W_HEADS), BF16),
        scratch_shapes=[pltpu.VMEM((SB_BLK, W_HEADS), F32), pltpu.VMEM((SB_BLK, W_HEADS), F32)],
        compiler_params=pltpu.CompilerParams(
            dimension_semantics=("arbitrary", "arbitrary"),
            vmem_limit_bytes=48 * MIB),
        name="stickbreak",
    )(proj3, proj3, proj3)


MIX_TM = 256


def _mixout_kernel(x_ref, ya_ref, yb_ref, ga_ref, gb_ref, wa_ref, wb_ref, wo_ref, o_ref):
    ta = jnp.dot(ya_ref[...].astype(BF16), wa_ref[...], preferred_element_type=F32)
    tb = jnp.dot(yb_ref[...], wb_ref[...], preferred_element_type=F32)
    merged = (jax.nn.sigmoid(ga_ref[...].astype(F32)) * ta
              + jax.nn.sigmoid(gb_ref[...].astype(F32)) * tb)
    o_ref[...] = x_ref[...] + jnp.dot(merged.astype(BF16), wo_ref[...], preferred_element_type=F32)


def _resident(shape):
    return pl.BlockSpec(shape, lambda *_: (0,) * len(shape), pipeline_mode=pl.Buffered(1))


def _mixout(x2d, ya2d, yb2d, proj2d, wa, wb, wo):
    t = x2d.shape[0]
    tm = MIX_TM
    ga_col = (6 * W_HEADS) // D_MODEL
    gb_col = ga_col + 1
    return pl.pallas_call(
        _mixout_kernel,
        grid=(t // tm,),
        in_specs=[
            pl.BlockSpec((tm, D_MODEL), lambda i: (i, 0)),
            pl.BlockSpec((tm, W_HEADS), lambda i: (i, 0)),
            pl.BlockSpec((tm, W_HEADS), lambda i: (i, 0)),
            pl.BlockSpec((tm, D_MODEL), lambda i: (i, ga_col)),
            pl.BlockSpec((tm, D_MODEL), lambda i: (i, gb_col)),
            _resident((W_HEADS, D_MODEL)),
            _resident((W_HEADS, D_MODEL)),
            _resident((D_MODEL, D_MODEL)),
        ],
        out_specs=pl.BlockSpec((tm, D_MODEL), lambda i: (i, 0)),
        out_shape=jax.ShapeDtypeStruct((t, D_MODEL), F32),
        compiler_params=pltpu.CompilerParams(
            dimension_semantics=("arbitrary",),
            vmem_limit_bytes=48 * MIB),
        name="mixout",
    )(x2d, ya2d, yb2d, proj2d, proj2d, wa, wb, wo)


MLP_TM = 1024
MLP_TF = 512


def _mlp_kernel(x_ref, g_ref, wu_ref, wd_ref, o_ref, hn_ref):
    f = pl.program_id(1)

    def chunk(hn):
        u = jnp.dot(hn, wu_ref[...], preferred_element_type=F32)
        a = jnp.square(jnp.maximum(u, 0.0)).astype(BF16)
        return jnp.dot(a, wd_ref[...], preferred_element_type=F32)

    @pl.when(f == 0)
    def _():
        x = x_ref[...]
        hn = _rms_scale(x, g_ref[...]).astype(BF16)
        hn_ref[...] = hn
        o_ref[...] = x + chunk(hn)

    @pl.when(f > 0)
    def _():
        o_ref[...] += chunk(hn_ref[...])


def _mlp(x2d, g, wu, wd):
    t = x2d.shape[0]
    tm, tf = MLP_TM, MLP_TF
    return pl.pallas_call(
        _mlp_kernel,
        grid=(t // tm, D_FF // tf),
        in_specs=[
            pl.BlockSpec((tm, D_MODEL), lambda i, f: (i, 0)),
            pl.BlockSpec((1, D_MODEL), lambda i, f: (0, 0)),
            pl.BlockSpec((D_MODEL, tf), lambda i, f: (0, f)),
            pl.BlockSpec((tf, D_MODEL), lambda i, f: (f, 0)),
        ],
        out_specs=pl.BlockSpec((tm, D_MODEL), lambda i, f: (i, 0)),
        out_shape=jax.ShapeDtypeStruct((t, D_MODEL), F32),
        scratch_shapes=[pltpu.VMEM((tm, D_MODEL), BF16)],
        compiler_params=pltpu.CompilerParams(
            dimension_semantics=("arbitrary", "arbitrary"),
            vmem_limit_bytes=56 * MIB),
        name="mlp",
    )(x2d, g, wu, wd)


PLE_TM = 256


def _ple_kernel(x_ref, p_ref, g_ref, wg_ref, wp_ref, o_ref):
    x = x_ref[...]
    hn = _rms_scale(x, g_ref[...]).astype(BF16)
    gate = jnp.dot(hn, wg_ref[...], preferred_element_type=F32)
    pp = jnp.dot(p_ref[...].astype(BF16), wp_ref[...], preferred_element_type=F32)
    o_ref[...] = x + pp * jax.nn.sigmoid(gate)


def _ple(x2d, p2d, g, wg, wp):
    t = x2d.shape[0]
    tm = PLE_TM
    ple_dim = p2d.shape[1]
    return pl.pallas_call(
        _ple_kernel,
        grid=(t // tm,),
        in_specs=[
            pl.BlockSpec((tm, D_MODEL), lambda i: (i, 0)),
            pl.BlockSpec((tm, ple_dim), lambda i: (i, 0)),
            pl.BlockSpec((1, D_MODEL), lambda i: (0, 0)),
            _resident((D_MODEL, D_MODEL)),
            _resident((ple_dim, D_MODEL)),
        ],
        out_specs=pl.BlockSpec((tm, D_MODEL), lambda i: (i, 0)),
        out_shape=jax.ShapeDtypeStruct((t, D_MODEL), F32),
        compiler_params=pltpu.CompilerParams(
            dimension_semantics=("arbitrary",),
            vmem_limit_bytes=40 * MIB),
        name="ple",
    )(x2d, p2d, g, wg, wp)


def _rope_tables(seq):
    half = ROT_DIM // 2
    pos = jnp.arange(seq, dtype=F32)
    inv = ROPE_THETA ** (-jnp.arange(0, ROT_DIM, 2, dtype=F32) / ROT_DIM)
    ang = pos[:, None] * inv[None, :]
    cos, sin = jnp.cos(ang), jnp.sin(ang)
    rest = HEAD_DIM - ROT_DIM
    cos_t = jnp.concatenate([cos, cos, jnp.ones((seq, rest), F32)], axis=-1)
    zeros_h = jnp.zeros((seq, half), F32)
    zeros_r = jnp.zeros((seq, rest), F32)
    sa_t = jnp.concatenate([-sin, zeros_h, zeros_r], axis=-1)
    sb_t = jnp.concatenate([zeros_h, sin, zeros_r], axis=-1)
    return cos_t, sa_t, sb_t


def kernel(x, p, g_mix, w_in, qn_gain, kn_gain, w_branch_a, w_branch_b, w_out,
           g_mlp, w_up, w_down, g_ple, w_ple_gate, w_ple_proj):
    b, s, d = x.shape
    t = b * s
    depth = w_in.shape[0]
    cos_t, sa_t, sb_t = _rope_tables(s)
    x2d = x.reshape(t, d)
    for i in range(depth):
        proj2d, nat2d, ph2d = _inproj(x2d, g_mix[i][None, :], w_in[i].astype(BF16),
                                      qn_gain[i][None, :], kn_gain[i][None, :], cos_t, sa_t, sb_t, s)
        proj3 = proj2d.reshape(b, s, D_IN)

        o_near, l_near = _near(nat2d.reshape(b, s, N_MIXA * W_HEADS))
        ya = _far(ph2d.reshape(b, s // FAR_DIL, N_MIXA * PH_SEG), o_near, l_near)
        ya2d = ya.reshape(t, W_HEADS)
        yb2d = _stickbreak(proj3).reshape(t, W_HEADS)

        x2d = _mixout(x2d, ya2d, yb2d, proj2d, w_branch_a[i].astype(BF16),
                      w_branch_b[i].astype(BF16), w_out[i].astype(BF16))
        x2d = _mlp(x2d, g_mlp[i][None, :], w_up[i].astype(BF16), w_down[i].astype(BF16))
        x2d = _ple(x2d, p[i].reshape(t, -1), g_ple[i][None, :],
                   w_ple_gate[i].astype(BF16), w_ple_proj[i].astype(BF16))
    return x2d.reshape(b, s, d)
```

```python
import jax
import jax.numpy as jnp
from jax import lax
from jax.experimental import pallas as pl
from jax.experimental.pallas import tpu as pltpu

D_MODEL = 2048
HEAD_DIM = 128
N_HEADS = 8
W_HEADS = N_HEADS * HEAD_DIM
DIL_PATTERNS = ((128, 1), (512, 4), (2048, 16))
BLOCK = 128
ROT_DIM = HEAD_DIM // 4
ROPE_THETA = 500000.0
D_FF = 4 * D_MODEL
EPS = 1e-6
D_IN = 6 * W_HEADS + 2 * D_MODEL
LOG2E = 1.4426950408889634
Q_SCALE = HEAD_DIM ** -0.5 * LOG2E

COL_QA, COL_KA, COL_VA, COL_QB, COL_KB, COL_VB = 0, 1, 2, 3, 4, 5
N_COL_TILES = D_IN // W_HEADS

F32 = jnp.float32
BF16 = jnp.bfloat16
NEG_INF = float("-inf")

MIB = 1024 * 1024


def _rms_scale(x, gain):
    ms = jnp.mean(x * x, axis=-1, keepdims=True)
    return x * lax.rsqrt(ms + EPS) * gain


INPROJ_TM = 512


FAR_DIL = DIL_PATTERNS[2][1]
PH_ROWS = INPROJ_TM // FAR_DIL
PH_HEAD = FAR_DIL * HEAD_DIM
PH_SEG = N_HEADS * PH_HEAD


ROW_CHUNK = 128
SUB_DIL = 4
ROLL_DN = HEAD_DIM - ROT_DIM // 2
ROLL_UP = ROT_DIM // 2


N_MIXA = 3
HEAD_GROUPS = ((0, 1, 2), (3, 4, 5), (6, 7))


def _inproj_kernel(x_ref, g_ref, w_ref, qn_ref, kn_ref, cos_ref, sa_ref, sb_ref,
                   o_ref, nat_ref, ph_ref, hn_ref, tab_ref, raw_ref, ys_ref, yq_ref):
    j = pl.program_id(1)
    tm = x_ref.shape[0]

    @pl.when(j == 0)
    def _():
        hn_ref[...] = _rms_scale(x_ref[...], g_ref[...]).astype(BF16)
        for n, (gain_ref, scale) in enumerate(((qn_ref, Q_SCALE), (kn_ref, 1.0))):
            g = jnp.broadcast_to(gain_ref[...] * scale, (tm, HEAD_DIM))
            tab_ref[3 * n] = cos_ref[...] * g
            tab_ref[3 * n + 1] = sa_ref[...] * pltpu.roll(g, ROLL_DN, 1)
            tab_ref[3 * n + 2] = sb_ref[...] * pltpu.roll(g, ROLL_UP, 1)

    def project():
        return jnp.dot(hn_ref[...], w_ref[...], preferred_element_type=F32)

    def finish_mixer_a(tile, head_group):
        tab = 3 * tile
        for h in head_group:
            sl = slice(h * HEAD_DIM, (h + 1) * HEAD_DIM)
            for c in range(tm // ROW_CHUNK):
                rows = slice(c * ROW_CHUNK, (c + 1) * ROW_CHUNK)
                a = raw_ref[tile, rows, sl]
                if tile < N_MIXA - 1:
                    rs = lax.rsqrt(jnp.mean(a * a, axis=-1, keepdims=True) + EPS)
                    a = (a * tab_ref[tab, rows, :]
                         + pltpu.roll(a, ROLL_DN, 1) * tab_ref[tab + 1, rows, :]
                         + pltpu.roll(a, ROLL_UP, 1) * tab_ref[tab + 2, rows, :]) * rs
                nat_ref[rows, sl] = a.astype(BF16)
                ys_ref[h, rows, :] = a
            for r_lo in range(SUB_DIL):
                yq_ref[h, r_lo] = ys_ref[h, pl.ds(r_lo, tm // SUB_DIL, stride=SUB_DIL), :]
                for r_hi in range(FAR_DIL // SUB_DIL):
                    r = r_lo + SUB_DIL * r_hi
                    lanes = slice(h * PH_HEAD + r * HEAD_DIM, h * PH_HEAD + (r + 1) * HEAD_DIM)
                    ph_ref[:, lanes] = yq_ref[h, r_lo, pl.ds(r_hi, PH_ROWS, stride=SUB_DIL), :].astype(BF16)

    @pl.when(j == 0)
    def _():
        acc = project()
        raw_ref[0] = acc
        o_ref[...] = acc.astype(BF16)

    assert N_COL_TILES == 1 + N_MIXA * len(HEAD_GROUPS)
    for step in range(1, N_COL_TILES):
        @pl.when(j == step)
        def _(step=step):
            acc = project()
            if step < N_MIXA:
                raw_ref[step] = acc
            if step == COL_QB:
                acc = acc * Q_SCALE
            o_ref[...] = acc.astype(BF16)
            finish_mixer_a((step - 1) // len(HEAD_GROUPS), HEAD_GROUPS[(step - 1) % len(HEAD_GROUPS)])


def _inproj(x2d, g, w_bf16, qn, kn, cos_t, sa_t, sb_t, seq):
    t = x2d.shape[0]
    tm = INPROJ_TM
    nseq = seq // tm
    tab_spec = pl.BlockSpec((tm, HEAD_DIM), lambda i, j: (i % nseq, 0))
    vec_spec = pl.BlockSpec((1, HEAD_DIM), lambda i, j: (0, 0))
    return pl.pallas_call(
        _inproj_kernel,
        grid=(t // tm, N_COL_TILES),
        in_specs=[
            pl.BlockSpec((tm, D_MODEL), lambda i, j: (i, 0)),
            pl.BlockSpec((1, D_MODEL), lambda i, j: (0, 0)),
            pl.BlockSpec((D_MODEL, W_HEADS), lambda i, j: (0, j)),
            vec_spec, vec_spec, tab_spec, tab_spec, tab_spec,
        ],
        out_specs=[
            pl.BlockSpec((tm, W_HEADS), lambda i, j: (i, j)),
            pl.BlockSpec((tm, W_HEADS), lambda i, j: (i, jnp.maximum(j - 1, 0) // len(HEAD_GROUPS))),
            pl.BlockSpec((PH_ROWS, PH_SEG), lambda i, j: (i, jnp.maximum(j - 1, 0) // len(HEAD_GROUPS))),
        ],
        out_shape=[
            jax.ShapeDtypeStruct((t, D_IN), BF16),
            jax.ShapeDtypeStruct((t, N_MIXA * W_HEADS), BF16),
            jax.ShapeDtypeStruct((t // FAR_DIL, N_MIXA * PH_SEG), BF16),
        ],
        scratch_shapes=[
            pltpu.VMEM((tm, D_MODEL), BF16),
            pltpu.VMEM((6, tm, HEAD_DIM), F32),
            pltpu.VMEM((N_MIXA, tm, W_HEADS), F32),
            pltpu.VMEM((N_HEADS, tm, HEAD_DIM), F32),
            pltpu.VMEM((N_HEADS, SUB_DIL, tm // SUB_DIL, HEAD_DIM), F32),
        ],
        compiler_params=pltpu.CompilerParams(
            dimension_semantics=("arbitrary", "arbitrary"),
            vmem_limit_bytes=48 * MIB),
        name="inproj",
    )(x2d, g, w_bf16, qn, kn, cos_t, sa_t, sb_t)


NEAR_QB = 512
NEAR_SUB = 256
NEAR_WIN = DIL_PATTERNS[1][0] + NEAR_SUB


def _near_kernel(q_ref, kp_ref, k_ref, vp_ref, v_ref, o_ref, l_ref, kcat_ref, vcat_ref):
    first = pl.program_id(1) == 0
    kcat_ref[:NEAR_QB, :] = kp_ref[...]
    kcat_ref[NEAR_QB:, :] = k_ref[...]
    vcat_ref[:NEAR_QB, :] = vp_ref[...]
    vcat_ref[NEAR_QB:, :] = v_ref[...]

    qi = lax.broadcasted_iota(jnp.int32, (NEAR_SUB, NEAR_WIN), 0)
    col = lax.broadcasted_iota(jnp.int32, (NEAR_SUB, NEAR_WIN), 1)
    dist = qi + (NEAR_WIN - NEAR_SUB) - col
    (w1, _), (w4, d4) = DIL_PATTERNS[0], DIL_PATTERNS[1]
    assert d4 & (d4 - 1) == 0
    in1 = (dist >= 0) & (dist <= w1)
    in4 = (dist >= 0) & (dist <= w4) & ((dist & (d4 - 1)) == 0)
    bias = jnp.where(in1 & in4, 1.0, jnp.where(in1 | in4, 0.0, NEG_INF))
    dn = (((1,), (1,)), ((), ()))

    for sb in range(NEAR_QB // NEAR_SUB):
        rows = slice(sb * NEAR_SUB, (sb + 1) * NEAR_SUB)
        win = slice(sb * NEAR_SUB, sb * NEAR_SUB + NEAR_WIN)
        bias_sb = jnp.where(first & (col < NEAR_QB - sb * NEAR_SUB), NEG_INF, bias)
        for h in range(N_HEADS):
            cols = slice(h * HEAD_DIM, (h + 1) * HEAD_DIM)
            s = lax.dot_general(q_ref[rows, cols], kcat_ref[win, cols], dn,
                                preferred_element_type=F32) + bias_sb
            m = jnp.max(s, axis=-1, keepdims=True)
            e = jnp.exp2(s - m)
            den = jnp.sum(e, axis=-1, keepdims=True)
            o = jnp.dot(e.astype(BF16), vcat_ref[win, cols], preferred_element_type=F32) / den
            o_ref[rows, cols] = o
            l_ref[rows, cols] = jnp.broadcast_to(m + jnp.log2(den), (NEAR_SUB, HEAD_DIM))


def _near(nat3):
    b, s, _ = nat3.shape

    def cur(c):
        return lambda bi, qi: (bi, qi, c)

    def prev(c):
        return lambda bi, qi: (bi, jnp.maximum(qi - 1, 0), c)

    blk = (None, NEAR_QB, W_HEADS)
    out_spec = pl.BlockSpec(blk, lambda bi, qi: (bi, qi, 0))
    return pl.pallas_call(
        _near_kernel,
        grid=(b, s // NEAR_QB),
        in_specs=[
            pl.BlockSpec(blk, cur(COL_QA)),
            pl.BlockSpec(blk, prev(COL_KA)), pl.BlockSpec(blk, cur(COL_KA)),
            pl.BlockSpec(blk, prev(COL_VA)), pl.BlockSpec(blk, cur(COL_VA)),
        ],
        out_specs=[out_spec, out_spec],
        out_shape=[jax.ShapeDtypeStruct((b, s, W_HEADS), F32)] * 2,
        scratch_shapes=[pltpu.VMEM((2 * NEAR_QB, W_HEADS), BF16)] * 2,
        compiler_params=pltpu.CompilerParams(
            dimension_semantics=("arbitrary", "arbitrary"),
            vmem_limit_bytes=40 * MIB),
        name="dilated_near",
    )(nat3, nat3, nat3, nat3, nat3)


FAR_ROWS = BLOCK * FAR_DIL


def _far_kernel(q_ref, kp_ref, k_ref, vp_ref, v_ref, on_ref, ln_ref, y_ref, kcat_ref, vcat_ref):
    first = pl.program_id(1) == 0
    kcat_ref[:BLOCK, :] = kp_ref[...]
    kcat_ref[BLOCK:, :] = k_ref[...]
    vcat_ref[:BLOCK, :] = vp_ref[...]
    vcat_ref[BLOCK:, :] = v_ref[...]
    qi = lax.broadcasted_iota(jnp.int32, (BLOCK, 2 * BLOCK), 0)
    col = lax.broadcasted_iota(jnp.int32, (BLOCK, 2 * BLOCK), 1)
    dist = qi + BLOCK - col
    bias = jnp.where((dist >= 0) & (dist <= BLOCK), 0.0, NEG_INF)
    bias = jnp.where(first & (col < BLOCK), NEG_INF, bias)
    dn = (((1,), (1,)), ((), ()))

    for r in range(FAR_DIL):
        lanes = slice(r * HEAD_DIM, (r + 1) * HEAD_DIM)
        tokens = pl.ds(r, BLOCK, stride=FAR_DIL)
        s = lax.dot_general(q_ref[:, lanes], kcat_ref[:, lanes], dn, preferred_element_type=F32) + bias
        m = jnp.max(s, axis=-1, keepdims=True)
        e = jnp.exp2(s - m)
        den = jnp.sum(e, axis=-1, keepdims=True)
        o = jnp.dot(e.astype(BF16), vcat_ref[:, lanes], preferred_element_type=F32) / den
        lse = m + jnp.log2(den)
        l_near = ln_ref[tokens, :]
        mm = jnp.maximum(l_near, lse)
        w_near = jnp.exp2(l_near - mm)
        w_far = jnp.exp2(lse - mm)
        y_ref[tokens, :] = (on_ref[tokens, :] * w_near + o * w_far) / (w_near + w_far)


def _far(ph3, o_near, l_near):
    b, m, _ = ph3.shape
    s = m * FAR_DIL

    def cur(seg):
        return lambda bi, mi, h: (bi, mi, seg * N_HEADS + h)

    def prev(seg):
        return lambda bi, mi, h: (bi, jnp.maximum(mi - 1, 0), seg * N_HEADS + h)

    blk = (None, BLOCK, PH_HEAD)
    tok_spec = pl.BlockSpec((None, FAR_ROWS, HEAD_DIM), lambda bi, mi, h: (bi, mi, h))
    return pl.pallas_call(
        _far_kernel,
        grid=(b, m // BLOCK, N_HEADS),
        in_specs=[
            pl.BlockSpec(blk, cur(0)),
            pl.BlockSpec(blk, prev(1)), pl.BlockSpec(blk, cur(1)),
            pl.BlockSpec(blk, prev(2)), pl.BlockSpec(blk, cur(2)),
            tok_spec, tok_spec,
        ],
        out_specs=tok_spec,
        out_shape=jax.ShapeDtypeStruct((b, s, W_HEADS), F32),
        scratch_shapes=[pltpu.VMEM((2 * BLOCK, PH_HEAD), BF16)] * 2,
        compiler_params=pltpu.CompilerParams(
            dimension_semantics=("arbitrary", "arbitrary", "arbitrary"),
            vmem_limit_bytes=32 * MIB),
        name="dilated_far",
    )(ph3, ph3, ph3, ph3, ph3, o_near, l_near)


SB_BLK = 256
SB_DONE = 160.0


def _sb_block(q, k, v, tot, later, causal):
    dn = (((1,), (1,)), ((), ()))
    z = lax.dot_general(q, k, dn, preferred_element_type=F32)
    sp = jnp.maximum(z, 0.0) + jnp.log2(1.0 + jnp.exp2(-jnp.abs(z)))
    if causal is not None:
        sp = jnp.where(causal, sp, 0.0)
    cs = jnp.dot(sp.astype(BF16), later, preferred_element_type=F32)
    arg = (z - sp) - cs - jnp.concatenate([tot, tot], axis=1)
    if causal is not None:
        arg = jnp.where(causal, arg, NEG_INF)
    pv = jnp.dot(jnp.exp2(arg).astype(BF16), v, preferred_element_type=F32)
    rs = jnp.sum(sp, axis=1, keepdims=True)
    return pv, jnp.broadcast_to(rs, (SB_BLK, HEAD_DIM))


def _sb_first_two(q, k2, v2, later, valid):
    dn = (((1,), (1,)), ((), ()))
    z = lax.dot_general(q, k2, dn, preferred_element_type=F32)
    sp = jnp.where(valid, jnp.maximum(z, 0.0) + jnp.log2(1.0 + jnp.exp2(-jnp.abs(z))), 0.0)
    sp_prev, sp_diag = sp[:, :SB_BLK], sp[:, SB_BLK:]
    cs = jnp.concatenate([jnp.dot(sp_prev.astype(BF16), later, preferred_element_type=F32),
                          jnp.dot(sp_diag.astype(BF16), later, preferred_element_type=F32)], axis=1)
    rs_diag = jnp.broadcast_to(jnp.sum(sp_diag, axis=1, keepdims=True), (SB_BLK, HEAD_DIM))
    rs_prev = jnp.broadcast_to(jnp.sum(sp_prev, axis=1, keepdims=True), (SB_BLK, HEAD_DIM))
    zero = jnp.zeros((SB_BLK, HEAD_DIM), F32)
    behind = jnp.concatenate([rs_diag, rs_diag, zero, zero], axis=1)
    arg = jnp.where(valid, (z - sp) - cs - behind, NEG_INF)
    pv = jnp.dot(jnp.exp2(arg).astype(BF16), v2, preferred_element_type=F32)
    return pv, rs_diag + rs_prev


def _stickbreak_kernel(q_ref, k_ref, v_ref, o_ref, acc_ref, tot_ref):
    qi = pl.program_id(1)
    row = lax.broadcasted_iota(jnp.int32, (SB_BLK, SB_BLK), 0)
    col = lax.broadcasted_iota(jnp.int32, (SB_BLK, SB_BLK), 1)
    later = (row > col).astype(BF16)
    causal = col < row
    heads = [slice(h * HEAD_DIM, (h + 1) * HEAD_DIM) for h in range(N_HEADS)]

    def key_rows(kb, nblk=1):
        return pl.ds(pl.multiple_of(kb * SB_BLK, SB_BLK), nblk * SB_BLK)

    @pl.when(qi == 0)
    def _():
        rows = key_rows(qi)
        zero = jnp.zeros((SB_BLK, HEAD_DIM), F32)
        for hs in heads:
            pv, rs = _sb_block(q_ref[:, hs], k_ref[rows, hs], v_ref[rows, hs], zero, later, causal)
            acc_ref[:, hs] = pv
            tot_ref[:, hs] = rs

    @pl.when(qi > 0)
    def _():
        rows = key_rows(qi - 1, 2)
        row2 = lax.broadcasted_iota(jnp.int32, (SB_BLK, 2 * SB_BLK), 0)
        col2 = lax.broadcasted_iota(jnp.int32, (SB_BLK, 2 * SB_BLK), 1)
        valid = col2 < row2 + SB_BLK
        for hs in heads:
            pv, rs = _sb_first_two(q_ref[:, hs], k_ref[rows, hs], v_ref[rows, hs], later, valid)
            acc_ref[:, hs] = pv
            tot_ref[:, hs] = rs

    def earlier_block(kb):
        rows = key_rows(kb)
        for hs in heads:
            pv, rs = _sb_block(q_ref[:, hs], k_ref[rows, hs], v_ref[rows, hs],
                               tot_ref[:, hs], later, None)
            acc_ref[:, hs] += pv
            tot_ref[:, hs] += rs

    def cond(c):
        kb, consumed = c
        return (kb >= 0) & (consumed < SB_DONE)

    def body(c):
        kb, _ = c
        earlier_block(kb)
        return kb - 1, jnp.min(tot_ref[...])

    lax.while_loop(cond, body, (qi - 2, jnp.min(tot_ref[...])))
    o_ref[...] = acc_ref[...].astype(o_ref.dtype)


def _stickbreak(proj3):
    b, s, _ = proj3.shape
    return pl.pallas_call(
        _stickbreak_kernel,
        grid=(b, s // SB_BLK),
        in_specs=[
            pl.BlockSpec((None, SB_BLK, W_HEADS), lambda bi, qi: (bi, qi, COL_QB)),
            pl.BlockSpec((None, s, W_HEADS), lambda bi, qi: (bi, 0, COL_KB),
                         pipeline_mode=pl.Buffered(1)),
            pl.BlockSpec((None, s, W_HEADS), lambda bi, qi: (bi, 0, COL_VB),
                         pipeline_mode=pl.Buffered(1)),
        ],
        out_specs=pl.BlockSpec((None, SB_BLK, W_HEADS), lambda bi, qi: (bi, qi, 0)),
        out_shape=jax.ShapeDtypeStruct((b, s, W_HEADS), BF16),
        scratch_shapes=[pltpu.VMEM((SB_BLK, W_HEADS), F32), pltpu.VMEM((SB_BLK, W_HEADS), F32)],
        compiler_params=pltpu.CompilerParams(
            dimension_semantics=("arbitrary", "arbitrary"),
            vmem_limit_bytes=48 * MIB),
        name="stickbreak",
    )(proj3, proj3, proj3)


MIX_TM = 512


def _mixout_kernel(x_ref, ya_ref, yb_ref, ga_ref, gb_ref, wa_ref, wb_ref, wo_ref, o_ref):
    ta = jnp.dot(ya_ref[...].astype(BF16), wa_ref[...], preferred_element_type=F32)
    tb = jnp.dot(yb_ref[...], wb_ref[...], preferred_element_type=F32)
    merged = (jax.nn.sigmoid(ga_ref[...].astype(F32)) * ta
              + jax.nn.sigmoid(gb_ref[...].astype(F32)) * tb)
    o_ref[...] = x_ref[...] + jnp.dot(merged.astype(BF16), wo_ref[...], preferred_element_type=F32)


def _resident(shape):
    return pl.BlockSpec(shape, lambda *_: (0,) * len(shape), pipeline_mode=pl.Buffered(1))


def _mixout(x2d, ya2d, yb2d, proj2d, wa, wb, wo):
    t = x2d.shape[0]
    tm = MIX_TM
    ga_col = (6 * W_HEADS) // D_MODEL
    gb_col = ga_col + 1
    return pl.pallas_call(
        _mixout_kernel,
        grid=(t // tm,),
        in_specs=[
            pl.BlockSpec((tm, D_MODEL), lambda i: (i, 0)),
            pl.BlockSpec((tm, W_HEADS), lambda i: (i, 0)),
            pl.BlockSpec((tm, W_HEADS), lambda i: (i, 0)),
            pl.BlockSpec((tm, D_MODEL), lambda i: (i, ga_col)),
            pl.BlockSpec((tm, D_MODEL), lambda i: (i, gb_col)),
            _resident((W_HEADS, D_MODEL)),
            _resident((W_HEADS, D_MODEL)),
            _resident((D_MODEL, D_MODEL)),
        ],
        out_specs=pl.BlockSpec((tm, D_MODEL), lambda i: (i, 0)),
        out_shape=jax.ShapeDtypeStruct((t, D_MODEL), F32),
        compiler_params=pltpu.CompilerParams(
            dimension_semantics=("arbitrary",),
            vmem_limit_bytes=56 * MIB),
        name="mixout",
    )(x2d, ya2d, yb2d, proj2d, proj2d, wa, wb, wo)


MLP_TM = 1024
MLP_TF = 512


def _mlp_kernel(x_ref, g_ref, wu_ref, wd_ref, o_ref, hn_ref):
    f = pl.program_id(1)

    def chunk(hn):
        u = jnp.dot(hn, wu_ref[...], preferred_element_type=F32)
        a = jnp.square(jnp.maximum(u, 0.0)).astype(BF16)
        return jnp.dot(a, wd_ref[...], preferred_element_type=F32)

    @pl.when(f == 0)
    def _():
        x = x_ref[...]
        hn = _rms_scale(x, g_ref[...]).astype(BF16)
        hn_ref[...] = hn
        o_ref[...] = x + chunk(hn)

    @pl.when(f > 0)
    def _():
        o_ref[...] += chunk(hn_ref[...])


def _mlp(x2d, g, wu, wd):
    t = x2d.shape[0]
    tm, tf = MLP_TM, MLP_TF
    return pl.pallas_call(
        _mlp_kernel,
        grid=(t // tm, D_FF // tf),
        in_specs=[
            pl.BlockSpec((tm, D_MODEL), lambda i, f: (i, 0)),
            pl.BlockSpec((1, D_MODEL), lambda i, f: (0, 0)),
            pl.BlockSpec((D_MODEL, tf), lambda i, f: (0, f)),
            pl.BlockSpec((tf, D_MODEL), lambda i, f: (f, 0)),
        ],
        out_specs=pl.BlockSpec((tm, D_MODEL), lambda i, f: (i, 0)),
        out_shape=jax.ShapeDtypeStruct((t, D_MODEL), F32),
        scratch_shapes=[pltpu.VMEM((tm, D_MODEL), BF16)],
        compiler_params=pltpu.CompilerParams(
            dimension_semantics=("arbitrary", "arbitrary"),
            vmem_limit_bytes=56 * MIB),
        name="mlp",
    )(x2d, g, wu, wd)


PLE_TM = 512


def _ple_kernel(x_ref, p_ref, g_ref, wg_ref, wp_ref, o_ref):
    x = x_ref[...]
    hn = _rms_scale(x, g_ref[...]).astype(BF16)
    gate = jnp.dot(hn, wg_ref[...], preferred_element_type=F32)
    pp = jnp.dot(p_ref[...].astype(BF16), wp_ref[...], preferred_element_type=F32)
    o_ref[...] = x + pp * jax.nn.sigmoid(gate)


def _ple(x2d, p2d, g, wg, wp):
    t = x2d.shape[0]
    tm = PLE_TM
    ple_dim = p2d.shape[1]
    return pl.pallas_call(
        _ple_kernel,
        grid=(t // tm,),
        in_specs=[
            pl.BlockSpec((tm, D_MODEL), lambda i: (i, 0)),
            pl.BlockSpec((tm, ple_dim), lambda i: (i, 0)),
            pl.BlockSpec((1, D_MODEL), lambda i: (0, 0)),
            _resident((D_MODEL, D_MODEL)),
            _resident((ple_dim, D_MODEL)),
        ],
        out_specs=pl.BlockSpec((tm, D_MODEL), lambda i: (i, 0)),
        out_shape=jax.ShapeDtypeStruct((t, D_MODEL), F32),
        compiler_params=pltpu.CompilerParams(
            dimension_semantics=("arbitrary",),
            vmem_limit_bytes=40 * MIB),
        name="ple",
    )(x2d, p2d, g, wg, wp)


def _rope_tables(seq):
    half = ROT_DIM // 2
    pos = jnp.arange(seq, dtype=F32)
    inv = ROPE_THETA ** (-jnp.arange(0, ROT_DIM, 2, dtype=F32) / ROT_DIM)
    ang = pos[:, None] * inv[None, :]
    cos, sin = jnp.cos(ang), jnp.sin(ang)
    rest = HEAD_DIM - ROT_DIM
    cos_t = jnp.concatenate([cos, cos, jnp.ones((seq, rest), F32)], axis=-1)
    zeros_h = jnp.zeros((seq, half), F32)
    zeros_r = jnp.zeros((seq, rest), F32)
    sa_t = jnp.concatenate([-sin, zeros_h, zeros_r], axis=-1)
    sb_t = jnp.concatenate([zeros_h, sin, zeros_r], axis=-1)
    return cos_t, sa_t, sb_t


def kernel(x, p, g_mix, w_in, qn_gain, kn_gain, w_branch_a, w_branch_b, w_out,
           g_mlp, w_up, w_down, g_ple, w_ple_gate, w_ple_proj):
    b, s, d = x.shape
    t = b * s
    depth = w_in.shape[0]
    cos_t, sa_t, sb_t = _rope_tables(s)
    x2d = x.reshape(t, d)
    for i in range(depth):
        proj2d, nat2d, ph2d = _inproj(x2d, g_mix[i][None, :], w_in[i].astype(BF16),
                                      qn_gain[i][None, :], kn_gain[i][None, :], cos_t, sa_t, sb_t, s)
        proj3 = proj2d.reshape(b, s, D_IN)

        o_near, l_near = _near(nat2d.reshape(b, s, N_MIXA * W_HEADS))
        ya = _far(ph2d.reshape(b, s // FAR_DIL, N_MIXA * PH_SEG), o_near, l_near)
        ya2d = ya.reshape(t, W_HEADS)
        yb2d = _stickbreak(proj3).reshape(t, W_HEADS)

        x2d = _mixout(x2d, ya2d, yb2d, proj2d, w_branch_a[i].astype(BF16),
                      w_branch_b[i].astype(BF16), w_out[i].astype(BF16))
        x2d = _mlp(x2d, g_mlp[i][None, :], w_up[i].astype(BF16), w_down[i].astype(BF16))
        x2d = _ple(x2d, p[i].reshape(t, -1), g_ple[i][None, :],
                   w_ple_gate[i].astype(BF16), w_ple_proj[i].astype(BF16))
    return x2d.reshape(b, s, d)
```

```python
import jax
import jax.numpy as jnp
from jax import lax
from jax.experimental import pallas as pl
from jax.experimental.pallas import tpu as pltpu

D_MODEL = 2048
HEAD_DIM = 128
N_HEADS = 8
W_HEADS = N_HEADS * HEAD_DIM
DIL_PATTERNS = ((128, 1), (512, 4), (2048, 16))
BLOCK = 128
ROT_DIM = HEAD_DIM // 4
ROPE_THETA = 500000.0
D_FF = 4 * D_MODEL
EPS = 1e-6
D_IN = 6 * W_HEADS + 2 * D_MODEL
LOG2E = 1.4426950408889634
Q_SCALE = HEAD_DIM ** -0.5 * LOG2E

COL_QA, COL_KA, COL_VA, COL_QB, COL_KB, COL_VB = 0, 1, 2, 3, 4, 5
N_COL_TILES = D_IN // W_HEADS

F32 = jnp.float32
BF16 = jnp.bfloat16
NEG_INF = float("-inf")

MIB = 1024 * 1024


def _rms_scale(x, gain):
    ms = jnp.mean(x * x, axis=-1, keepdims=True)
    return x * lax.rsqrt(ms + EPS) * gain


INPROJ_TM = 512


FAR_DIL = DIL_PATTERNS[2][1]
PH_ROWS = INPROJ_TM // FAR_DIL
PH_HEAD = FAR_DIL * HEAD_DIM
PH_SEG = N_HEADS * PH_HEAD


ROW_CHUNK = 128
SUB_DIL = 4
ROLL_DN = HEAD_DIM - ROT_DIM // 2
ROLL_UP = ROT_DIM // 2


N_MIXA = 3
HEAD_GROUPS = ((0, 1, 2), (3, 4, 5), (6, 7))


def _inproj_kernel(x_ref, g_ref, w_ref, qn_ref, kn_ref, cos_ref, sa_ref, sb_ref,
                   o_ref, nat_ref, ph_ref, hn_ref, tab_ref, raw_ref, ys_ref, yq_ref):
    j = pl.program_id(1)
    tm = x_ref.shape[0]

    @pl.when(j == 0)
    def _():
        hn_ref[...] = _rms_scale(x_ref[...], g_ref[...]).astype(BF16)
        for n, (gain_ref, scale) in enumerate(((qn_ref, Q_SCALE), (kn_ref, 1.0))):
            g = jnp.broadcast_to(gain_ref[...] * scale, (tm, HEAD_DIM))
            tab_ref[3 * n] = cos_ref[...] * g
            tab_ref[3 * n + 1] = sa_ref[...] * pltpu.roll(g, ROLL_DN, 1)
            tab_ref[3 * n + 2] = sb_ref[...] * pltpu.roll(g, ROLL_UP, 1)

    def project():
        return jnp.dot(hn_ref[...], w_ref[...], preferred_element_type=F32)

    def finish_mixer_a(tile, head_group):
        tab = 3 * tile
        for h in head_group:
            sl = slice(h * HEAD_DIM, (h + 1) * HEAD_DIM)
            for c in range(tm // ROW_CHUNK):
                rows = slice(c * ROW_CHUNK, (c + 1) * ROW_CHUNK)
                a = raw_ref[tile, rows, sl]
                if tile < N_MIXA - 1:
                    rs = lax.rsqrt(jnp.mean(a * a, axis=-1, keepdims=True) + EPS)
                    a = (a * tab_ref[tab, rows, :]
                         + pltpu.roll(a, ROLL_DN, 1) * tab_ref[tab + 1, rows, :]
                         + pltpu.roll(a, ROLL_UP, 1) * tab_ref[tab + 2, rows, :]) * rs
                nat_ref[rows, sl] = a.astype(BF16)
                ys_ref[h, rows, :] = a
            for r_lo in range(SUB_DIL):
                yq_ref[h, r_lo] = ys_ref[h, pl.ds(r_lo, tm // SUB_DIL, stride=SUB_DIL), :]
                for r_hi in range(FAR_DIL // SUB_DIL):
                    r = r_lo + SUB_DIL * r_hi
                    lanes = slice(h * PH_HEAD + r * HEAD_DIM, h * PH_HEAD + (r + 1) * HEAD_DIM)
                    ph_ref[:, lanes] = yq_ref[h, r_lo, pl.ds(r_hi, PH_ROWS, stride=SUB_DIL), :].astype(BF16)

    @pl.when(j == 0)
    def _():
        acc = project()
        raw_ref[0] = acc
        o_ref[...] = acc.astype(BF16)

    assert N_COL_TILES == 1 + N_MIXA * len(HEAD_GROUPS)
    for step in range(1, N_COL_TILES):
        @pl.when(j == step)
        def _(step=step):
            acc = project()
            if step < N_MIXA:
                raw_ref[step] = acc
            if step == COL_QB:
                acc = acc * Q_SCALE
            o_ref[...] = acc.astype(BF16)
            finish_mixer_a((step - 1) // len(HEAD_GROUPS), HEAD_GROUPS[(step - 1) % len(HEAD_GROUPS)])


def _inproj(x2d, g, w_bf16, qn, kn, cos_t, sa_t, sb_t, seq):
    t = x2d.shape[0]
    tm = INPROJ_TM
    nseq = seq // tm
    tab_spec = pl.BlockSpec((tm, HEAD_DIM), lambda i, j: (i % nseq, 0))
    vec_spec = pl.BlockSpec((1, HEAD_DIM), lambda i, j: (0, 0))
    return pl.pallas_call(
        _inproj_kernel,
        grid=(t // tm, N_COL_TILES),
        in_specs=[
            pl.BlockSpec((tm, D_MODEL), lambda i, j: (i, 0)),
            pl.BlockSpec((1, D_MODEL), lambda i, j: (0, 0)),
            pl.BlockSpec((None, D_MODEL, W_HEADS), lambda i, j: (j, 0, 0)),
            vec_spec, vec_spec, tab_spec, tab_spec, tab_spec,
        ],
        out_specs=[
            pl.BlockSpec((tm, W_HEADS), lambda i, j: (i, j)),
            pl.BlockSpec((tm, W_HEADS), lambda i, j: (i, jnp.maximum(j - 1, 0) // len(HEAD_GROUPS))),
            pl.BlockSpec((PH_ROWS, PH_SEG), lambda i, j: (i, jnp.maximum(j - 1, 0) // len(HEAD_GROUPS))),
        ],
        out_shape=[
            jax.ShapeDtypeStruct((t, D_IN), BF16),
            jax.ShapeDtypeStruct((t, N_MIXA * W_HEADS), BF16),
            jax.ShapeDtypeStruct((t // FAR_DIL, N_MIXA * PH_SEG), BF16),
        ],
        scratch_shapes=[
            pltpu.VMEM((tm, D_MODEL), BF16),
            pltpu.VMEM((6, tm, HEAD_DIM), F32),
            pltpu.VMEM((N_MIXA, tm, W_HEADS), F32),
            pltpu.VMEM((N_HEADS, tm, HEAD_DIM), F32),
            pltpu.VMEM((N_HEADS, SUB_DIL, tm // SUB_DIL, HEAD_DIM), F32),
        ],
        compiler_params=pltpu.CompilerParams(
            dimension_semantics=("arbitrary", "arbitrary"),
            vmem_limit_bytes=48 * MIB),
        name="inproj",
    )(x2d, g, w_bf16, qn, kn, cos_t, sa_t, sb_t)


NEAR_QB = 512
NEAR_SUB = 256
NEAR_WIN = DIL_PATTERNS[1][0] + NEAR_SUB


def _near_kernel(q_ref, kp_ref, k_ref, vp_ref, v_ref, o_ref, l_ref, kcat_ref, vcat_ref):
    first = pl.program_id(1) == 0
    kcat_ref[:NEAR_QB, :] = kp_ref[...]
    kcat_ref[NEAR_QB:, :] = k_ref[...]
    vcat_ref[:NEAR_QB, :] = vp_ref[...]
    vcat_ref[NEAR_QB:, :] = v_ref[...]

    qi = lax.broadcasted_iota(jnp.int32, (NEAR_SUB, NEAR_WIN), 0)
    col = lax.broadcasted_iota(jnp.int32, (NEAR_SUB, NEAR_WIN), 1)
    dist = qi + (NEAR_WIN - NEAR_SUB) - col
    (w1, _), (w4, d4) = DIL_PATTERNS[0], DIL_PATTERNS[1]
    assert d4 & (d4 - 1) == 0
    in1 = (dist >= 0) & (dist <= w1)
    in4 = (dist >= 0) & (dist <= w4) & ((dist & (d4 - 1)) == 0)
    bias = jnp.where(in1 & in4, 1.0, jnp.where(in1 | in4, 0.0, NEG_INF))
    dn = (((1,), (1,)), ((), ()))

    for sb in range(NEAR_QB // NEAR_SUB):
        rows = slice(sb * NEAR_SUB, (sb + 1) * NEAR_SUB)
        win = slice(sb * NEAR_SUB, sb * NEAR_SUB + NEAR_WIN)
        bias_sb = jnp.where(first & (col < NEAR_QB - sb * NEAR_SUB), NEG_INF, bias)
        for h in range(N_HEADS):
            cols = slice(h * HEAD_DIM, (h + 1) * HEAD_DIM)
            s = lax.dot_general(q_ref[rows, cols], kcat_ref[win, cols], dn,
                                preferred_element_type=F32) + bias_sb
            m = jnp.max(s, axis=-1, keepdims=True)
            e = jnp.exp2(s - m)
            den = jnp.sum(e, axis=-1, keepdims=True)
            o = jnp.dot(e.astype(BF16), vcat_ref[win, cols], preferred_element_type=F32) / den
            o_ref[rows, cols] = o
            l_ref[rows, cols] = jnp.broadcast_to(m + jnp.log2(den), (NEAR_SUB, HEAD_DIM))


def _near(nat3):
    b, s, _ = nat3.shape

    def cur(c):
        return lambda bi, qi: (bi, qi, c)

    def prev(c):
        return lambda bi, qi: (bi, jnp.maximum(qi - 1, 0), c)

    blk = (None, NEAR_QB, W_HEADS)
    out_spec = pl.BlockSpec(blk, lambda bi, qi: (bi, qi, 0))
    return pl.pallas_call(
        _near_kernel,
        grid=(b, s // NEAR_QB),
        in_specs=[
            pl.BlockSpec(blk, cur(COL_QA)),
            pl.BlockSpec(blk, prev(COL_KA)), pl.BlockSpec(blk, cur(COL_KA)),
            pl.BlockSpec(blk, prev(COL_VA)), pl.BlockSpec(blk, cur(COL_VA)),
        ],
        out_specs=[out_spec, out_spec],
        out_shape=[jax.ShapeDtypeStruct((b, s, W_HEADS), F32)] * 2,
        scratch_shapes=[pltpu.VMEM((2 * NEAR_QB, W_HEADS), BF16)] * 2,
        compiler_params=pltpu.CompilerParams(
            dimension_semantics=("arbitrary", "arbitrary"),
            vmem_limit_bytes=40 * MIB),
        name="dilated_near",
    )(nat3, nat3, nat3, nat3, nat3)


FAR_ROWS = BLOCK * FAR_DIL


def _far_kernel(q_ref, kp_ref, k_ref, vp_ref, v_ref, on_ref, ln_ref, y_ref, kcat_ref, vcat_ref):
    first = pl.program_id(1) == 0
    kcat_ref[:BLOCK, :] = kp_ref[...]
    kcat_ref[BLOCK:, :] = k_ref[...]
    vcat_ref[:BLOCK, :] = vp_ref[...]
    vcat_ref[BLOCK:, :] = v_ref[...]
    qi = lax.broadcasted_iota(jnp.int32, (BLOCK, 2 * BLOCK), 0)
    col = lax.broadcasted_iota(jnp.int32, (BLOCK, 2 * BLOCK), 1)
    dist = qi + BLOCK - col
    bias = jnp.where((dist >= 0) & (dist <= BLOCK), 0.0, NEG_INF)
    bias = jnp.where(first & (col < BLOCK), NEG_INF, bias)
    dn = (((1,), (1,)), ((), ()))

    for r in range(FAR_DIL):
        lanes = slice(r * HEAD_DIM, (r + 1) * HEAD_DIM)
        tokens = pl.ds(r, BLOCK, stride=FAR_DIL)
        s = lax.dot_general(q_ref[:, lanes], kcat_ref[:, lanes], dn, preferred_element_type=F32) + bias
        m = jnp.max(s, axis=-1, keepdims=True)
        e = jnp.exp2(s - m)
        den = jnp.sum(e, axis=-1, keepdims=True)
        o = jnp.dot(e.astype(BF16), vcat_ref[:, lanes], preferred_element_type=F32) / den
        lse = m + jnp.log2(den)
        l_near = ln_ref[tokens, :]
        mm = jnp.maximum(l_near, lse)
        w_near = jnp.exp2(l_near - mm)
        w_far = jnp.exp2(lse - mm)
        y_ref[tokens, :] = (on_ref[tokens, :] * w_near + o * w_far) / (w_near + w_far)


def _far(ph3, o_near, l_near):
    b, m, _ = ph3.shape
    s = m * FAR_DIL

    def cur(seg):
        return lambda bi, mi, h: (bi, mi, seg * N_HEADS + h)

    def prev(seg):
        return lambda bi, mi, h: (bi, jnp.maximum(mi - 1, 0), seg * N_HEADS + h)

    blk = (None, BLOCK, PH_HEAD)
    tok_spec = pl.BlockSpec((None, FAR_ROWS, HEAD_DIM), lambda bi, mi, h: (bi, mi, h))
    return pl.pallas_call(
        _far_kernel,
        grid=(b, m // BLOCK, N_HEADS),
        in_specs=[
            pl.BlockSpec(blk, cur(0)),
            pl.BlockSpec(blk, prev(1)), pl.BlockSpec(blk, cur(1)),
            pl.BlockSpec(blk, prev(2)), pl.BlockSpec(blk, cur(2)),
            tok_spec, tok_spec,
        ],
        out_specs=tok_spec,
        out_shape=jax.ShapeDtypeStruct((b, s, W_HEADS), F32),
        scratch_shapes=[pltpu.VMEM((2 * BLOCK, PH_HEAD), BF16)] * 2,
        compiler_params=pltpu.CompilerParams(
            dimension_semantics=("arbitrary", "arbitrary", "arbitrary"),
            vmem_limit_bytes=32 * MIB),
        name="dilated_far",
    )(ph3, ph3, ph3, ph3, ph3, o_near, l_near)


SB_BLK = 256
SB_DONE = 160.0


def _sb_block(q, k, v, tot, later, causal):
    dn = (((1,), (1,)), ((), ()))
    z = lax.dot_general(q, k, dn, preferred_element_type=F32)
    sp = jnp.maximum(z, 0.0) + jnp.log2(1.0 + jnp.exp2(-jnp.abs(z)))
    if causal is not None:
        sp = jnp.where(causal, sp, 0.0)
    cs = jnp.dot(sp.astype(BF16), later, preferred_element_type=F32)
    arg = (z - sp) - cs - jnp.concatenate([tot, tot], axis=1)
    if causal is not None:
        arg = jnp.where(causal, arg, NEG_INF)
    pv = jnp.dot(jnp.exp2(arg).astype(BF16), v, preferred_element_type=F32)
    rs = jnp.sum(sp, axis=1, keepdims=True)
    return pv, jnp.broadcast_to(rs, (SB_BLK, HEAD_DIM))


def _sb_first_two(q, k2, v2, later, valid):
    dn = (((1,), (1,)), ((), ()))
    z = lax.dot_general(q, k2, dn, preferred_element_type=F32)
    sp = jnp.where(valid, jnp.maximum(z, 0.0) + jnp.log2(1.0 + jnp.exp2(-jnp.abs(z))), 0.0)
    sp_prev, sp_diag = sp[:, :SB_BLK], sp[:, SB_BLK:]
    cs = jnp.concatenate([jnp.dot(sp_prev.astype(BF16), later, preferred_element_type=F32),
                          jnp.dot(sp_diag.astype(BF16), later, preferred_element_type=F32)], axis=1)
    rs_diag = jnp.broadcast_to(jnp.sum(sp_diag, axis=1, keepdims=True), (SB_BLK, HEAD_DIM))
    rs_prev = jnp.broadcast_to(jnp.sum(sp_prev, axis=1, keepdims=True), (SB_BLK, HEAD_DIM))
    zero = jnp.zeros((SB_BLK, HEAD_DIM), F32)
    behind = jnp.concatenate([rs_diag, rs_diag, zero, zero], axis=1)
    arg = jnp.where(valid, (z - sp) - cs - behind, NEG_INF)
    pv = jnp.dot(jnp.exp2(arg).astype(BF16), v2, preferred_element_type=F32)
    return pv, rs_diag + rs_prev


def _stickbreak_kernel(q_ref, k_ref, v_ref, o_ref, acc_ref, tot_ref):
    qi = pl.program_id(1)
    row = lax.broadcasted_iota(jnp.int32, (SB_BLK, SB_BLK), 0)
    col = lax.broadcasted_iota(jnp.int32, (SB_BLK, SB_BLK), 1)
    later = (row > col).astype(BF16)
    causal = col < row
    heads = [slice(h * HEAD_DIM, (h + 1) * HEAD_DIM) for h in range(N_HEADS)]

    def key_rows(kb, nblk=1):
        return pl.ds(pl.multiple_of(kb * SB_BLK, SB_BLK), nblk * SB_BLK)

    @pl.when(qi == 0)
    def _():
        rows = key_rows(qi)
        zero = jnp.zeros((SB_BLK, HEAD_DIM), F32)
        for hs in heads:
            pv, rs = _sb_block(q_ref[:, hs], k_ref[rows, hs], v_ref[rows, hs], zero, later, causal)
            acc_ref[:, hs] = pv
            tot_ref[:, hs] = rs

    @pl.when(qi > 0)
    def _():
        rows = key_rows(qi - 1, 2)
        row2 = lax.broadcasted_iota(jnp.int32, (SB_BLK, 2 * SB_BLK), 0)
        col2 = lax.broadcasted_iota(jnp.int32, (SB_BLK, 2 * SB_BLK), 1)
        valid = col2 < row2 + SB_BLK
        for hs in heads:
            pv, rs = _sb_first_two(q_ref[:, hs], k_ref[rows, hs], v_ref[rows, hs], later, valid)
            acc_ref[:, hs] = pv
            tot_ref[:, hs] = rs

    def earlier_block(kb):
        rows = key_rows(kb)
        for hs in heads:
            pv, rs = _sb_block(q_ref[:, hs], k_ref[rows, hs], v_ref[rows, hs],
                               tot_ref[:, hs], later, None)
            acc_ref[:, hs] += pv
            tot_ref[:, hs] += rs

    def cond(c):
        kb, consumed = c
        return (kb >= 0) & (consumed < SB_DONE)

    def body(c):
        kb, _ = c
        earlier_block(kb)
        return kb - 1, jnp.min(tot_ref[...])

    lax.while_loop(cond, body, (qi - 2, jnp.min(tot_ref[...])))
    o_ref[...] = acc_ref[...].astype(o_ref.dtype)


def _stickbreak(proj3):
    b, s, _ = proj3.shape
    return pl.pallas_call(
        _stickbreak_kernel,
        grid=(b, s // SB_BLK),
        in_specs=[
            pl.BlockSpec((None, SB_BLK, W_HEADS), lambda bi, qi: (bi, qi, COL_QB)),
            pl.BlockSpec((None, s, W_HEADS), lambda bi, qi: (bi, 0, COL_KB),
                         pipeline_mode=pl.Buffered(1)),
            pl.BlockSpec((None, s, W_HEADS), lambda bi, qi: (bi, 0, COL_VB),
                         pipeline_mode=pl.Buffered(1)),
        ],
        out_specs=pl.BlockSpec((None, SB_BLK, W_HEADS), lambda bi, qi: (bi, qi, 0)),
        out_shape=jax.ShapeDtypeStruct((b, s, W_HEADS), BF16),
        scratch_shapes=[pltpu.VMEM((SB_BLK, W_HEADS), F32), pltpu.VMEM((SB_BLK, W_HEADS), F32)],
        compiler_params=pltpu.CompilerParams(
            dimension_semantics=("arbitrary", "arbitrary"),
            vmem_limit_bytes=48 * MIB),
        name="stickbreak",
    )(proj3, proj3, proj3)


MIX_TM = 512


def _mixout_kernel(x_ref, ya_ref, yb_ref, ga_ref, gb_ref, wa_ref, wb_ref, wo_ref, o_ref):
    ta = jnp.dot(ya_ref[...].astype(BF16), wa_ref[...], preferred_element_type=F32)
    tb = jnp.dot(yb_ref[...], wb_ref[...], preferred_element_type=F32)
    merged = (jax.nn.sigmoid(ga_ref[...].astype(F32)) * ta
              + jax.nn.sigmoid(gb_ref[...].astype(F32)) * tb)
    o_ref[...] = x_ref[...] + jnp.dot(merged.astype(BF16), wo_ref[...], preferred_element_type=F32)


def _resident(shape):
    return pl.BlockSpec(shape, lambda *_: (0,) * len(shape), pipeline_mode=pl.Buffered(1))


def _mixout(x2d, ya2d, yb2d, proj2d, wa, wb, wo):
    t = x2d.shape[0]
    tm = MIX_TM
    ga_col = (6 * W_HEADS) // D_MODEL
    gb_col = ga_col + 1
    return pl.pallas_call(
        _mixout_kernel,
        grid=(t // tm,),
        in_specs=[
            pl.BlockSpec((tm, D_MODEL), lambda i: (i, 0)),
            pl.BlockSpec((tm, W_HEADS), lambda i: (i, 0)),
            pl.BlockSpec((tm, W_HEADS), lambda i: (i, 0)),
            pl.BlockSpec((tm, D_MODEL), lambda i: (i, ga_col)),
            pl.BlockSpec((tm, D_MODEL), lambda i: (i, gb_col)),
            _resident((W_HEADS, D_MODEL)),
            _resident((W_HEADS, D_MODEL)),
            _resident((D_MODEL, D_MODEL)),
        ],
        out_specs=pl.BlockSpec((tm, D_MODEL), lambda i: (i, 0)),
        out_shape=jax.ShapeDtypeStruct((t, D_MODEL), F32),
        compiler_params=pltpu.CompilerParams(
            dimension_semantics=("arbitrary",),
            vmem_limit_bytes=56 * MIB),
        name="mixout",
    )(x2d, ya2d, yb2d, proj2d, proj2d, wa, wb, wo)


MLP_TM = 1024
MLP_TF = 512


def _mlp_kernel(x_ref, g_ref, wu_ref, wd_ref, o_ref, hn_ref):
    f = pl.program_id(1)

    def chunk(hn):
        u = jnp.dot(hn, wu_ref[...], preferred_element_type=F32)
        a = jnp.square(jnp.maximum(u, 0.0)).astype(BF16)
        return jnp.dot(a, wd_ref[...], preferred_element_type=F32)

    @pl.when(f == 0)
    def _():
        x = x_ref[...]
        hn = _rms_scale(x, g_ref[...]).astype(BF16)
        hn_ref[...] = hn
        o_ref[...] = x + chunk(hn)

    @pl.when(f > 0)
    def _():
        o_ref[...] += chunk(hn_ref[...])


def _mlp(x2d, g, wu, wd):
    t = x2d.shape[0]
    tm, tf = MLP_TM, MLP_TF
    return pl.pallas_call(
        _mlp_kernel,
        grid=(t // tm, D_FF // tf),
        in_specs=[
            pl.BlockSpec((tm, D_MODEL), lambda i, f: (i, 0)),
            pl.BlockSpec((1, D_MODEL), lambda i, f: (0, 0)),
            pl.BlockSpec((None, D_MODEL, tf), lambda i, f: (f, 0, 0)),
            pl.BlockSpec((tf, D_MODEL), lambda i, f: (f, 0)),
        ],
        out_specs=pl.BlockSpec((tm, D_MODEL), lambda i, f: (i, 0)),
        out_shape=jax.ShapeDtypeStruct((t, D_MODEL), F32),
        scratch_shapes=[pltpu.VMEM((tm, D_MODEL), BF16)],
        compiler_params=pltpu.CompilerParams(
            dimension_semantics=("arbitrary", "arbitrary"),
            vmem_limit_bytes=56 * MIB),
        name="mlp",
    )(x2d, g, wu, wd)


PLE_TM = 512


def _ple_kernel(x_ref, p_ref, g_ref, wg_ref, wp_ref, o_ref):
    x = x_ref[...]
    hn = _rms_scale(x, g_ref[...]).astype(BF16)
    gate = jnp.dot(hn, wg_ref[...], preferred_element_type=F32)
    pp = jnp.dot(p_ref[...].astype(BF16), wp_ref[...], preferred_element_type=F32)
    o_ref[...] = x + pp * jax.nn.sigmoid(gate)


def _ple(x2d, p2d, g, wg, wp):
    t = x2d.shape[0]
    tm = PLE_TM
    ple_dim = p2d.shape[1]
    return pl.pallas_call(
        _ple_kernel,
        grid=(t // tm,),
        in_specs=[
            pl.BlockSpec((tm, D_MODEL), lambda i: (i, 0)),
            pl.BlockSpec((tm, ple_dim), lambda i: (i, 0)),
            pl.BlockSpec((1, D_MODEL), lambda i: (0, 0)),
            _resident((D_MODEL, D_MODEL)),
            _resident((ple_dim, D_MODEL)),
        ],
        out_specs=pl.BlockSpec((tm, D_MODEL), lambda i: (i, 0)),
        out_shape=jax.ShapeDtypeStruct((t, D_MODEL), F32),
        compiler_params=pltpu.CompilerParams(
            dimension_semantics=("arbitrary",),
            vmem_limit_bytes=40 * MIB),
        name="ple",
    )(x2d, p2d, g, wg, wp)


def _rope_tables(seq):
    half = ROT_DIM // 2
    pos = jnp.arange(seq, dtype=F32)
    inv = ROPE_THETA ** (-jnp.arange(0, ROT_DIM, 2, dtype=F32) / ROT_DIM)
    ang = pos[:, None] * inv[None, :]
    cos, sin = jnp.cos(ang), jnp.sin(ang)
    rest = HEAD_DIM - ROT_DIM
    cos_t = jnp.concatenate([cos, cos, jnp.ones((seq, rest), F32)], axis=-1)
    zeros_h = jnp.zeros((seq, half), F32)
    zeros_r = jnp.zeros((seq, rest), F32)
    sa_t = jnp.concatenate([-sin, zeros_h, zeros_r], axis=-1)
    sb_t = jnp.concatenate([zeros_h, sin, zeros_r], axis=-1)
    return cos_t, sa_t, sb_t


def _column_tiles(w, width):
    k, n = w.shape
    return w.astype(BF16).reshape(k, n // width, width).transpose(1, 0, 2)


def kernel(x, p, g_mix, w_in, qn_gain, kn_gain, w_branch_a, w_branch_b, w_out,
           g_mlp, w_up, w_down, g_ple, w_ple_gate, w_ple_proj):
    b, s, d = x.shape
    t = b * s
    depth = w_in.shape[0]
    cos_t, sa_t, sb_t = _rope_tables(s)
    x2d = x.reshape(t, d)
    for i in range(depth):
        proj2d, nat2d, ph2d = _inproj(x2d, g_mix[i][None, :], _column_tiles(w_in[i], W_HEADS),
                                      qn_gain[i][None, :], kn_gain[i][None, :], cos_t, sa_t, sb_t, s)
        proj3 = proj2d.reshape(b, s, D_IN)

        o_near, l_near = _near(nat2d.reshape(b, s, N_MIXA * W_HEADS))
        ya = _far(ph2d.reshape(b, s // FAR_DIL, N_MIXA * PH_SEG), o_near, l_near)
        ya2d = ya.reshape(t, W_HEADS)
        yb2d = _stickbreak(proj3).reshape(t, W_HEADS)

        x2d = _mixout(x2d, ya2d, yb2d, proj2d, w_branch_a[i].astype(BF16),
                      w_branch_b[i].astype(BF16), w_out[i].astype(BF16))
        x2d = _mlp(x2d, g_mlp[i][None, :], _column_tiles(w_up[i], MLP_TF), w_down[i].astype(BF16))
        x2d = _ple(x2d, p[i].reshape(t, -1), g_ple[i][None, :],
                   w_ple_gate[i].astype(BF16), w_ple_proj[i].astype(BF16))
    return x2d.reshape(b, s, d)
```

```python
import jax
import jax.numpy as jnp
from jax import lax
from jax.experimental import pallas as pl
from jax.experimental.pallas import tpu as pltpu

D_MODEL = 2048
HEAD_DIM = 128
N_HEADS = 8
W_HEADS = N_HEADS * HEAD_DIM
DIL_PATTERNS = ((128, 1), (512, 4), (2048, 16))
BLOCK = 128
ROT_DIM = HEAD_DIM // 4
ROPE_THETA = 500000.0
D_FF = 4 * D_MODEL
EPS = 1e-6
D_IN = 6 * W_HEADS + 2 * D_MODEL
LOG2E = 1.4426950408889634
Q_SCALE = HEAD_DIM ** -0.5 * LOG2E

COL_QA, COL_KA, COL_VA, COL_QB, COL_KB, COL_VB = 0, 1, 2, 3, 4, 5
N_COL_TILES = D_IN // W_HEADS
PROJ_SKIP = 2
D_PROJ = D_IN - PROJ_SKIP * W_HEADS
P_VA, P_QB, P_KB, P_VB = (c - PROJ_SKIP for c in (COL_VA, COL_QB, COL_KB, COL_VB))

F32 = jnp.float32
BF16 = jnp.bfloat16
NEG_INF = float("-inf")

MIB = 1024 * 1024


def _rms_scale(x, gain):
    ms = jnp.mean(x * x, axis=-1, keepdims=True)
    return x * lax.rsqrt(ms + EPS) * gain


INPROJ_TM = 512


FAR_DIL = DIL_PATTERNS[2][1]
PH_ROWS = INPROJ_TM // FAR_DIL
PH_HEAD = FAR_DIL * HEAD_DIM
PH_SEG = N_HEADS * PH_HEAD


ROW_CHUNK = 128
SUB_DIL = 4
ROLL_DN = HEAD_DIM - ROT_DIM // 2
ROLL_UP = ROT_DIM // 2


N_MIXA = 3
HEAD_GROUPS = ((0, 1, 2), (3, 4, 5), (6, 7))


def _inproj_kernel(x_ref, g_ref, w_ref, qn_ref, kn_ref, cos_ref, sa_ref, sb_ref,
                   o_ref, nat_ref, ph_ref, hn_ref, tab_ref, raw_ref, ys_ref, yq_ref):
    j = pl.program_id(1)
    tm = x_ref.shape[0]

    @pl.when(j == 0)
    def _():
        hn_ref[...] = _rms_scale(x_ref[...], g_ref[...]).astype(BF16)
        for n, (gain_ref, scale) in enumerate(((qn_ref, Q_SCALE), (kn_ref, 1.0))):
            g = jnp.broadcast_to(gain_ref[...] * scale, (tm, HEAD_DIM))
            tab_ref[3 * n] = cos_ref[...] * g
            tab_ref[3 * n + 1] = sa_ref[...] * pltpu.roll(g, ROLL_DN, 1)
            tab_ref[3 * n + 2] = sb_ref[...] * pltpu.roll(g, ROLL_UP, 1)

    def project():
        return jnp.dot(hn_ref[...], w_ref[...], preferred_element_type=F32)

    def finish_mixer_a(tile, head_group):
        tab = 3 * tile
        for h in head_group:
            sl = slice(h * HEAD_DIM, (h + 1) * HEAD_DIM)
            for c in range(tm // ROW_CHUNK):
                rows = slice(c * ROW_CHUNK, (c + 1) * ROW_CHUNK)
                a = raw_ref[tile, rows, sl]
                if tile < N_MIXA - 1:
                    rs = lax.rsqrt(jnp.mean(a * a, axis=-1, keepdims=True) + EPS)
                    a = (a * tab_ref[tab, rows, :]
                         + pltpu.roll(a, ROLL_DN, 1) * tab_ref[tab + 1, rows, :]
                         + pltpu.roll(a, ROLL_UP, 1) * tab_ref[tab + 2, rows, :]) * rs
                    nat_ref[rows, sl] = a.astype(BF16)
                ys_ref[h, rows, :] = a
            for r_lo in range(SUB_DIL):
                yq_ref[h, r_lo] = ys_ref[h, pl.ds(r_lo, tm // SUB_DIL, stride=SUB_DIL), :]
                for r_hi in range(FAR_DIL // SUB_DIL):
                    r = r_lo + SUB_DIL * r_hi
                    lanes = slice(h * PH_HEAD + r * HEAD_DIM, h * PH_HEAD + (r + 1) * HEAD_DIM)
                    ph_ref[:, lanes] = yq_ref[h, r_lo, pl.ds(r_hi, PH_ROWS, stride=SUB_DIL), :].astype(BF16)

    @pl.when(j == 0)
    def _():
        raw_ref[0] = project()

    assert N_COL_TILES == 1 + N_MIXA * len(HEAD_GROUPS)
    for step in range(1, N_COL_TILES):
        @pl.when(j == step)
        def _(step=step):
            acc = project()
            if step < N_MIXA:
                raw_ref[step] = acc
            if step == COL_QB:
                acc = acc * Q_SCALE
            if step >= PROJ_SKIP:
                o_ref[...] = acc.astype(BF16)
            finish_mixer_a((step - 1) // len(HEAD_GROUPS), HEAD_GROUPS[(step - 1) % len(HEAD_GROUPS)])


def _inproj(x2d, g, w_bf16, qn, kn, cos_t, sa_t, sb_t, seq):
    t = x2d.shape[0]
    tm = INPROJ_TM
    nseq = seq // tm
    tab_spec = pl.BlockSpec((tm, HEAD_DIM), lambda i, j: (i % nseq, 0))
    vec_spec = pl.BlockSpec((1, HEAD_DIM), lambda i, j: (0, 0))
    return pl.pallas_call(
        _inproj_kernel,
        grid=(t // tm, N_COL_TILES),
        in_specs=[
            pl.BlockSpec((tm, D_MODEL), lambda i, j: (i, 0)),
            pl.BlockSpec((1, D_MODEL), lambda i, j: (0, 0)),
            pl.BlockSpec((D_MODEL, W_HEADS), lambda i, j: (0, j)),
            vec_spec, vec_spec, tab_spec, tab_spec, tab_spec,
        ],
        out_specs=[
            pl.BlockSpec((tm, W_HEADS), lambda i, j: (i, jnp.maximum(j - PROJ_SKIP, 0))),
            pl.BlockSpec((tm, W_HEADS), lambda i, j: (
                i, jnp.minimum(jnp.maximum(j - 1, 0) // len(HEAD_GROUPS), N_MIXA - 2))),
            pl.BlockSpec((PH_ROWS, PH_SEG), lambda i, j: (i, jnp.maximum(j - 1, 0) // len(HEAD_GROUPS))),
        ],
        out_shape=[
            jax.ShapeDtypeStruct((t, D_PROJ), BF16),
            jax.ShapeDtypeStruct((t, (N_MIXA - 1) * W_HEADS), BF16),
            jax.ShapeDtypeStruct((t // FAR_DIL, N_MIXA * PH_SEG), BF16),
        ],
        scratch_shapes=[
            pltpu.VMEM((tm, D_MODEL), BF16),
            pltpu.VMEM((6, tm, HEAD_DIM), F32),
            pltpu.VMEM((N_MIXA, tm, W_HEADS), F32),
            pltpu.VMEM((N_HEADS, tm, HEAD_DIM), F32),
            pltpu.VMEM((N_HEADS, SUB_DIL, tm // SUB_DIL, HEAD_DIM), F32),
        ],
        compiler_params=pltpu.CompilerParams(
            dimension_semantics=("arbitrary", "arbitrary"),
            vmem_limit_bytes=48 * MIB),
        name="inproj",
    )(x2d, g, w_bf16, qn, kn, cos_t, sa_t, sb_t)


NEAR_QB = 512
NEAR_SUB = 256
NEAR_WIN = DIL_PATTERNS[1][0] + NEAR_SUB


def _near_kernel(q_ref, kp_ref, k_ref, vp_ref, v_ref, o_ref, l_ref, kcat_ref, vcat_ref):
    first = pl.program_id(1) == 0
    kcat_ref[:NEAR_QB, :] = kp_ref[...]
    kcat_ref[NEAR_QB:, :] = k_ref[...]
    vcat_ref[:NEAR_QB, :] = vp_ref[...]
    vcat_ref[NEAR_QB:, :] = v_ref[...]

    qi = lax.broadcasted_iota(jnp.int32, (NEAR_SUB, NEAR_WIN), 0)
    col = lax.broadcasted_iota(jnp.int32, (NEAR_SUB, NEAR_WIN), 1)
    dist = qi + (NEAR_WIN - NEAR_SUB) - col
    (w1, _), (w4, d4) = DIL_PATTERNS[0], DIL_PATTERNS[1]
    assert d4 & (d4 - 1) == 0
    in1 = (dist >= 0) & (dist <= w1)
    in4 = (dist >= 0) & (dist <= w4) & ((dist & (d4 - 1)) == 0)
    bias = jnp.where(in1 & in4, 1.0, jnp.where(in1 | in4, 0.0, NEG_INF))
    dn = (((1,), (1,)), ((), ()))

    for sb in range(NEAR_QB // NEAR_SUB):
        rows = slice(sb * NEAR_SUB, (sb + 1) * NEAR_SUB)
        win = slice(sb * NEAR_SUB, sb * NEAR_SUB + NEAR_WIN)
        bias_sb = jnp.where(first & (col < NEAR_QB - sb * NEAR_SUB), NEG_INF, bias)
        for h in range(N_HEADS):
            cols = slice(h * HEAD_DIM, (h + 1) * HEAD_DIM)
            s = lax.dot_general(q_ref[rows, cols], kcat_ref[win, cols], dn,
                                preferred_element_type=F32) + bias_sb
            m = jnp.max(s, axis=-1, keepdims=True)
            e = jnp.exp2(s - m)
            den = jnp.sum(e, axis=-1, keepdims=True)
            o = jnp.dot(e.astype(BF16), vcat_ref[win, cols], preferred_element_type=F32) / den
            o_ref[h, rows, :] = o
            l_ref[h, rows, :] = jnp.broadcast_to(m + jnp.log2(den), (NEAR_SUB, HEAD_DIM))


def _near(nat3, proj3):
    b, s, _ = nat3.shape

    def cur(c):
        return lambda bi, qi: (bi, qi, c)

    def prev(c):
        return lambda bi, qi: (bi, jnp.maximum(qi - 1, 0), c)

    blk = (None, NEAR_QB, W_HEADS)
    out_spec = pl.BlockSpec((None, N_HEADS, NEAR_QB, HEAD_DIM), lambda bi, qi: (bi, 0, qi, 0))
    return pl.pallas_call(
        _near_kernel,
        grid=(b, s // NEAR_QB),
        in_specs=[
            pl.BlockSpec(blk, cur(COL_QA)),
            pl.BlockSpec(blk, prev(COL_KA)), pl.BlockSpec(blk, cur(COL_KA)),
            pl.BlockSpec(blk, prev(P_VA)), pl.BlockSpec(blk, cur(P_VA)),
        ],
        out_specs=[out_spec, out_spec],
        out_shape=[jax.ShapeDtypeStruct((b, N_HEADS, s, HEAD_DIM), F32)] * 2,
        scratch_shapes=[pltpu.VMEM((2 * NEAR_QB, W_HEADS), BF16)] * 2,
        compiler_params=pltpu.CompilerParams(
            dimension_semantics=("arbitrary", "arbitrary"),
            vmem_limit_bytes=40 * MIB),
        name="dilated_near",
    )(nat3, nat3, nat3, proj3, proj3)


FAR_ROWS = BLOCK * FAR_DIL


def _far_kernel(q_ref, kp_ref, k_ref, vp_ref, v_ref, on_ref, ln_ref, y_ref, kcat_ref, vcat_ref):
    first = pl.program_id(1) == 0
    kcat_ref[:BLOCK, :] = kp_ref[...]
    kcat_ref[BLOCK:, :] = k_ref[...]
    vcat_ref[:BLOCK, :] = vp_ref[...]
    vcat_ref[BLOCK:, :] = v_ref[...]
    qi = lax.broadcasted_iota(jnp.int32, (BLOCK, 2 * BLOCK), 0)
    col = lax.broadcasted_iota(jnp.int32, (BLOCK, 2 * BLOCK), 1)
    dist = qi + BLOCK - col
    bias = jnp.where((dist >= 0) & (dist <= BLOCK), 0.0, NEG_INF)
    bias = jnp.where(first & (col < BLOCK), NEG_INF, bias)
    dn = (((1,), (1,)), ((), ()))

    for r in range(FAR_DIL):
        lanes = slice(r * HEAD_DIM, (r + 1) * HEAD_DIM)
        tokens = pl.ds(r, BLOCK, stride=FAR_DIL)
        s = lax.dot_general(q_ref[:, lanes], kcat_ref[:, lanes], dn, preferred_element_type=F32) + bias
        m = jnp.max(s, axis=-1, keepdims=True)
        e = jnp.exp2(s - m)
        den = jnp.sum(e, axis=-1, keepdims=True)
        o = jnp.dot(e.astype(BF16), vcat_ref[:, lanes], preferred_element_type=F32) / den
        lse = m + jnp.log2(den)
        l_near = ln_ref[tokens, :]
        mm = jnp.maximum(l_near, lse)
        w_near = jnp.exp2(l_near - mm)
        w_far = jnp.exp2(lse - mm)
        y_ref[tokens, :] = (on_ref[tokens, :] * w_near + o * w_far) / (w_near + w_far)


def _far(ph3, o_near, l_near):
    b, m, _ = ph3.shape
    s = m * FAR_DIL

    def cur(seg):
        return lambda bi, mi, h: (bi, mi, seg * N_HEADS + h)

    def prev(seg):
        return lambda bi, mi, h: (bi, jnp.maximum(mi - 1, 0), seg * N_HEADS + h)

    blk = (None, BLOCK, PH_HEAD)
    tok_spec = pl.BlockSpec((None, None, FAR_ROWS, HEAD_DIM), lambda bi, mi, h: (bi, h, mi, 0))
    return pl.pallas_call(
        _far_kernel,
        grid=(b, m // BLOCK, N_HEADS),
        in_specs=[
            pl.BlockSpec(blk, cur(0)),
            pl.BlockSpec(blk, prev(1)), pl.BlockSpec(blk, cur(1)),
            pl.BlockSpec(blk, prev(2)), pl.BlockSpec(blk, cur(2)),
            tok_spec, tok_spec,
        ],
        out_specs=tok_spec,
        out_shape=jax.ShapeDtypeStruct((b, N_HEADS, s, HEAD_DIM), F32),
        scratch_shapes=[pltpu.VMEM((2 * BLOCK, PH_HEAD), BF16)] * 2,
        compiler_params=pltpu.CompilerParams(
            dimension_semantics=("arbitrary", "arbitrary", "arbitrary"),
            vmem_limit_bytes=32 * MIB),
        name="dilated_far",
    )(ph3, ph3, ph3, ph3, ph3, o_near, l_near)


SB_BLK = 256
SB_DONE = 160.0


def _sb_block(q, k, v, tot, later, causal):
    dn = (((1,), (1,)), ((), ()))
    z = lax.dot_general(q, k, dn, preferred_element_type=F32)
    sp = jnp.maximum(z, 0.0) + jnp.log2(1.0 + jnp.exp2(-jnp.abs(z)))
    if causal is not None:
        sp = jnp.where(causal, sp, 0.0)
    cs = jnp.dot(sp.astype(BF16), later, preferred_element_type=F32)
    arg = (z - sp) - cs - jnp.concatenate([tot, tot], axis=1)
    if causal is not None:
        arg = jnp.where(causal, arg, NEG_INF)
    pv = jnp.dot(jnp.exp2(arg).astype(BF16), v, preferred_element_type=F32)
    rs = jnp.sum(sp, axis=1, keepdims=True)
    return pv, jnp.broadcast_to(rs, (SB_BLK, HEAD_DIM))


def _sb_first_two(q, k2, v2, later, causal):
    dn = (((1,), (1,)), ((), ()))
    z = lax.dot_general(q, k2, dn, preferred_element_type=F32)
    sp = jnp.maximum(z, 0.0) + jnp.log2(1.0 + jnp.exp2(-jnp.abs(z)))
    sp_prev = sp[:, :SB_BLK]
    sp_diag = jnp.where(causal, sp[:, SB_BLK:], 0.0)
    cs_prev = jnp.dot(sp_prev.astype(BF16), later, preferred_element_type=F32)
    cs_diag = jnp.dot(sp_diag.astype(BF16), later, preferred_element_type=F32)
    rs_diag = jnp.broadcast_to(jnp.sum(sp_diag, axis=1, keepdims=True), (SB_BLK, HEAD_DIM))
    rs_prev = jnp.broadcast_to(jnp.sum(sp_prev, axis=1, keepdims=True), (SB_BLK, HEAD_DIM))
    arg_prev = (z[:, :SB_BLK] - sp_prev) - cs_prev - jnp.concatenate([rs_diag, rs_diag], axis=1)
    arg_diag = jnp.where(causal, (z[:, SB_BLK:] - sp_diag) - cs_diag, NEG_INF)
    a = jnp.exp2(jnp.concatenate([arg_prev, arg_diag], axis=1)).astype(BF16)
    pv = jnp.dot(a, v2, preferred_element_type=F32)
    return pv, rs_diag + rs_prev


def _stickbreak_kernel(q_ref, k_ref, v_ref, o_ref, acc_ref, tot_ref):
    qi = pl.program_id(1)
    row = lax.broadcasted_iota(jnp.int32, (SB_BLK, SB_BLK), 0)
    col = lax.broadcasted_iota(jnp.int32, (SB_BLK, SB_BLK), 1)
    later = (row > col).astype(BF16)
    causal = col < row
    heads = [slice(h * HEAD_DIM, (h + 1) * HEAD_DIM) for h in range(N_HEADS)]

    def key_rows(kb, nblk=1):
        return pl.ds(pl.multiple_of(kb * SB_BLK, SB_BLK), nblk * SB_BLK)

    @pl.when(qi == 0)
    def _():
        rows = key_rows(qi)
        zero = jnp.zeros((SB_BLK, HEAD_DIM), F32)
        for hs in heads:
            pv, rs = _sb_block(q_ref[:, hs], k_ref[rows, hs], v_ref[rows, hs], zero, later, causal)
            acc_ref[:, hs] = pv
            tot_ref[:, hs] = rs

    @pl.when(qi > 0)
    def _():
        rows = key_rows(qi - 1, 2)
        for hs in heads:
            pv, rs = _sb_first_two(q_ref[:, hs], k_ref[rows, hs], v_ref[rows, hs], later, causal)
            acc_ref[:, hs] = pv
            tot_ref[:, hs] = rs

    def earlier_block(kb):
        rows = key_rows(kb)
        for hs in heads:
            pv, rs = _sb_block(q_ref[:, hs], k_ref[rows, hs], v_ref[rows, hs],
                               tot_ref[:, hs], later, None)
            acc_ref[:, hs] += pv
            tot_ref[:, hs] += rs

    def cond(c):
        kb, consumed = c
        return (kb >= 0) & (consumed < SB_DONE)

    def body(c):
        kb, _ = c
        earlier_block(kb)
        return kb - 1, jnp.min(tot_ref[...])

    lax.while_loop(cond, body, (qi - 2, jnp.min(tot_ref[...])))
    o_ref[...] = acc_ref[...].astype(o_ref.dtype)


def _stickbreak(proj3):
    b, s, _ = proj3.shape
    return pl.pallas_call(
        _stickbreak_kernel,
        grid=(b, s // SB_BLK),
        in_specs=[
            pl.BlockSpec((None, SB_BLK, W_HEADS), lambda bi, qi: (bi, qi, P_QB)),
            pl.BlockSpec((None, s, W_HEADS), lambda bi, qi: (bi, 0, P_KB),
                         pipeline_mode=pl.Buffered(1)),
            pl.BlockSpec((None, s, W_HEADS), lambda bi, qi: (bi, 0, P_VB),
                         pipeline_mode=pl.Buffered(1)),
        ],
        out_specs=pl.BlockSpec((None, SB_BLK, W_HEADS), lambda bi, qi: (bi, qi, 0)),
        out_shape=jax.ShapeDtypeStruct((b, s, W_HEADS), BF16),
        scratch_shapes=[pltpu.VMEM((SB_BLK, W_HEADS), F32), pltpu.VMEM((SB_BLK, W_HEADS), F32)],
        compiler_params=pltpu.CompilerParams(
            dimension_semantics=("arbitrary", "arbitrary"),
            vmem_limit_bytes=48 * MIB),
        name="stickbreak",
    )(proj3, proj3, proj3)


MIX_TM = 512


def _mixout_kernel(x_ref, ya_ref, yb_ref, ga_ref, gb_ref, wa_ref, wb_ref, wo_ref, o_ref):
    ya = jnp.concatenate([ya_ref[h] for h in range(N_HEADS)], axis=1).astype(BF16)
    ta = jnp.dot(ya, wa_ref[...], preferred_element_type=F32)
    tb = jnp.dot(yb_ref[...], wb_ref[...], preferred_element_type=F32)
    merged = (jax.nn.sigmoid(ga_ref[...].astype(F32)) * ta
              + jax.nn.sigmoid(gb_ref[...].astype(F32)) * tb)
    o_ref[...] = x_ref[...] + jnp.dot(merged.astype(BF16), wo_ref[...], preferred_element_type=F32)


def _resident(shape):
    return pl.BlockSpec(shape, lambda *_: (0,) * len(shape), pipeline_mode=pl.Buffered(1))


def _mixout(x2d, ya4, yb2d, proj2d, wa, wb, wo):
    t = x2d.shape[0]
    tm = MIX_TM
    tiles_per_seq = ya4.shape[2] // tm
    ga_col = ((P_VB + 1) * W_HEADS) // D_MODEL
    assert ga_col * D_MODEL == (P_VB + 1) * W_HEADS
    gb_col = ga_col + 1
    return pl.pallas_call(
        _mixout_kernel,
        grid=(t // tm,),
        in_specs=[
            pl.BlockSpec((tm, D_MODEL), lambda i: (i, 0)),
            pl.BlockSpec((None, N_HEADS, tm, HEAD_DIM),
                         lambda i: (i // tiles_per_seq, 0, i % tiles_per_seq, 0)),
            pl.BlockSpec((tm, W_HEADS), lambda i: (i, 0)),
            pl.BlockSpec((tm, D_MODEL), lambda i: (i, ga_col)),
            pl.BlockSpec((tm, D_MODEL), lambda i: (i, gb_col)),
            _resident((W_HEADS, D_MODEL)),
            _resident((W_HEADS, D_MODEL)),
            _resident((D_MODEL, D_MODEL)),
        ],
        out_specs=pl.BlockSpec((tm, D_MODEL), lambda i: (i, 0)),
        out_shape=jax.ShapeDtypeStruct((t, D_MODEL), F32),
        compiler_params=pltpu.CompilerParams(
            dimension_semantics=("arbitrary",),
            vmem_limit_bytes=56 * MIB),
        name="mixout",
    )(x2d, ya4, yb2d, proj2d, proj2d, wa, wb, wo)


MLP_TM = 1024
MLP_TF = 512


def _mlp_kernel(x_ref, g_ref, wu_ref, wd_ref, o_ref, hn_ref):
    f = pl.program_id(1)

    def chunk(hn):
        u = jnp.dot(hn, wu_ref[...], preferred_element_type=F32)
        a = jnp.square(jnp.maximum(u, 0.0)).astype(BF16)
        return jnp.dot(a, wd_ref[...], preferred_element_type=F32)

    @pl.when(f == 0)
    def _():
        x = x_ref[...]
        hn = _rms_scale(x, g_ref[...]).astype(BF16)
        hn_ref[...] = hn
        o_ref[...] = x + chunk(hn)

    @pl.when(f > 0)
    def _():
        o_ref[...] += chunk(hn_ref[...])


def _mlp(x2d, g, wu, wd):
    t = x2d.shape[0]
    tm, tf = MLP_TM, MLP_TF
    return pl.pallas_call(
        _mlp_kernel,
        grid=(t // tm, D_FF // tf),
        in_specs=[
            pl.BlockSpec((tm, D_MODEL), lambda i, f: (i, 0)),
            pl.BlockSpec((1, D_MODEL), lambda i, f: (0, 0)),
            pl.BlockSpec((D_MODEL, tf), lambda i, f: (0, f)),
            pl.BlockSpec((tf, D_MODEL), lambda i, f: (f, 0)),
        ],
        out_specs=pl.BlockSpec((tm, D_MODEL), lambda i, f: (i, 0)),
        out_shape=jax.ShapeDtypeStruct((t, D_MODEL), F32),
        scratch_shapes=[pltpu.VMEM((tm, D_MODEL), BF16)],
        compiler_params=pltpu.CompilerParams(
            dimension_semantics=("arbitrary", "arbitrary"),
            vmem_limit_bytes=56 * MIB),
        name="mlp",
    )(x2d, g, wu, wd)


PLE_TM = 512


def _ple_kernel(x_ref, p_ref, g_ref, wg_ref, wp_ref, o_ref):
    x = x_ref[...]
    hn = _rms_scale(x, g_ref[...]).astype(BF16)
    gate = jnp.dot(hn, wg_ref[...], preferred_element_type=F32)
    pp = jnp.dot(p_ref[...].astype(BF16), wp_ref[...], preferred_element_type=F32)
    o_ref[...] = x + pp * jax.nn.sigmoid(gate)


def _ple(x2d, p2d, g, wg, wp):
    t = x2d.shape[0]
    tm = PLE_TM
    ple_dim = p2d.shape[1]
    return pl.pallas_call(
        _ple_kernel,
        grid=(t // tm,),
        in_specs=[
            pl.BlockSpec((tm, D_MODEL), lambda i: (i, 0)),
            pl.BlockSpec((tm, ple_dim), lambda i: (i, 0)),
            pl.BlockSpec((1, D_MODEL), lambda i: (0, 0)),
            _resident((D_MODEL, D_MODEL)),
            _resident((ple_dim, D_MODEL)),
        ],
        out_specs=pl.BlockSpec((tm, D_MODEL), lambda i: (i, 0)),
        out_shape=jax.ShapeDtypeStruct((t, D_MODEL), F32),
        compiler_params=pltpu.CompilerParams(
            dimension_semantics=("arbitrary",),
            vmem_limit_bytes=40 * MIB),
        name="ple",
    )(x2d, p2d, g, wg, wp)


def _rope_tables(seq):
    half = ROT_DIM // 2
    pos = jnp.arange(seq, dtype=F32)
    inv = ROPE_THETA ** (-jnp.arange(0, ROT_DIM, 2, dtype=F32) / ROT_DIM)
    ang = pos[:, None] * inv[None, :]
    cos, sin = jnp.cos(ang), jnp.sin(ang)
    rest = HEAD_DIM - ROT_DIM
    cos_t = jnp.concatenate([cos, cos, jnp.ones((seq, rest), F32)], axis=-1)
    zeros_h = jnp.zeros((seq, half), F32)
    zeros_r = jnp.zeros((seq, rest), F32)
    sa_t = jnp.concatenate([-sin, zeros_h, zeros_r], axis=-1)
    sb_t = jnp.concatenate([zeros_h, sin, zeros_r], axis=-1)
    return cos_t, sa_t, sb_t


def kernel(x, p, g_mix, w_in, qn_gain, kn_gain, w_branch_a, w_branch_b, w_out,
           g_mlp, w_up, w_down, g_ple, w_ple_gate, w_ple_proj):
    b, s, d = x.shape
    t = b * s
    depth = w_in.shape[0]
    cos_t, sa_t, sb_t = _rope_tables(s)
    x2d = x.reshape(t, d)
    for i in range(depth):
        proj2d, nat2d, ph2d = _inproj(x2d, g_mix[i][None, :], w_in[i].astype(BF16),
                                      qn_gain[i][None, :], kn_gain[i][None, :], cos_t, sa_t, sb_t, s)
        proj3 = proj2d.reshape(b, s, D_PROJ)

        o_near, l_near = _near(nat2d.reshape(b, s, (N_MIXA - 1) * W_HEADS), proj3)
        ya = _far(ph2d.reshape(b, s // FAR_DIL, N_MIXA * PH_SEG), o_near, l_near)
        yb2d = _stickbreak(proj3).reshape(t, W_HEADS)

        x2d = _mixout(x2d, ya, yb2d, proj2d, w_branch_a[i].astype(BF16),
                      w_branch_b[i].astype(BF16), w_out[i].astype(BF16))
        x2d = _mlp(x2d, g_mlp[i][None, :], w_up[i].astype(BF16), w_down[i].astype(BF16))
        x2d = _ple(x2d, p[i].reshape(t, -1), g_ple[i][None, :],
                   w_ple_gate[i].astype(BF16), w_ple_proj[i].astype(BF16))
    return x2d.reshape(b, s, d)
```

```python
import jax
import jax.numpy as jnp
from jax import lax
from jax.experimental import pallas as pl
from jax.experimental.pallas import tpu as pltpu

D_MODEL = 2048
HEAD_DIM = 128
N_HEADS = 8
W_HEADS = N_HEADS * HEAD_DIM
DIL_PATTERNS = ((128, 1), (512, 4), (2048, 16))
BLOCK = 128
ROT_DIM = HEAD_DIM // 4
ROPE_THETA = 500000.0
D_FF = 4 * D_MODEL
EPS = 1e-6
D_IN = 6 * W_HEADS + 2 * D_MODEL
LOG2E = 1.4426950408889634
Q_SCALE = HEAD_DIM ** -0.5 * LOG2E

COL_QA, COL_KA, COL_VA, COL_QB, COL_KB, COL_VB = 0, 1, 2, 3, 4, 5
N_COL_TILES = D_IN // W_HEADS
PROJ_SKIP = 2
D_PROJ = D_IN - PROJ_SKIP * W_HEADS
P_VA, P_QB, P_KB, P_VB = (c - PROJ_SKIP for c in (COL_VA, COL_QB, COL_KB, COL_VB))

F32 = jnp.float32
BF16 = jnp.bfloat16
NEG_INF = float("-inf")

MIB = 1024 * 1024


def _rms_scale(x, gain):
    ms = jnp.mean(x * x, axis=-1, keepdims=True)
    return x * lax.rsqrt(ms + EPS) * gain


INPROJ_TM = 512


FAR_DIL = DIL_PATTERNS[2][1]
PH_ROWS = INPROJ_TM // FAR_DIL
PH_HEAD = FAR_DIL * HEAD_DIM
PH_SEG = N_HEADS * PH_HEAD


ROW_CHUNK = 128
SUB_DIL = 4
ROLL_DN = HEAD_DIM - ROT_DIM // 2
ROLL_UP = ROT_DIM // 2


N_MIXA = 3
HEAD_GROUPS = ((0, 1, 2), (3, 4, 5), (6, 7))


def _inproj_kernel(x_ref, g_ref, w_ref, qn_ref, kn_ref, cos_ref, sa_ref, sb_ref,
                   o_ref, nat_ref, kt_ref, ph_ref, hn_ref, tab_ref, raw_ref, ys_ref, yq_ref):
    j = pl.program_id(1)
    tm = x_ref.shape[0]

    @pl.when(j == 0)
    def _():
        hn_ref[...] = _rms_scale(x_ref[...], g_ref[...]).astype(BF16)
        for n, (gain_ref, scale) in enumerate(((qn_ref, Q_SCALE), (kn_ref, 1.0))):
            g = jnp.broadcast_to(gain_ref[...] * scale, (tm, HEAD_DIM))
            tab_ref[3 * n] = cos_ref[...] * g
            tab_ref[3 * n + 1] = sa_ref[...] * pltpu.roll(g, ROLL_DN, 1)
            tab_ref[3 * n + 2] = sb_ref[...] * pltpu.roll(g, ROLL_UP, 1)

    def project():
        return jnp.dot(hn_ref[...], w_ref[...], preferred_element_type=F32)

    def finish_mixer_a(tile, head_group):
        tab = 3 * tile
        for h in head_group:
            sl = slice(h * HEAD_DIM, (h + 1) * HEAD_DIM)
            for c in range(tm // ROW_CHUNK):
                rows = slice(c * ROW_CHUNK, (c + 1) * ROW_CHUNK)
                a = raw_ref[tile, rows, sl]
                if tile < N_MIXA - 1:
                    rs = lax.rsqrt(jnp.mean(a * a, axis=-1, keepdims=True) + EPS)
                    a = (a * tab_ref[tab, rows, :]
                         + pltpu.roll(a, ROLL_DN, 1) * tab_ref[tab + 1, rows, :]
                         + pltpu.roll(a, ROLL_UP, 1) * tab_ref[tab + 2, rows, :]) * rs
                    if tile == 0:
                        nat_ref[rows, sl] = a.astype(BF16)
                    else:
                        kt_ref[h, :, rows] = a.T.astype(BF16)
                ys_ref[h, rows, :] = a
            for r_lo in range(SUB_DIL):
                yq_ref[h, r_lo] = ys_ref[h, pl.ds(r_lo, tm // SUB_DIL, stride=SUB_DIL), :]
                for r_hi in range(FAR_DIL // SUB_DIL):
                    r = r_lo + SUB_DIL * r_hi
                    lanes = slice(h * PH_HEAD + r * HEAD_DIM, h * PH_HEAD + (r + 1) * HEAD_DIM)
                    ph_ref[:, lanes] = yq_ref[h, r_lo, pl.ds(r_hi, PH_ROWS, stride=SUB_DIL), :].astype(BF16)

    @pl.when(j == 0)
    def _():
        raw_ref[0] = project()

    assert N_COL_TILES == 1 + N_MIXA * len(HEAD_GROUPS)
    for step in range(1, N_COL_TILES):
        @pl.when(j == step)
        def _(step=step):
            acc = project()
            if step < N_MIXA:
                raw_ref[step] = acc
            if step == COL_QB:
                acc = acc * Q_SCALE
            if step >= PROJ_SKIP:
                o_ref[...] = acc.astype(BF16)
            finish_mixer_a((step - 1) // len(HEAD_GROUPS), HEAD_GROUPS[(step - 1) % len(HEAD_GROUPS)])


def _inproj(x2d, g, w_bf16, qn, kn, cos_t, sa_t, sb_t, seq):
    t = x2d.shape[0]
    tm = INPROJ_TM
    nseq = seq // tm
    tab_spec = pl.BlockSpec((tm, HEAD_DIM), lambda i, j: (i % nseq, 0))
    vec_spec = pl.BlockSpec((1, HEAD_DIM), lambda i, j: (0, 0))
    return pl.pallas_call(
        _inproj_kernel,
        grid=(t // tm, N_COL_TILES),
        in_specs=[
            pl.BlockSpec((tm, D_MODEL), lambda i, j: (i, 0)),
            pl.BlockSpec((1, D_MODEL), lambda i, j: (0, 0)),
            pl.BlockSpec((D_MODEL, W_HEADS), lambda i, j: (0, j)),
            vec_spec, vec_spec, tab_spec, tab_spec, tab_spec,
        ],
        out_specs=[
            pl.BlockSpec((tm, W_HEADS), lambda i, j: (i, jnp.maximum(j - PROJ_SKIP, 0))),
            pl.BlockSpec((tm, W_HEADS), lambda i, j: (i, 0)),
            pl.BlockSpec((N_HEADS, HEAD_DIM, tm), lambda i, j: (0, 0, i)),
            pl.BlockSpec((PH_ROWS, PH_SEG), lambda i, j: (i, jnp.maximum(j - 1, 0) // len(HEAD_GROUPS))),
        ],
        out_shape=[
            jax.ShapeDtypeStruct((t, D_PROJ), BF16),
            jax.ShapeDtypeStruct((t, W_HEADS), BF16),
            jax.ShapeDtypeStruct((N_HEADS, HEAD_DIM, t), BF16),
            jax.ShapeDtypeStruct((t // FAR_DIL, N_MIXA * PH_SEG), BF16),
        ],
        scratch_shapes=[
            pltpu.VMEM((tm, D_MODEL), BF16),
            pltpu.VMEM((6, tm, HEAD_DIM), F32),
            pltpu.VMEM((N_MIXA, tm, W_HEADS), F32),
            pltpu.VMEM((N_HEADS, tm, HEAD_DIM), F32),
            pltpu.VMEM((N_HEADS, SUB_DIL, tm // SUB_DIL, HEAD_DIM), F32),
        ],
        compiler_params=pltpu.CompilerParams(
            dimension_semantics=("arbitrary", "arbitrary"),
            vmem_limit_bytes=48 * MIB),
        name="inproj",
    )(x2d, g, w_bf16, qn, kn, cos_t, sa_t, sb_t)


NEAR_QB = 512
NEAR_SUB = 256
NEAR_WIN = DIL_PATTERNS[1][0] + NEAR_SUB


def _near_kernel(q_ref, ktp_ref, kt_ref, vp_ref, v_ref, o_ref, l_ref, ktcat_ref, vcat_ref):
    first = pl.program_id(1) == 0
    ktcat_ref[:, :, :NEAR_QB] = ktp_ref[...]
    ktcat_ref[:, :, NEAR_QB:] = kt_ref[...]
    vcat_ref[:NEAR_QB, :] = vp_ref[...]
    vcat_ref[NEAR_QB:, :] = v_ref[...]

    qi = lax.broadcasted_iota(jnp.int32, (NEAR_SUB, NEAR_WIN), 0)
    col = lax.broadcasted_iota(jnp.int32, (NEAR_SUB, NEAR_WIN), 1)
    dist = qi + (NEAR_WIN - NEAR_SUB) - col
    (w1, _), (w4, d4) = DIL_PATTERNS[0], DIL_PATTERNS[1]
    assert d4 & (d4 - 1) == 0
    in1 = (dist >= 0) & (dist <= w1)
    in4 = (dist >= 0) & (dist <= w4) & ((dist & (d4 - 1)) == 0)
    bias = jnp.where(in1 & in4, 1.0, jnp.where(in1 | in4, 0.0, NEG_INF))

    for sb in range(NEAR_QB // NEAR_SUB):
        rows = slice(sb * NEAR_SUB, (sb + 1) * NEAR_SUB)
        win = slice(sb * NEAR_SUB, sb * NEAR_SUB + NEAR_WIN)
        bias_sb = jnp.where(first & (col < NEAR_QB - sb * NEAR_SUB), NEG_INF, bias)
        for h in range(N_HEADS):
            cols = slice(h * HEAD_DIM, (h + 1) * HEAD_DIM)
            q = q_ref[rows, cols]
            m = den = acc = None
            for c in reversed(range(NEAR_WIN // NEAR_SUB)):
                keys = slice(sb * NEAR_SUB + c * NEAR_SUB, sb * NEAR_SUB + (c + 1) * NEAR_SUB)
                lanes = slice(c * NEAR_SUB, (c + 1) * NEAR_SUB)
                s = jnp.dot(q, ktcat_ref[h, :, keys], preferred_element_type=F32) + bias_sb[:, lanes]
                m_c = jnp.max(s, axis=-1, keepdims=True)
                if m is None:
                    m = m_c
                    e = jnp.exp2(s - m)
                    den = jnp.sum(e, axis=-1, keepdims=True)
                    acc = jnp.dot(e.astype(BF16), vcat_ref[keys, cols], preferred_element_type=F32)
                else:
                    m_new = jnp.maximum(m, m_c)
                    alpha = jnp.exp2(m - m_new)
                    e = jnp.exp2(s - m_new)
                    den = den * alpha + jnp.sum(e, axis=-1, keepdims=True)
                    acc = acc * alpha + jnp.dot(e.astype(BF16), vcat_ref[keys, cols],
                                                preferred_element_type=F32)
                    m = m_new
            o_ref[h, rows, :] = acc / den
            l_ref[h, rows, :] = jnp.broadcast_to(m + jnp.log2(den), (NEAR_SUB, HEAD_DIM))


def _near(nat3, kt, proj3):
    b, s, _ = nat3.shape
    nq = s // NEAR_QB

    def cur(c):
        return lambda bi, qi: (bi, qi, c)

    def prev(c):
        return lambda bi, qi: (bi, jnp.maximum(qi - 1, 0), c)

    kt_blk = (N_HEADS, HEAD_DIM, NEAR_QB)
    blk = (None, NEAR_QB, W_HEADS)
    out_spec = pl.BlockSpec((None, N_HEADS, NEAR_QB, HEAD_DIM), lambda bi, qi: (bi, 0, qi, 0))
    return pl.pallas_call(
        _near_kernel,
        grid=(b, s // NEAR_QB),
        in_specs=[
            pl.BlockSpec(blk, cur(0)),
            pl.BlockSpec(kt_blk, lambda bi, qi: (0, 0, bi * nq + jnp.maximum(qi - 1, 0))),
            pl.BlockSpec(kt_blk, lambda bi, qi: (0, 0, bi * nq + qi)),
            pl.BlockSpec(blk, prev(P_VA)), pl.BlockSpec(blk, cur(P_VA)),
        ],
        out_specs=[out_spec, out_spec],
        out_shape=[jax.ShapeDtypeStruct((b, N_HEADS, s, HEAD_DIM), F32)] * 2,
        scratch_shapes=[pltpu.VMEM((N_HEADS, HEAD_DIM, 2 * NEAR_QB), BF16),
                        pltpu.VMEM((2 * NEAR_QB, W_HEADS), BF16)],
        compiler_params=pltpu.CompilerParams(
            dimension_semantics=("arbitrary", "arbitrary"),
            vmem_limit_bytes=40 * MIB),
        name="dilated_near",
    )(nat3, kt, kt, proj3, proj3)


FAR_ROWS = BLOCK * FAR_DIL


def _far_kernel(q_ref, kp_ref, k_ref, vp_ref, v_ref, on_ref, ln_ref, y_ref, kcat_ref, vcat_ref):
    first = pl.program_id(1) == 0
    kcat_ref[:BLOCK, :] = kp_ref[...]
    kcat_ref[BLOCK:, :] = k_ref[...]
    vcat_ref[:BLOCK, :] = vp_ref[...]
    vcat_ref[BLOCK:, :] = v_ref[...]
    qi = lax.broadcasted_iota(jnp.int32, (BLOCK, 2 * BLOCK), 0)
    col = lax.broadcasted_iota(jnp.int32, (BLOCK, 2 * BLOCK), 1)
    dist = qi + BLOCK - col
    bias = jnp.where((dist >= 0) & (dist <= BLOCK), 0.0, NEG_INF)
    bias = jnp.where(first & (col < BLOCK), NEG_INF, bias)
    dn = (((1,), (1,)), ((), ()))

    for r in range(FAR_DIL):
        lanes = slice(r * HEAD_DIM, (r + 1) * HEAD_DIM)
        tokens = pl.ds(r, BLOCK, stride=FAR_DIL)
        s = lax.dot_general(q_ref[:, lanes], kcat_ref[:, lanes], dn, preferred_element_type=F32) + bias
        m = jnp.max(s, axis=-1, keepdims=True)
        e = jnp.exp2(s - m)
        den = jnp.sum(e, axis=-1, keepdims=True)
        o = jnp.dot(e.astype(BF16), vcat_ref[:, lanes], preferred_element_type=F32) / den
        lse = m + jnp.log2(den)
        l_near = ln_ref[tokens, :]
        mm = jnp.maximum(l_near, lse)
        w_near = jnp.exp2(l_near - mm)
        w_far = jnp.exp2(lse - mm)
        y_ref[tokens, :] = (on_ref[tokens, :] * w_near + o * w_far) / (w_near + w_far)


def _far(ph3, o_near, l_near):
    b, m, _ = ph3.shape
    s = m * FAR_DIL

    def cur(seg):
        return lambda bi, mi, h: (bi, mi, seg * N_HEADS + h)

    def prev(seg):
        return lambda bi, mi, h: (bi, jnp.maximum(mi - 1, 0), seg * N_HEADS + h)

    blk = (None, BLOCK, PH_HEAD)
    tok_spec = pl.BlockSpec((None, None, FAR_ROWS, HEAD_DIM), lambda bi, mi, h: (bi, h, mi, 0))
    return pl.pallas_call(
        _far_kernel,
        grid=(b, m // BLOCK, N_HEADS),
        in_specs=[
            pl.BlockSpec(blk, cur(0)),
            pl.BlockSpec(blk, prev(1)), pl.BlockSpec(blk, cur(1)),
            pl.BlockSpec(blk, prev(2)), pl.BlockSpec(blk, cur(2)),
            tok_spec, tok_spec,
        ],
        out_specs=tok_spec,
        out_shape=jax.ShapeDtypeStruct((b, N_HEADS, s, HEAD_DIM), F32),
        scratch_shapes=[pltpu.VMEM((2 * BLOCK, PH_HEAD), BF16)] * 2,
        compiler_params=pltpu.CompilerParams(
            dimension_semantics=("arbitrary", "arbitrary", "arbitrary"),
            vmem_limit_bytes=32 * MIB),
        name="dilated_far",
    )(ph3, ph3, ph3, ph3, ph3, o_near, l_near)


SB_BLK = 256
SB_DONE = 160.0


def _sb_block(q, k, v, tot, later, causal):
    dn = (((1,), (1,)), ((), ()))
    z = lax.dot_general(q, k, dn, preferred_element_type=F32)
    sp = jnp.maximum(z, 0.0) + jnp.log2(1.0 + jnp.exp2(-jnp.abs(z)))
    if causal is not None:
        sp = jnp.where(causal, sp, 0.0)
    cs = jnp.dot(sp.astype(BF16), later, preferred_element_type=F32)
    arg = (z - sp) - cs - jnp.concatenate([tot, tot], axis=1)
    if causal is not None:
        arg = jnp.where(causal, arg, NEG_INF)
    pv = jnp.dot(jnp.exp2(arg).astype(BF16), v, preferred_element_type=F32)
    rs = jnp.sum(sp, axis=1, keepdims=True)
    return pv, jnp.broadcast_to(rs, (SB_BLK, HEAD_DIM))


def _sb_first_two(q, k2, v2, later, causal):
    dn = (((1,), (1,)), ((), ()))
    z = lax.dot_general(q, k2, dn, preferred_element_type=F32)
    sp = jnp.maximum(z, 0.0) + jnp.log2(1.0 + jnp.exp2(-jnp.abs(z)))
    sp_prev = sp[:, :SB_BLK]
    sp_diag = jnp.where(causal, sp[:, SB_BLK:], 0.0)
    cs_prev = jnp.dot(sp_prev.astype(BF16), later, preferred_element_type=F32)
    cs_diag = jnp.dot(sp_diag.astype(BF16), later, preferred_element_type=F32)
    rs_diag = jnp.broadcast_to(jnp.sum(sp_diag, axis=1, keepdims=True), (SB_BLK, HEAD_DIM))
    rs_prev = jnp.broadcast_to(jnp.sum(sp_prev, axis=1, keepdims=True), (SB_BLK, HEAD_DIM))
    arg_prev = (z[:, :SB_BLK] - sp_prev) - cs_prev - jnp.concatenate([rs_diag, rs_diag], axis=1)
    arg_diag = jnp.where(causal, (z[:, SB_BLK:] - sp_diag) - cs_diag, NEG_INF)
    a = jnp.exp2(jnp.concatenate([arg_prev, arg_diag], axis=1)).astype(BF16)
    pv = jnp.dot(a, v2, preferred_element_type=F32)
    return pv, rs_diag + rs_prev


def _stickbreak_kernel(q_ref, k_ref, v_ref, o_ref, acc_ref, tot_ref):
    qi = pl.program_id(1)
    row = lax.broadcasted_iota(jnp.int32, (SB_BLK, SB_BLK), 0)
    col = lax.broadcasted_iota(jnp.int32, (SB_BLK, SB_BLK), 1)
    later = (row > col).astype(BF16)
    causal = col < row
    heads = [slice(h * HEAD_DIM, (h + 1) * HEAD_DIM) for h in range(N_HEADS)]

    def key_rows(kb, nblk=1):
        return pl.ds(pl.multiple_of(kb * SB_BLK, SB_BLK), nblk * SB_BLK)

    @pl.when(qi == 0)
    def _():
        rows = key_rows(qi)
        zero = jnp.zeros((SB_BLK, HEAD_DIM), F32)
        for hs in heads:
            pv, rs = _sb_block(q_ref[:, hs], k_ref[rows, hs], v_ref[rows, hs], zero, later, causal)
            acc_ref[:, hs] = pv
            tot_ref[:, hs] = rs

    @pl.when(qi > 0)
    def _():
        rows = key_rows(qi - 1, 2)
        for hs in heads:
            pv, rs = _sb_first_two(q_ref[:, hs], k_ref[rows, hs], v_ref[rows, hs], later, causal)
            acc_ref[:, hs] = pv
            tot_ref[:, hs] = rs

    def earlier_block(kb):
        rows = key_rows(kb)
        for hs in heads:
            pv, rs = _sb_block(q_ref[:, hs], k_ref[rows, hs], v_ref[rows, hs],
                               tot_ref[:, hs], later, None)
            acc_ref[:, hs] += pv
            tot_ref[:, hs] += rs

    def cond(c):
        kb, consumed = c
        return (kb >= 0) & (consumed < SB_DONE)

    def body(c):
        kb, _ = c
        earlier_block(kb)
        return kb - 1, jnp.min(tot_ref[...])

    lax.while_loop(cond, body, (qi - 2, jnp.min(tot_ref[...])))
    o_ref[...] = acc_ref[...].astype(o_ref.dtype)


def _stickbreak(proj3):
    b, s, _ = proj3.shape
    return pl.pallas_call(
        _stickbreak_kernel,
        grid=(b, s // SB_BLK),
        in_specs=[
            pl.BlockSpec((None, SB_BLK, W_HEADS), lambda bi, qi: (bi, qi, P_QB)),
            pl.BlockSpec((None, s, W_HEADS), lambda bi, qi: (bi, 0, P_KB),
                         pipeline_mode=pl.Buffered(1)),
            pl.BlockSpec((None, s, W_HEADS), lambda bi, qi: (bi, 0, P_VB),
                         pipeline_mode=pl.Buffered(1)),
        ],
        out_specs=pl.BlockSpec((None, SB_BLK, W_HEADS), lambda bi, qi: (bi, qi, 0)),
        out_shape=jax.ShapeDtypeStruct((b, s, W_HEADS), BF16),
        scratch_shapes=[pltpu.VMEM((SB_BLK, W_HEADS), F32), pltpu.VMEM((SB_BLK, W_HEADS), F32)],
        compiler_params=pltpu.CompilerParams(
            dimension_semantics=("arbitrary", "arbitrary"),
            vmem_limit_bytes=48 * MIB),
        name="stickbreak",
    )(proj3, proj3, proj3)


MIX_TM = 512


def _mixout_kernel(x_ref, ya_ref, yb_ref, ga_ref, gb_ref, wa_ref, wb_ref, wo_ref, o_ref):
    ya = jnp.concatenate([ya_ref[h] for h in range(N_HEADS)], axis=1).astype(BF16)
    ta = jnp.dot(ya, wa_ref[...], preferred_element_type=F32)
    tb = jnp.dot(yb_ref[...], wb_ref[...], preferred_element_type=F32)
    merged = (jax.nn.sigmoid(ga_ref[...].astype(F32)) * ta
              + jax.nn.sigmoid(gb_ref[...].astype(F32)) * tb)
    o_ref[...] = x_ref[...] + jnp.dot(merged.astype(BF16), wo_ref[...], preferred_element_type=F32)


def _resident(shape):
    return pl.BlockSpec(shape, lambda *_: (0,) * len(shape), pipeline_mode=pl.Buffered(1))


def _mixout(x2d, ya4, yb2d, proj2d, wa, wb, wo):
    t = x2d.shape[0]
    tm = MIX_TM
    tiles_per_seq = ya4.shape[2] // tm
    ga_col = ((P_VB + 1) * W_HEADS) // D_MODEL
    assert ga_col * D_MODEL == (P_VB + 1) * W_HEADS
    gb_col = ga_col + 1
    return pl.pallas_call(
        _mixout_kernel,
        grid=(t // tm,),
        in_specs=[
            pl.BlockSpec((tm, D_MODEL), lambda i: (i, 0)),
            pl.BlockSpec((None, N_HEADS, tm, HEAD_DIM),
                         lambda i: (i // tiles_per_seq, 0, i % tiles_per_seq, 0)),
            pl.BlockSpec((tm, W_HEADS), lambda i: (i, 0)),
            pl.BlockSpec((tm, D_MODEL), lambda i: (i, ga_col)),
            pl.BlockSpec((tm, D_MODEL), lambda i: (i, gb_col)),
            _resident((W_HEADS, D_MODEL)),
            _resident((W_HEADS, D_MODEL)),
            _resident((D_MODEL, D_MODEL)),
        ],
        out_specs=pl.BlockSpec((tm, D_MODEL), lambda i: (i, 0)),
        out_shape=jax.ShapeDtypeStruct((t, D_MODEL), F32),
        compiler_params=pltpu.CompilerParams(
            dimension_semantics=("arbitrary",),
            vmem_limit_bytes=56 * MIB),
        name="mixout",
    )(x2d, ya4, yb2d, proj2d, proj2d, wa, wb, wo)


MLP_TM = 1024
MLP_TF = 512


def _mlp_kernel(x_ref, g_ref, wu_ref, wd_ref, o_ref, hn_ref):
    f = pl.program_id(1)

    def chunk(hn):
        u = jnp.dot(hn, wu_ref[...], preferred_element_type=F32)
        a = jnp.square(jnp.maximum(u, 0.0)).astype(BF16)
        return jnp.dot(a, wd_ref[...], preferred_element_type=F32)

    @pl.when(f == 0)
    def _():
        x = x_ref[...]
        hn = _rms_scale(x, g_ref[...]).astype(BF16)
        hn_ref[...] = hn
        o_ref[...] = x + chunk(hn)

    @pl.when(f > 0)
    def _():
        o_ref[...] += chunk(hn_ref[...])


def _mlp(x2d, g, wu, wd):
    t = x2d.shape[0]
    tm, tf = MLP_TM, MLP_TF
    return pl.pallas_call(
        _mlp_kernel,
        grid=(t // tm, D_FF // tf),
        in_specs=[
            pl.BlockSpec((tm, D_MODEL), lambda i, f: (i, 0)),
            pl.BlockSpec((1, D_MODEL), lambda i, f: (0, 0)),
            pl.BlockSpec((D_MODEL, tf), lambda i, f: (0, f)),
            pl.BlockSpec((tf, D_MODEL), lambda i, f: (f, 0)),
        ],
        out_specs=pl.BlockSpec((tm, D_MODEL), lambda i, f: (i, 0)),
        out_shape=jax.ShapeDtypeStruct((t, D_MODEL), F32),
        scratch_shapes=[pltpu.VMEM((tm, D_MODEL), BF16)],
        compiler_params=pltpu.CompilerParams(
            dimension_semantics=("arbitrary", "arbitrary"),
            vmem_limit_bytes=56 * MIB),
        name="mlp",
    )(x2d, g, wu, wd)


PLE_TM = 512


def _ple_kernel(x_ref, p_ref, g_ref, wg_ref, wp_ref, o_ref):
    x = x_ref[...]
    hn = _rms_scale(x, g_ref[...]).astype(BF16)
    gate = jnp.dot(hn, wg_ref[...], preferred_element_type=F32)
    pp = jnp.dot(p_ref[...].astype(BF16), wp_ref[...], preferred_element_type=F32)
    o_ref[...] = x + pp * jax.nn.sigmoid(gate)


def _ple(x2d, p2d, g, wg, wp):
    t = x2d.shape[0]
    tm = PLE_TM
    ple_dim = p2d.shape[1]
    return pl.pallas_call(
        _ple_kernel,
        grid=(t // tm,),
        in_specs=[
            pl.BlockSpec((tm, D_MODEL), lambda i: (i, 0)),
            pl.BlockSpec((tm, ple_dim), lambda i: (i, 0)),
            pl.BlockSpec((1, D_MODEL), lambda i: (0, 0)),
            _resident((D_MODEL, D_MODEL)),
            _resident((ple_dim, D_MODEL)),
        ],
        out_specs=pl.BlockSpec((tm, D_MODEL), lambda i: (i, 0)),
        out_shape=jax.ShapeDtypeStruct((t, D_MODEL), F32),
        compiler_params=pltpu.CompilerParams(
            dimension_semantics=("arbitrary",),
            vmem_limit_bytes=40 * MIB),
        name="ple",
    )(x2d, p2d, g, wg, wp)


def _rope_tables(seq):
    half = ROT_DIM // 2
    pos = jnp.arange(seq, dtype=F32)
    inv = ROPE_THETA ** (-jnp.arange(0, ROT_DIM, 2, dtype=F32) / ROT_DIM)
    ang = pos[:, None] * inv[None, :]
    cos, sin = jnp.cos(ang), jnp.sin(ang)
    rest = HEAD_DIM - ROT_DIM
    cos_t = jnp.concatenate([cos, cos, jnp.ones((seq, rest), F32)], axis=-1)
    zeros_h = jnp.zeros((seq, half), F32)
    zeros_r = jnp.zeros((seq, rest), F32)
    sa_t = jnp.concatenate([-sin, zeros_h, zeros_r], axis=-1)
    sb_t = jnp.concatenate([zeros_h, sin, zeros_r], axis=-1)
    return cos_t, sa_t, sb_t


def kernel(x, p, g_mix, w_in, qn_gain, kn_gain, w_branch_a, w_branch_b, w_out,
           g_mlp, w_up, w_down, g_ple, w_ple_gate, w_ple_proj):
    b, s, d = x.shape
    t = b * s
    depth = w_in.shape[0]
    cos_t, sa_t, sb_t = _rope_tables(s)
    x2d = x.reshape(t, d)
    for i in range(depth):
        proj2d, nat2d, kt, ph2d = _inproj(x2d, g_mix[i][None, :], w_in[i].astype(BF16),
                                      qn_gain[i][None, :], kn_gain[i][None, :], cos_t, sa_t, sb_t, s)
        proj3 = proj2d.reshape(b, s, D_PROJ)

        o_near, l_near = _near(nat2d.reshape(b, s, W_HEADS), kt, proj3)
        ya = _far(ph2d.reshape(b, s // FAR_DIL, N_MIXA * PH_SEG), o_near, l_near)
        yb2d = _stickbreak(proj3).reshape(t, W_HEADS)

        x2d = _mixout(x2d, ya, yb2d, proj2d, w_branch_a[i].astype(BF16),
                      w_branch_b[i].astype(BF16), w_out[i].astype(BF16))
        x2d = _mlp(x2d, g_mlp[i][None, :], w_up[i].astype(BF16), w_down[i].astype(BF16))
        x2d = _ple(x2d, p[i].reshape(t, -1), g_ple[i][None, :],
                   w_ple_gate[i].astype(BF16), w_ple_proj[i].astype(BF16))
    return x2d.reshape(b, s, d)
```

```python
import jax
import jax.numpy as jnp
from jax import lax
from jax.experimental import pallas as pl
from jax.experimental.pallas import tpu as pltpu

D_MODEL = 2048
HEAD_DIM = 128
N_HEADS = 8
W_HEADS = N_HEADS * HEAD_DIM
DIL_PATTERNS = ((128, 1), (512, 4), (2048, 16))
BLOCK = 128
ROT_DIM = HEAD_DIM // 4
ROPE_THETA = 500000.0
D_FF = 4 * D_MODEL
EPS = 1e-6
D_IN = 6 * W_HEADS + 2 * D_MODEL
LOG2E = 1.4426950408889634
Q_SCALE = HEAD_DIM ** -0.5 * LOG2E

COL_QA, COL_KA, COL_VA, COL_QB, COL_KB, COL_VB = 0, 1, 2, 3, 4, 5
N_COL_TILES = D_IN // W_HEADS
PROJ_SKIP = 2
D_PROJ = D_IN - PROJ_SKIP * W_HEADS
P_VA, P_QB, P_KB, P_VB = (c - PROJ_SKIP for c in (COL_VA, COL_QB, COL_KB, COL_VB))

F32 = jnp.float32
BF16 = jnp.bfloat16
NEG_INF = float("-inf")

MIB = 1024 * 1024


def _rms_scale(x, gain):
    ms = jnp.mean(x * x, axis=-1, keepdims=True)
    return x * lax.rsqrt(ms + EPS) * gain


INPROJ_TM = 512


FAR_DIL = DIL_PATTERNS[2][1]
PH_ROWS = INPROJ_TM // FAR_DIL
PH_HEAD = FAR_DIL * HEAD_DIM
PH_SEG = N_HEADS * PH_HEAD


ROW_CHUNK = 128
SUB_DIL = 4
ROLL_DN = HEAD_DIM - ROT_DIM // 2
ROLL_UP = ROT_DIM // 2


N_MIXA = 3
HEAD_GROUPS = ((0, 1, 2), (3, 4, 5), (6, 7))


def _inproj_kernel(x_ref, g_ref, w_ref, qn_ref, kn_ref, cos_ref, sa_ref, sb_ref,
                   o_ref, nat_ref, kt_ref, ph_ref, hn_ref, tab_ref, raw_ref, ys_ref, yq_ref):
    j = pl.program_id(1)
    tm = x_ref.shape[0]

    @pl.when(j == 0)
    def _():
        hn_ref[...] = _rms_scale(x_ref[...], g_ref[...]).astype(BF16)
        for n, (gain_ref, scale) in enumerate(((qn_ref, Q_SCALE), (kn_ref, 1.0))):
            g = jnp.broadcast_to(gain_ref[...] * scale, (tm, HEAD_DIM))
            tab_ref[3 * n] = cos_ref[...] * g
            tab_ref[3 * n + 1] = sa_ref[...] * pltpu.roll(g, ROLL_DN, 1)
            tab_ref[3 * n + 2] = sb_ref[...] * pltpu.roll(g, ROLL_UP, 1)

    def project():
        return jnp.dot(hn_ref[...], w_ref[...], preferred_element_type=F32)

    def finish_mixer_a(tile, head_group):
        tab = 3 * tile
        for h in head_group:
            sl = slice(h * HEAD_DIM, (h + 1) * HEAD_DIM)
            for c in range(tm // ROW_CHUNK):
                rows = slice(c * ROW_CHUNK, (c + 1) * ROW_CHUNK)
                a = raw_ref[tile, rows, sl]
                if tile < N_MIXA - 1:
                    rs = lax.rsqrt(jnp.mean(a * a, axis=-1, keepdims=True) + EPS)
                    a = (a * tab_ref[tab, rows, :]
                         + pltpu.roll(a, ROLL_DN, 1) * tab_ref[tab + 1, rows, :]
                         + pltpu.roll(a, ROLL_UP, 1) * tab_ref[tab + 2, rows, :]) * rs
                    if tile == 0:
                        nat_ref[rows, sl] = a.astype(BF16)
                    else:
                        kt_ref[h, :, rows] = a.T.astype(BF16)
                ys_ref[h, rows, :] = a
            for r_lo in range(SUB_DIL):
                yq_ref[h, r_lo] = ys_ref[h, pl.ds(r_lo, tm // SUB_DIL, stride=SUB_DIL), :]
                for r_hi in range(FAR_DIL // SUB_DIL):
                    r = r_lo + SUB_DIL * r_hi
                    lanes = slice(h * PH_HEAD + r * HEAD_DIM, h * PH_HEAD + (r + 1) * HEAD_DIM)
                    ph_ref[:, lanes] = yq_ref[h, r_lo, pl.ds(r_hi, PH_ROWS, stride=SUB_DIL), :].astype(BF16)

    @pl.when(j == 0)
    def _():
        raw_ref[0] = project()

    assert N_COL_TILES == 1 + N_MIXA * len(HEAD_GROUPS)
    for step in range(1, N_COL_TILES):
        @pl.when(j == step)
        def _(step=step):
            acc = project()
            if step < N_MIXA:
                raw_ref[step] = acc
            if step == COL_QB:
                acc = acc * Q_SCALE
            if step >= PROJ_SKIP:
                o_ref[...] = acc.astype(BF16)
            finish_mixer_a((step - 1) // len(HEAD_GROUPS), HEAD_GROUPS[(step - 1) % len(HEAD_GROUPS)])


def _inproj(x2d, g, w_bf16, qn, kn, cos_t, sa_t, sb_t, seq):
    t = x2d.shape[0]
    tm = INPROJ_TM
    nseq = seq // tm
    tab_spec = pl.BlockSpec((tm, HEAD_DIM), lambda i, j: (i % nseq, 0))
    vec_spec = pl.BlockSpec((1, HEAD_DIM), lambda i, j: (0, 0))
    return pl.pallas_call(
        _inproj_kernel,
        grid=(t // tm, N_COL_TILES),
        in_specs=[
            pl.BlockSpec((tm, D_MODEL), lambda i, j: (i, 0)),
            pl.BlockSpec((1, D_MODEL), lambda i, j: (0, 0)),
            pl.BlockSpec((D_MODEL, W_HEADS), lambda i, j: (0, j)),
            vec_spec, vec_spec, tab_spec, tab_spec, tab_spec,
        ],
        out_specs=[
            pl.BlockSpec((tm, W_HEADS), lambda i, j: (i, jnp.maximum(j - PROJ_SKIP, 0))),
            pl.BlockSpec((tm, W_HEADS), lambda i, j: (i, 0)),
            pl.BlockSpec((N_HEADS, HEAD_DIM, tm), lambda i, j: (0, 0, i)),
            pl.BlockSpec((PH_ROWS, PH_SEG), lambda i, j: (i, jnp.maximum(j - 1, 0) // len(HEAD_GROUPS))),
        ],
        out_shape=[
            jax.ShapeDtypeStruct((t, D_PROJ), BF16),
            jax.ShapeDtypeStruct((t, W_HEADS), BF16),
            jax.ShapeDtypeStruct((N_HEADS, HEAD_DIM, t), BF16),
            jax.ShapeDtypeStruct((t // FAR_DIL, N_MIXA * PH_SEG), BF16),
        ],
        scratch_shapes=[
            pltpu.VMEM((tm, D_MODEL), BF16),
            pltpu.VMEM((6, tm, HEAD_DIM), F32),
            pltpu.VMEM((N_MIXA, tm, W_HEADS), F32),
            pltpu.VMEM((N_HEADS, tm, HEAD_DIM), F32),
            pltpu.VMEM((N_HEADS, SUB_DIL, tm // SUB_DIL, HEAD_DIM), F32),
        ],
        compiler_params=pltpu.CompilerParams(
            dimension_semantics=("arbitrary", "arbitrary"),
            vmem_limit_bytes=48 * MIB),
        name="inproj",
    )(x2d, g, w_bf16, qn, kn, cos_t, sa_t, sb_t)


NEAR_QB = 512
NEAR_SUB = 256
NEAR_WIN = DIL_PATTERNS[1][0] + NEAR_SUB


def _near_kernel(q_ref, kt_ref, v_ref, o_ref, l_ref, ktcat_ref, vcat_ref):
    first = pl.program_id(1) == 0

    @pl.when(first)
    def _():
        ktcat_ref[:, :, :NEAR_QB] = jnp.zeros((N_HEADS, HEAD_DIM, NEAR_QB), BF16)
        vcat_ref[:NEAR_QB, :] = jnp.zeros((NEAR_QB, W_HEADS), BF16)

    @pl.when(jnp.logical_not(first))
    def _():
        ktcat_ref[:, :, :NEAR_QB] = ktcat_ref[:, :, NEAR_QB:]
        vcat_ref[:NEAR_QB, :] = vcat_ref[NEAR_QB:, :]

    ktcat_ref[:, :, NEAR_QB:] = kt_ref[...]
    vcat_ref[NEAR_QB:, :] = v_ref[...]

    qi = lax.broadcasted_iota(jnp.int32, (NEAR_SUB, NEAR_WIN), 0)
    col = lax.broadcasted_iota(jnp.int32, (NEAR_SUB, NEAR_WIN), 1)
    dist = qi + (NEAR_WIN - NEAR_SUB) - col
    (w1, _), (w4, d4) = DIL_PATTERNS[0], DIL_PATTERNS[1]
    assert d4 & (d4 - 1) == 0
    in1 = (dist >= 0) & (dist <= w1)
    in4 = (dist >= 0) & (dist <= w4) & ((dist & (d4 - 1)) == 0)
    bias = jnp.where(in1 & in4, 1.0, jnp.where(in1 | in4, 0.0, NEG_INF))

    for sb in range(NEAR_QB // NEAR_SUB):
        rows = slice(sb * NEAR_SUB, (sb + 1) * NEAR_SUB)
        win = slice(sb * NEAR_SUB, sb * NEAR_SUB + NEAR_WIN)
        bias_sb = jnp.where(first & (col < NEAR_QB - sb * NEAR_SUB), NEG_INF, bias)
        for h in range(N_HEADS):
            cols = slice(h * HEAD_DIM, (h + 1) * HEAD_DIM)
            q = q_ref[rows, cols]
            m = den = acc = None
            for c in reversed(range(NEAR_WIN // NEAR_SUB)):
                keys = slice(sb * NEAR_SUB + c * NEAR_SUB, sb * NEAR_SUB + (c + 1) * NEAR_SUB)
                lanes = slice(c * NEAR_SUB, (c + 1) * NEAR_SUB)
                s = jnp.dot(q, ktcat_ref[h, :, keys], preferred_element_type=F32) + bias_sb[:, lanes]
                m_c = jnp.max(s, axis=-1, keepdims=True)
                if m is None:
                    m = m_c
                    e = jnp.exp2(s - m)
                    den = jnp.sum(e, axis=-1, keepdims=True)
                    acc = jnp.dot(e.astype(BF16), vcat_ref[keys, cols], preferred_element_type=F32)
                else:
                    m_new = jnp.maximum(m, m_c)
                    alpha = jnp.exp2(m - m_new)
                    e = jnp.exp2(s - m_new)
                    den = den * alpha + jnp.sum(e, axis=-1, keepdims=True)
                    acc = acc * alpha + jnp.dot(e.astype(BF16), vcat_ref[keys, cols],
                                                preferred_element_type=F32)
                    m = m_new
            o_ref[h, rows, :] = acc / den
            l_ref[h, rows, :] = jnp.broadcast_to(m + jnp.log2(den), (NEAR_SUB, HEAD_DIM))


def _near(nat3, kt, proj3):
    b, s, _ = nat3.shape
    nq = s // NEAR_QB

    def cur(c):
        return lambda bi, qi: (bi, qi, c)

    kt_blk = (N_HEADS, HEAD_DIM, NEAR_QB)
    blk = (None, NEAR_QB, W_HEADS)
    out_spec = pl.BlockSpec((None, N_HEADS, NEAR_QB, HEAD_DIM), lambda bi, qi: (bi, 0, qi, 0))
    return pl.pallas_call(
        _near_kernel,
        grid=(b, s // NEAR_QB),
        in_specs=[
            pl.BlockSpec(blk, cur(0)),
            pl.BlockSpec(kt_blk, lambda bi, qi: (0, 0, bi * nq + qi)),
            pl.BlockSpec(blk, cur(P_VA)),
        ],
        out_specs=[out_spec, out_spec],
        out_shape=[jax.ShapeDtypeStruct((b, N_HEADS, s, HEAD_DIM), F32)] * 2,
        scratch_shapes=[pltpu.VMEM((N_HEADS, HEAD_DIM, 2 * NEAR_QB), BF16),
                        pltpu.VMEM((2 * NEAR_QB, W_HEADS), BF16)],
        compiler_params=pltpu.CompilerParams(
            dimension_semantics=("arbitrary", "arbitrary"),
            vmem_limit_bytes=40 * MIB),
        name="dilated_near",
    )(nat3, kt, proj3)


FAR_ROWS = BLOCK * FAR_DIL


def _far_kernel(q_ref, k_ref, v_ref, on_ref, ln_ref, y_ref, kcat_ref, vcat_ref):
    first = pl.program_id(2) == 0

    @pl.when(first)
    def _():
        kcat_ref[:BLOCK, :] = jnp.zeros((BLOCK, PH_HEAD), BF16)
        vcat_ref[:BLOCK, :] = jnp.zeros((BLOCK, PH_HEAD), BF16)

    @pl.when(jnp.logical_not(first))
    def _():
        kcat_ref[:BLOCK, :] = kcat_ref[BLOCK:, :]
        vcat_ref[:BLOCK, :] = vcat_ref[BLOCK:, :]

    kcat_ref[BLOCK:, :] = k_ref[...]
    vcat_ref[BLOCK:, :] = v_ref[...]
    qi = lax.broadcasted_iota(jnp.int32, (BLOCK, 2 * BLOCK), 0)
    col = lax.broadcasted_iota(jnp.int32, (BLOCK, 2 * BLOCK), 1)
    dist = qi + BLOCK - col
    bias = jnp.where((dist >= 0) & (dist <= BLOCK), 0.0, NEG_INF)
    bias = jnp.where(first & (col < BLOCK), NEG_INF, bias)
    dn = (((1,), (1,)), ((), ()))

    for r in range(FAR_DIL):
        lanes = slice(r * HEAD_DIM, (r + 1) * HEAD_DIM)
        tokens = pl.ds(r, BLOCK, stride=FAR_DIL)
        s = lax.dot_general(q_ref[:, lanes], kcat_ref[:, lanes], dn, preferred_element_type=F32) + bias
        m = jnp.max(s, axis=-1, keepdims=True)
        e = jnp.exp2(s - m)
        den = jnp.sum(e, axis=-1, keepdims=True)
        o = jnp.dot(e.astype(BF16), vcat_ref[:, lanes], preferred_element_type=F32) / den
        lse = m + jnp.log2(den)
        l_near = ln_ref[tokens, :]
        mm = jnp.maximum(l_near, lse)
        w_near = jnp.exp2(l_near - mm)
        w_far = jnp.exp2(lse - mm)
        y_ref[tokens, :] = (on_ref[tokens, :] * w_near + o * w_far) / (w_near + w_far)


def _far(ph3, o_near, l_near):
    b, m, _ = ph3.shape
    s = m * FAR_DIL

    def seg(n):
        return lambda bi, h, mi: (bi, mi, n * N_HEADS + h)

    blk = (None, BLOCK, PH_HEAD)
    tok_spec = pl.BlockSpec((None, None, FAR_ROWS, HEAD_DIM), lambda bi, h, mi: (bi, h, mi, 0))
    return pl.pallas_call(
        _far_kernel,
        grid=(b, N_HEADS, m // BLOCK),
        in_specs=[
            pl.BlockSpec(blk, seg(0)), pl.BlockSpec(blk, seg(1)), pl.BlockSpec(blk, seg(2)),
            tok_spec, tok_spec,
        ],
        out_specs=tok_spec,
        out_shape=jax.ShapeDtypeStruct((b, N_HEADS, s, HEAD_DIM), F32),
        scratch_shapes=[pltpu.VMEM((2 * BLOCK, PH_HEAD), BF16)] * 2,
        compiler_params=pltpu.CompilerParams(
            dimension_semantics=("arbitrary", "arbitrary", "arbitrary"),
            vmem_limit_bytes=32 * MIB),
        name="dilated_far",
    )(ph3, ph3, ph3, o_near, l_near)


SB_BLK = 256
SB_DONE = 160.0


def _sb_block(q, k, v, tot, later, causal):
    dn = (((1,), (1,)), ((), ()))
    z = lax.dot_general(q, k, dn, preferred_element_type=F32)
    sp = jnp.maximum(z, 0.0) + jnp.log2(1.0 + jnp.exp2(-jnp.abs(z)))
    if causal is not None:
        sp = jnp.where(causal, sp, 0.0)
    cs = jnp.dot(sp.astype(BF16), later, preferred_element_type=F32)
    arg = (z - sp) - cs - jnp.concatenate([tot, tot], axis=1)
    if causal is not None:
        arg = jnp.where(causal, arg, NEG_INF)
    pv = jnp.dot(jnp.exp2(arg).astype(BF16), v, preferred_element_type=F32)
    rs = jnp.sum(sp, axis=1, keepdims=True)
    return pv, jnp.broadcast_to(rs, (SB_BLK, HEAD_DIM))


def _sb_first_two(q, k2, v2, later, causal):
    dn = (((1,), (1,)), ((), ()))
    z = lax.dot_general(q, k2, dn, preferred_element_type=F32)
    sp = jnp.maximum(z, 0.0) + jnp.log2(1.0 + jnp.exp2(-jnp.abs(z)))
    sp_prev = sp[:, :SB_BLK]
    sp_diag = jnp.where(causal, sp[:, SB_BLK:], 0.0)
    cs_prev = jnp.dot(sp_prev.astype(BF16), later, preferred_element_type=F32)
    cs_diag = jnp.dot(sp_diag.astype(BF16), later, preferred_element_type=F32)
    rs_diag = jnp.broadcast_to(jnp.sum(sp_diag, axis=1, keepdims=True), (SB_BLK, HEAD_DIM))
    rs_prev = jnp.broadcast_to(jnp.sum(sp_prev, axis=1, keepdims=True), (SB_BLK, HEAD_DIM))
    arg_prev = (z[:, :SB_BLK] - sp_prev) - cs_prev - jnp.concatenate([rs_diag, rs_diag], axis=1)
    arg_diag = jnp.where(causal, (z[:, SB_BLK:] - sp_diag) - cs_diag, NEG_INF)
    a = jnp.exp2(jnp.concatenate([arg_prev, arg_diag], axis=1)).astype(BF16)
    pv = jnp.dot(a, v2, preferred_element_type=F32)
    return pv, rs_diag + rs_prev


def _stickbreak_kernel(q_ref, k_ref, v_ref, o_ref, acc_ref, tot_ref):
    qi = pl.program_id(1)
    row = lax.broadcasted_iota(jnp.int32, (SB_BLK, SB_BLK), 0)
    col = lax.broadcasted_iota(jnp.int32, (SB_BLK, SB_BLK), 1)
    later = (row > col).astype(BF16)
    causal = col < row
    heads = [slice(h * HEAD_DIM, (h + 1) * HEAD_DIM) for h in range(N_HEADS)]

    def key_rows(kb, nblk=1):
        return pl.ds(pl.multiple_of(kb * SB_BLK, SB_BLK), nblk * SB_BLK)

    @pl.when(qi == 0)
    def _():
        rows = key_rows(qi)
        zero = jnp.zeros((SB_BLK, HEAD_DIM), F32)
        for hs in heads:
            pv, rs = _sb_block(q_ref[:, hs], k_ref[rows, hs], v_ref[rows, hs], zero, later, causal)
            acc_ref[:, hs] = pv
            tot_ref[:, hs] = rs

    @pl.when(qi > 0)
    def _():
        rows = key_rows(qi - 1, 2)
        for hs in heads:
            pv, rs = _sb_first_two(q_ref[:, hs], k_ref[rows, hs], v_ref[rows, hs], later, causal)
            acc_ref[:, hs] = pv
            tot_ref[:, hs] = rs

    def earlier_block(kb):
        rows = key_rows(kb)
        for hs in heads:
            pv, rs = _sb_block(q_ref[:, hs], k_ref[rows, hs], v_ref[rows, hs],
                               tot_ref[:, hs], later, None)
            acc_ref[:, hs] += pv
            tot_ref[:, hs] += rs

    def cond(c):
        kb, consumed = c
        return (kb >= 0) & (consumed < SB_DONE)

    def body(c):
        kb, _ = c
        earlier_block(kb)
        return kb - 1, jnp.min(tot_ref[...])

    lax.while_loop(cond, body, (qi - 2, jnp.min(tot_ref[...])))
    o_ref[...] = acc_ref[...].astype(o_ref.dtype)


def _stickbreak(proj3):
    b, s, _ = proj3.shape
    return pl.pallas_call(
        _stickbreak_kernel,
        grid=(b, s // SB_BLK),
        in_specs=[
            pl.BlockSpec((None, SB_BLK, W_HEADS), lambda bi, qi: (bi, qi, P_QB)),
            pl.BlockSpec((None, s, W_HEADS), lambda bi, qi: (bi, 0, P_KB),
                         pipeline_mode=pl.Buffered(1)),
            pl.BlockSpec((None, s, W_HEADS), lambda bi, qi: (bi, 0, P_VB),
                         pipeline_mode=pl.Buffered(1)),
        ],
        out_specs=pl.BlockSpec((None, SB_BLK, W_HEADS), lambda bi, qi: (bi, qi, 0)),
        out_shape=jax.ShapeDtypeStruct((b, s, W_HEADS), BF16),
        scratch_shapes=[pltpu.VMEM((SB_BLK, W_HEADS), F32), pltpu.VMEM((SB_BLK, W_HEADS), F32)],
        compiler_params=pltpu.CompilerParams(
            dimension_semantics=("arbitrary", "arbitrary"),
            vmem_limit_bytes=48 * MIB),
        name="stickbreak",
    )(proj3, proj3, proj3)


MIX_TM = 512


def _mixout_kernel(x_ref, ya_ref, yb_ref, ga_ref, gb_ref, wa_ref, wb_ref, wo_ref, o_ref):
    ya = jnp.concatenate([ya_ref[h] for h in range(N_HEADS)], axis=1).astype(BF16)
    ta = jnp.dot(ya, wa_ref[...], preferred_element_type=F32)
    tb = jnp.dot(yb_ref[...], wb_ref[...], preferred_element_type=F32)
    merged = (jax.nn.sigmoid(ga_ref[...].astype(F32)) * ta
              + jax.nn.sigmoid(gb_ref[...].astype(F32)) * tb)
    o_ref[...] = x_ref[...] + jnp.dot(merged.astype(BF16), wo_ref[...], preferred_element_type=F32)


def _resident(shape):
    return pl.BlockSpec(shape, lambda *_: (0,) * len(shape), pipeline_mode=pl.Buffered(1))


def _mixout(x2d, ya4, yb2d, proj2d, wa, wb, wo):
    t = x2d.shape[0]
    tm = MIX_TM
    tiles_per_seq = ya4.shape[2] // tm
    ga_col = ((P_VB + 1) * W_HEADS) // D_MODEL
    assert ga_col * D_MODEL == (P_VB + 1) * W_HEADS
    gb_col = ga_col + 1
    return pl.pallas_call(
        _mixout_kernel,
        grid=(t // tm,),
        in_specs=[
            pl.BlockSpec((tm, D_MODEL), lambda i: (i, 0)),
            pl.BlockSpec((None, N_HEADS, tm, HEAD_DIM),
                         lambda i: (i // tiles_per_seq, 0, i % tiles_per_seq, 0)),
            pl.BlockSpec((tm, W_HEADS), lambda i: (i, 0)),
            pl.BlockSpec((tm, D_MODEL), lambda i: (i, ga_col)),
            pl.BlockSpec((tm, D_MODEL), lambda i: (i, gb_col)),
            _resident((W_HEADS, D_MODEL)),
            _resident((W_HEADS, D_MODEL)),
            _resident((D_MODEL, D_MODEL)),
        ],
        out_specs=pl.BlockSpec((tm, D_MODEL), lambda i: (i, 0)),
        out_shape=jax.ShapeDtypeStruct((t, D_MODEL), F32),
        compiler_params=pltpu.CompilerParams(
            dimension_semantics=("arbitrary",),
            vmem_limit_bytes=56 * MIB),
        name="mixout",
    )(x2d, ya4, yb2d, proj2d, proj2d, wa, wb, wo)


MLP_TM = 1024
MLP_TF = 512


def _mlp_kernel(x_ref, g_ref, wu_ref, wd_ref, o_ref, hn_ref):
    f = pl.program_id(1)

    def chunk(hn):
        u = jnp.dot(hn, wu_ref[...], preferred_element_type=F32)
        a = jnp.square(jnp.maximum(u, 0.0)).astype(BF16)
        return jnp.dot(a, wd_ref[...], preferred_element_type=F32)

    @pl.when(f == 0)
    def _():
        x = x_ref[...]
        hn = _rms_scale(x, g_ref[...]).astype(BF16)
        hn_ref[...] = hn
        o_ref[...] = x + chunk(hn)

    @pl.when(f > 0)
    def _():
        o_ref[...] += chunk(hn_ref[...])


def _mlp(x2d, g, wu, wd):
    t = x2d.shape[0]
    tm, tf = MLP_TM, MLP_TF
    return pl.pallas_call(
        _mlp_kernel,
        grid=(t // tm, D_FF // tf),
        in_specs=[
            pl.BlockSpec((tm, D_MODEL), lambda i, f: (i, 0)),
            pl.BlockSpec((1, D_MODEL), lambda i, f: (0, 0)),
            pl.BlockSpec((D_MODEL, tf), lambda i, f: (0, f)),
            pl.BlockSpec((tf, D_MODEL), lambda i, f: (f, 0)),
        ],
        out_specs=pl.BlockSpec((tm, D_MODEL), lambda i, f: (i, 0)),
        out_shape=jax.ShapeDtypeStruct((t, D_MODEL), F32),
        scratch_shapes=[pltpu.VMEM((tm, D_MODEL), BF16)],
        compiler_params=pltpu.CompilerParams(
            dimension_semantics=("arbitrary", "arbitrary"),
            vmem_limit_bytes=56 * MIB),
        name="mlp",
    )(x2d, g, wu, wd)


PLE_TM = 512


def _ple_kernel(x_ref, p_ref, g_ref, wg_ref, wp_ref, o_ref):
    x = x_ref[...]
    hn = _rms_scale(x, g_ref[...]).astype(BF16)
    gate = jnp.dot(hn, wg_ref[...], preferred_element_type=F32)
    pp = jnp.dot(p_ref[...].astype(BF16), wp_ref[...], preferred_element_type=F32)
    o_ref[...] = x + pp * jax.nn.sigmoid(gate)


def _ple(x2d, p2d, g, wg, wp):
    t = x2d.shape[0]
    tm = PLE_TM
    ple_dim = p2d.shape[1]
    return pl.pallas_call(
        _ple_kernel,
        grid=(t // tm,),
        in_specs=[
            pl.BlockSpec((tm, D_MODEL), lambda i: (i, 0)),
            pl.BlockSpec((tm, ple_dim), lambda i: (i, 0)),
            pl.BlockSpec((1, D_MODEL), lambda i: (0, 0)),
            _resident((D_MODEL, D_MODEL)),
            _resident((ple_dim, D_MODEL)),
        ],
        out_specs=pl.BlockSpec((tm, D_MODEL), lambda i: (i, 0)),
        out_shape=jax.ShapeDtypeStruct((t, D_MODEL), F32),
        compiler_params=pltpu.CompilerParams(
            dimension_semantics=("arbitrary",),
            vmem_limit_bytes=40 * MIB),
        name="ple",
    )(x2d, p2d, g, wg, wp)


def _rope_tables(seq):
    half = ROT_DIM // 2
    pos = jnp.arange(seq, dtype=F32)
    inv = ROPE_THETA ** (-jnp.arange(0, ROT_DIM, 2, dtype=F32) / ROT_DIM)
    ang = pos[:, None] * inv[None, :]
    cos, sin = jnp.cos(ang), jnp.sin(ang)
    rest = HEAD_DIM - ROT_DIM
    cos_t = jnp.concatenate([cos, cos, jnp.ones((seq, rest), F32)], axis=-1)
    zeros_h = jnp.zeros((seq, half), F32)
    zeros_r = jnp.zeros((seq, rest), F32)
    sa_t = jnp.concatenate([-sin, zeros_h, zeros_r], axis=-1)
    sb_t = jnp.concatenate([zeros_h, sin, zeros_r], axis=-1)
    return cos_t, sa_t, sb_t


def kernel(x, p, g_mix, w_in, qn_gain, kn_gain, w_branch_a, w_branch_b, w_out,
           g_mlp, w_up, w_down, g_ple, w_ple_gate, w_ple_proj):
    b, s, d = x.shape
    t = b * s
    depth = w_in.shape[0]
    cos_t, sa_t, sb_t = _rope_tables(s)
    x2d = x.reshape(t, d)
    for i in range(depth):
        proj2d, nat2d, kt, ph2d = _inproj(x2d, g_mix[i][None, :], w_in[i].astype(BF16),
                                      qn_gain[i][None, :], kn_gain[i][None, :], cos_t, sa_t, sb_t, s)
        proj3 = proj2d.reshape(b, s, D_PROJ)

        o_near, l_near = _near(nat2d.reshape(b, s, W_HEADS), kt, proj3)
        ya = _far(ph2d.reshape(b, s // FAR_DIL, N_MIXA * PH_SEG), o_near, l_near)
        yb2d = _stickbreak(proj3).reshape(t, W_HEADS)

        x2d = _mixout(x2d, ya, yb2d, proj2d, w_branch_a[i].astype(BF16),
                      w_branch_b[i].astype(BF16), w_out[i].astype(BF16))
        x2d = _mlp(x2d, g_mlp[i][None, :], w_up[i].astype(BF16), w_down[i].astype(BF16))
        x2d = _ple(x2d, p[i].reshape(t, -1), g_ple[i][None, :],
                   w_ple_gate[i].astype(BF16), w_ple_proj[i].astype(BF16))
    return x2d.reshape(b, s, d)
```

```python
import jax
import jax.numpy as jnp
from jax import lax
from jax.experimental import pallas as pl
from jax.experimental.pallas import tpu as pltpu

D_MODEL = 2048
HEAD_DIM = 128
N_HEADS = 8
W_HEADS = N_HEADS * HEAD_DIM
DIL_PATTERNS = ((128, 1), (512, 4), (2048, 16))
BLOCK = 128
ROT_DIM = HEAD_DIM // 4
ROPE_THETA = 500000.0
D_FF = 4 * D_MODEL
EPS = 1e-6
D_IN = 6 * W_HEADS + 2 * D_MODEL
LOG2E = 1.4426950408889634
Q_SCALE = HEAD_DIM ** -0.5 * LOG2E

COL_QA, COL_KA, COL_VA, COL_QB, COL_KB, COL_VB = 0, 1, 2, 3, 4, 5
N_COL_TILES = D_IN // W_HEADS
PROJ_SKIP = 2
D_PROJ = D_IN - PROJ_SKIP * W_HEADS
P_VA, P_QB, P_KB, P_VB = (c - PROJ_SKIP for c in (COL_VA, COL_QB, COL_KB, COL_VB))

F32 = jnp.float32
BF16 = jnp.bfloat16
NEG_INF = float("-inf")

MIB = 1024 * 1024


def _rms_scale(x, gain):
    ms = jnp.mean(x * x, axis=-1, keepdims=True)
    return x * lax.rsqrt(ms + EPS) * gain


INPROJ_TM = 512


FAR_DIL = DIL_PATTERNS[2][1]
PH_ROWS = INPROJ_TM // FAR_DIL
PH_HEAD = FAR_DIL * HEAD_DIM
PH_SEG = N_HEADS * PH_HEAD


ROW_CHUNK = 128
SUB_DIL = 4
ROLL_DN = HEAD_DIM - ROT_DIM // 2
ROLL_UP = ROT_DIM // 2


N_MIXA = 3
HEAD_GROUPS = ((0, 1, 2), (3, 4, 5), (6, 7))


def _inproj_kernel(x_ref, g_ref, w_ref, qn_ref, kn_ref, cos_ref, sa_ref, sb_ref,
                   o_ref, nat_ref, kt_ref, ph_ref, hn_ref, tab_ref, raw_ref, ys_ref, yq_ref):
    j = pl.program_id(1)
    tm = x_ref.shape[0]

    @pl.when(j == 0)
    def _():
        hn_ref[...] = _rms_scale(x_ref[...], g_ref[...]).astype(BF16)
        for n, (gain_ref, scale) in enumerate(((qn_ref, Q_SCALE), (kn_ref, 1.0))):
            g = jnp.broadcast_to(gain_ref[...] * scale, (tm, HEAD_DIM))
            tab_ref[3 * n] = cos_ref[...] * g
            tab_ref[3 * n + 1] = sa_ref[...] * pltpu.roll(g, ROLL_DN, 1)
            tab_ref[3 * n + 2] = sb_ref[...] * pltpu.roll(g, ROLL_UP, 1)

    def project():
        return jnp.dot(hn_ref[...], w_ref[...], preferred_element_type=F32)

    def finish_mixer_a(tile, head_group):
        tab = 3 * tile
        for h in head_group:
            sl = slice(h * HEAD_DIM, (h + 1) * HEAD_DIM)
            for c in range(tm // ROW_CHUNK):
                rows = slice(c * ROW_CHUNK, (c + 1) * ROW_CHUNK)
                a = raw_ref[tile, rows, sl]
                if tile < N_MIXA - 1:
                    rs = lax.rsqrt(jnp.mean(a * a, axis=-1, keepdims=True) + EPS)
                    a = (a * tab_ref[tab, rows, :]
                         + pltpu.roll(a, ROLL_DN, 1) * tab_ref[tab + 1, rows, :]
                         + pltpu.roll(a, ROLL_UP, 1) * tab_ref[tab + 2, rows, :]) * rs
                    if tile == 0:
                        nat_ref[rows, sl] = a.astype(BF16)
                    else:
                        kt_ref[h, :, rows] = a.T.astype(BF16)
                ys_ref[h, rows, :] = a
            for r_lo in range(SUB_DIL):
                yq_ref[h, r_lo] = ys_ref[h, pl.ds(r_lo, tm // SUB_DIL, stride=SUB_DIL), :]
                for r_hi in range(FAR_DIL // SUB_DIL):
                    r = r_lo + SUB_DIL * r_hi
                    lanes = slice(h * PH_HEAD + r * HEAD_DIM, h * PH_HEAD + (r + 1) * HEAD_DIM)
                    ph_ref[:, lanes] = yq_ref[h, r_lo, pl.ds(r_hi, PH_ROWS, stride=SUB_DIL), :].astype(BF16)

    @pl.when(j == 0)
    def _():
        raw_ref[0] = project()

    assert N_COL_TILES == 1 + N_MIXA * len(HEAD_GROUPS)
    for step in range(1, N_COL_TILES):
        @pl.when(j == step)
        def _(step=step):
            acc = project()
            if step < N_MIXA:
                raw_ref[step] = acc
            if step == COL_QB:
                acc = acc * Q_SCALE
            if step >= PROJ_SKIP:
                o_ref[...] = acc.astype(BF16)
            finish_mixer_a((step - 1) // len(HEAD_GROUPS), HEAD_GROUPS[(step - 1) % len(HEAD_GROUPS)])


def _inproj(x2d, g, w_bf16, qn, kn, cos_t, sa_t, sb_t, seq):
    t = x2d.shape[0]
    tm = INPROJ_TM
    nseq = seq // tm
    tab_spec = pl.BlockSpec((tm, HEAD_DIM), lambda i, j: (i % nseq, 0))
    vec_spec = pl.BlockSpec((1, HEAD_DIM), lambda i, j: (0, 0))
    return pl.pallas_call(
        _inproj_kernel,
        grid=(t // tm, N_COL_TILES),
        in_specs=[
            pl.BlockSpec((tm, D_MODEL), lambda i, j: (i, 0)),
            pl.BlockSpec((1, D_MODEL), lambda i, j: (0, 0)),
            pl.BlockSpec((D_MODEL, W_HEADS), lambda i, j: (0, j)),
            vec_spec, vec_spec, tab_spec, tab_spec, tab_spec,
        ],
        out_specs=[
            pl.BlockSpec((tm, W_HEADS), lambda i, j: (i, jnp.maximum(j - PROJ_SKIP, 0))),
            pl.BlockSpec((tm, W_HEADS), lambda i, j: (i, 0)),
            pl.BlockSpec((N_HEADS, HEAD_DIM, tm), lambda i, j: (0, 0, i)),
            pl.BlockSpec((PH_ROWS, PH_SEG), lambda i, j: (i, jnp.maximum(j - 1, 0) // len(HEAD_GROUPS))),
        ],
        out_shape=[
            jax.ShapeDtypeStruct((t, D_PROJ), BF16),
            jax.ShapeDtypeStruct((t, W_HEADS), BF16),
            jax.ShapeDtypeStruct((N_HEADS, HEAD_DIM, t), BF16),
            jax.ShapeDtypeStruct((t // FAR_DIL, N_MIXA * PH_SEG), BF16),
        ],
        scratch_shapes=[
            pltpu.VMEM((tm, D_MODEL), BF16),
            pltpu.VMEM((6, tm, HEAD_DIM), F32),
            pltpu.VMEM((N_MIXA, tm, W_HEADS), F32),
            pltpu.VMEM((N_HEADS, tm, HEAD_DIM), F32),
            pltpu.VMEM((N_HEADS, SUB_DIL, tm // SUB_DIL, HEAD_DIM), F32),
        ],
        compiler_params=pltpu.CompilerParams(
            dimension_semantics=("arbitrary", "arbitrary"),
            vmem_limit_bytes=48 * MIB),
        name="inproj",
    )(x2d, g, w_bf16, qn, kn, cos_t, sa_t, sb_t)


NEAR_QB = 512
NEAR_SUB = 256
NEAR_WIN = DIL_PATTERNS[1][0] + NEAR_SUB


def _near_kernel(q_ref, ktp_ref, kt_ref, vp_ref, v_ref, o_ref, l_ref, ktcat_ref, vcat_ref):
    first = pl.program_id(1) == 0
    ktcat_ref[:, :, :NEAR_QB] = ktp_ref[...]
    ktcat_ref[:, :, NEAR_QB:] = kt_ref[...]
    vcat_ref[:NEAR_QB, :] = vp_ref[...]
    vcat_ref[NEAR_QB:, :] = v_ref[...]

    qi = lax.broadcasted_iota(jnp.int32, (NEAR_SUB, NEAR_WIN), 0)
    col = lax.broadcasted_iota(jnp.int32, (NEAR_SUB, NEAR_WIN), 1)
    dist = qi + (NEAR_WIN - NEAR_SUB) - col
    (w1, _), (w4, d4) = DIL_PATTERNS[0], DIL_PATTERNS[1]
    assert d4 & (d4 - 1) == 0
    in1 = (dist >= 0) & (dist <= w1)
    in4 = (dist >= 0) & (dist <= w4) & ((dist & (d4 - 1)) == 0)
    bias = jnp.where(in1 & in4, 1.0, jnp.where(in1 | in4, 0.0, NEG_INF))

    for sb in range(NEAR_QB // NEAR_SUB):
        rows = slice(sb * NEAR_SUB, (sb + 1) * NEAR_SUB)
        win = slice(sb * NEAR_SUB, sb * NEAR_SUB + NEAR_WIN)
        bias_sb = jnp.where(first & (col < NEAR_QB - sb * NEAR_SUB), NEG_INF, bias)
        for h in range(N_HEADS):
            cols = slice(h * HEAD_DIM, (h + 1) * HEAD_DIM)
            q = q_ref[rows, cols]
            m = den = acc = None
            for c in reversed(range(NEAR_WIN // NEAR_SUB)):
                keys = slice(sb * NEAR_SUB + c * NEAR_SUB, sb * NEAR_SUB + (c + 1) * NEAR_SUB)
                lanes = slice(c * NEAR_SUB, (c + 1) * NEAR_SUB)
                s = jnp.dot(q, ktcat_ref[h, :, keys], preferred_element_type=F32) + bias_sb[:, lanes]
                m_c = jnp.max(s, axis=-1, keepdims=True)
                if m is None:
                    m = m_c
                    e = jnp.exp2(s - m)
                    den = jnp.sum(e, axis=-1, keepdims=True)
                    acc = jnp.dot(e.astype(BF16), vcat_ref[keys, cols], preferred_element_type=F32)
                else:
                    m_new = jnp.maximum(m, m_c)
                    alpha = jnp.exp2(m - m_new)
                    e = jnp.exp2(s - m_new)
                    den = den * alpha + jnp.sum(e, axis=-1, keepdims=True)
                    acc = acc * alpha + jnp.dot(e.astype(BF16), vcat_ref[keys, cols],
                                                preferred_element_type=F32)
                    m = m_new
            o_ref[h, rows, :] = acc / den
            l_ref[h, rows, :] = jnp.broadcast_to(m + jnp.log2(den), (NEAR_SUB, HEAD_DIM))


def _near(nat3, kt, proj3):
    b, s, _ = nat3.shape
    nq = s // NEAR_QB

    def cur(c):
        return lambda bi, qi: (bi, qi, c)

    def prev(c):
        return lambda bi, qi: (bi, jnp.maximum(qi - 1, 0), c)

    kt_blk = (N_HEADS, HEAD_DIM, NEAR_QB)
    blk = (None, NEAR_QB, W_HEADS)
    out_spec = pl.BlockSpec((None, N_HEADS, NEAR_QB, HEAD_DIM), lambda bi, qi: (bi, 0, qi, 0))
    return pl.pallas_call(
        _near_kernel,
        grid=(b, s // NEAR_QB),
        in_specs=[
            pl.BlockSpec(blk, cur(0)),
            pl.BlockSpec(kt_blk, lambda bi, qi: (0, 0, bi * nq + jnp.maximum(qi - 1, 0))),
            pl.BlockSpec(kt_blk, lambda bi, qi: (0, 0, bi * nq + qi)),
            pl.BlockSpec(blk, prev(P_VA)), pl.BlockSpec(blk, cur(P_VA)),
        ],
        out_specs=[out_spec, out_spec],
        out_shape=[jax.ShapeDtypeStruct((b, N_HEADS, s, HEAD_DIM), F32)] * 2,
        scratch_shapes=[pltpu.VMEM((N_HEADS, HEAD_DIM, 2 * NEAR_QB), BF16),
                        pltpu.VMEM((2 * NEAR_QB, W_HEADS), BF16)],
        compiler_params=pltpu.CompilerParams(
            dimension_semantics=("arbitrary", "arbitrary"),
            vmem_limit_bytes=40 * MIB),
        name="dilated_near",
    )(nat3, kt, kt, proj3, proj3)


FAR_ROWS = BLOCK * FAR_DIL


def _far_kernel(q_ref, k_ref, v_ref, on_ref, ln_ref, y_ref, kcat_ref, vcat_ref):
    first = pl.program_id(2) == 0

    @pl.when(first)
    def _():
        kcat_ref[:BLOCK, :] = jnp.zeros((BLOCK, PH_HEAD), BF16)
        vcat_ref[:BLOCK, :] = jnp.zeros((BLOCK, PH_HEAD), BF16)

    @pl.when(jnp.logical_not(first))
    def _():
        kcat_ref[:BLOCK, :] = kcat_ref[BLOCK:, :]
        vcat_ref[:BLOCK, :] = vcat_ref[BLOCK:, :]

    kcat_ref[BLOCK:, :] = k_ref[...]
    vcat_ref[BLOCK:, :] = v_ref[...]
    qi = lax.broadcasted_iota(jnp.int32, (BLOCK, 2 * BLOCK), 0)
    col = lax.broadcasted_iota(jnp.int32, (BLOCK, 2 * BLOCK), 1)
    dist = qi + BLOCK - col
    bias = jnp.where((dist >= 0) & (dist <= BLOCK), 0.0, NEG_INF)
    bias = jnp.where(first & (col < BLOCK), NEG_INF, bias)
    dn = (((1,), (1,)), ((), ()))

    for r in range(FAR_DIL):
        lanes = slice(r * HEAD_DIM, (r + 1) * HEAD_DIM)
        tokens = pl.ds(r, BLOCK, stride=FAR_DIL)
        s = lax.dot_general(q_ref[:, lanes], kcat_ref[:, lanes], dn, preferred_element_type=F32) + bias
        m = jnp.max(s, axis=-1, keepdims=True)
        e = jnp.exp2(s - m)
        den = jnp.sum(e, axis=-1, keepdims=True)
        o = jnp.dot(e.astype(BF16), vcat_ref[:, lanes], preferred_element_type=F32) / den
        lse = m + jnp.log2(den)
        l_near = ln_ref[tokens, :]
        mm = jnp.maximum(l_near, lse)
        w_near = jnp.exp2(l_near - mm)
        w_far = jnp.exp2(lse - mm)
        y_ref[tokens, :] = (on_ref[tokens, :] * w_near + o * w_far) / (w_near + w_far)


def _far(ph3, o_near, l_near):
    b, m, _ = ph3.shape
    s = m * FAR_DIL

    def seg(n):
        return lambda bi, h, mi: (bi, mi, n * N_HEADS + h)

    blk = (None, BLOCK, PH_HEAD)
    tok_spec = pl.BlockSpec((None, None, FAR_ROWS, HEAD_DIM), lambda bi, h, mi: (bi, h, mi, 0))
    return pl.pallas_call(
        _far_kernel,
        grid=(b, N_HEADS, m // BLOCK),
        in_specs=[
            pl.BlockSpec(blk, seg(0)), pl.BlockSpec(blk, seg(1)), pl.BlockSpec(blk, seg(2)),
            tok_spec, tok_spec,
        ],
        out_specs=tok_spec,
        out_shape=jax.ShapeDtypeStruct((b, N_HEADS, s, HEAD_DIM), F32),
        scratch_shapes=[pltpu.VMEM((2 * BLOCK, PH_HEAD), BF16)] * 2,
        compiler_params=pltpu.CompilerParams(
            dimension_semantics=("arbitrary", "arbitrary", "arbitrary"),
            vmem_limit_bytes=32 * MIB),
        name="dilated_far",
    )(ph3, ph3, ph3, o_near, l_near)


SB_BLK = 256
SB_DONE = 160.0


def _sb_block(q, k, v, tot, later, causal):
    dn = (((1,), (1,)), ((), ()))
    z = lax.dot_general(q, k, dn, preferred_element_type=F32)
    sp = jnp.maximum(z, 0.0) + jnp.log2(1.0 + jnp.exp2(-jnp.abs(z)))
    if causal is not None:
        sp = jnp.where(causal, sp, 0.0)
    cs = jnp.dot(sp.astype(BF16), later, preferred_element_type=F32)
    arg = (z - sp) - cs - jnp.concatenate([tot, tot], axis=1)
    if causal is not None:
        arg = jnp.where(causal, arg, NEG_INF)
    pv = jnp.dot(jnp.exp2(arg).astype(BF16), v, preferred_element_type=F32)
    rs = jnp.sum(sp, axis=1, keepdims=True)
    return pv, jnp.broadcast_to(rs, (SB_BLK, HEAD_DIM))


def _sb_first_two(q, k2, v2, later, causal):
    dn = (((1,), (1,)), ((), ()))
    z = lax.dot_general(q, k2, dn, preferred_element_type=F32)
    sp = jnp.maximum(z, 0.0) + jnp.log2(1.0 + jnp.exp2(-jnp.abs(z)))
    sp_prev = sp[:, :SB_BLK]
    sp_diag = jnp.where(causal, sp[:, SB_BLK:], 0.0)
    cs_prev = jnp.dot(sp_prev.astype(BF16), later, preferred_element_type=F32)
    cs_diag = jnp.dot(sp_diag.astype(BF16), later, preferred_element_type=F32)
    rs_diag = jnp.broadcast_to(jnp.sum(sp_diag, axis=1, keepdims=True), (SB_BLK, HEAD_DIM))
    rs_prev = jnp.broadcast_to(jnp.sum(sp_prev, axis=1, keepdims=True), (SB_BLK, HEAD_DIM))
    arg_prev = (z[:, :SB_BLK] - sp_prev) - cs_prev - jnp.concatenate([rs_diag, rs_diag], axis=1)
    arg_diag = jnp.where(causal, (z[:, SB_BLK:] - sp_diag) - cs_diag, NEG_INF)
    a = jnp.exp2(jnp.concatenate([arg_prev, arg_diag], axis=1)).astype(BF16)
    pv = jnp.dot(a, v2, preferred_element_type=F32)
    return pv, rs_diag + rs_prev


def _stickbreak_kernel(q_ref, k_ref, v_ref, o_ref, acc_ref, tot_ref):
    qi = pl.program_id(1)
    row = lax.broadcasted_iota(jnp.int32, (SB_BLK, SB_BLK), 0)
    col = lax.broadcasted_iota(jnp.int32, (SB_BLK, SB_BLK), 1)
    later = (row > col).astype(BF16)
    causal = col < row
    heads = [slice(h * HEAD_DIM, (h + 1) * HEAD_DIM) for h in range(N_HEADS)]

    def key_rows(kb, nblk=1):
        return pl.ds(pl.multiple_of(kb * SB_BLK, SB_BLK), nblk * SB_BLK)

    @pl.when(qi == 0)
    def _():
        rows = key_rows(qi)
        zero = jnp.zeros((SB_BLK, HEAD_DIM), F32)
        for hs in heads:
            pv, rs = _sb_block(q_ref[:, hs], k_ref[rows, hs], v_ref[rows, hs], zero, later, causal)
            acc_ref[:, hs] = pv
            tot_ref[:, hs] = rs

    @pl.when(qi > 0)
    def _():
        rows = key_rows(qi - 1, 2)
        for hs in heads:
            pv, rs = _sb_first_two(q_ref[:, hs], k_ref[rows, hs], v_ref[rows, hs], later, causal)
            acc_ref[:, hs] = pv
            tot_ref[:, hs] = rs

    def earlier_block(kb):
        rows = key_rows(kb)
        for hs in heads:
            pv, rs = _sb_block(q_ref[:, hs], k_ref[rows, hs], v_ref[rows, hs],
                               tot_ref[:, hs], later, None)
            acc_ref[:, hs] += pv
            tot_ref[:, hs] += rs

    def cond(c):
        kb, consumed = c
        return (kb >= 0) & (consumed < SB_DONE)

    def body(c):
        kb, _ = c
        earlier_block(kb)
        return kb - 1, jnp.min(tot_ref[...])

    lax.while_loop(cond, body, (qi - 2, jnp.min(tot_ref[...])))
    o_ref[...] = acc_ref[...].astype(o_ref.dtype)


def _stickbreak(proj3):
    b, s, _ = proj3.shape
    return pl.pallas_call(
        _stickbreak_kernel,
        grid=(b, s // SB_BLK),
        in_specs=[
            pl.BlockSpec((None, SB_BLK, W_HEADS), lambda bi, qi: (bi, qi, P_QB)),
            pl.BlockSpec((None, s, W_HEADS), lambda bi, qi: (bi, 0, P_KB),
                         pipeline_mode=pl.Buffered(1)),
            pl.BlockSpec((None, s, W_HEADS), lambda bi, qi: (bi, 0, P_VB),
                         pipeline_mode=pl.Buffered(1)),
        ],
        out_specs=pl.BlockSpec((None, SB_BLK, W_HEADS), lambda bi, qi: (bi, qi, 0)),
        out_shape=jax.ShapeDtypeStruct((b, s, W_HEADS), BF16),
        scratch_shapes=[pltpu.VMEM((SB_BLK, W_HEADS), F32), pltpu.VMEM((SB_BLK, W_HEADS), F32)],
        compiler_params=pltpu.CompilerParams(
            dimension_semantics=("arbitrary", "arbitrary"),
            vmem_limit_bytes=48 * MIB),
        name="stickbreak",
    )(proj3, proj3, proj3)


MIX_TM = 512


def _mixout_kernel(x_ref, ya_ref, yb_ref, ga_ref, gb_ref, wa_ref, wb_ref, wo_ref, o_ref):
    ya = jnp.concatenate([ya_ref[h] for h in range(N_HEADS)], axis=1).astype(BF16)
    ta = jnp.dot(ya, wa_ref[...], preferred_element_type=F32)
    tb = jnp.dot(yb_ref[...], wb_ref[...], preferred_element_type=F32)
    merged = (jax.nn.sigmoid(ga_ref[...].astype(F32)) * ta
              + jax.nn.sigmoid(gb_ref[...].astype(F32)) * tb)
    o_ref[...] = x_ref[...] + jnp.dot(merged.astype(BF16), wo_ref[...], preferred_element_type=F32)


def _resident(shape):
    return pl.BlockSpec(shape, lambda *_: (0,) * len(shape), pipeline_mode=pl.Buffered(1))


def _mixout(x2d, ya4, yb2d, proj2d, wa, wb, wo):
    t = x2d.shape[0]
    tm = MIX_TM
    tiles_per_seq = ya4.shape[2] // tm
    ga_col = ((P_VB + 1) * W_HEADS) // D_MODEL
    assert ga_col * D_MODEL == (P_VB + 1) * W_HEADS
    gb_col = ga_col + 1
    return pl.pallas_call(
        _mixout_kernel,
        grid=(t // tm,),
        in_specs=[
            pl.BlockSpec((tm, D_MODEL), lambda i: (i, 0)),
            pl.BlockSpec((None, N_HEADS, tm, HEAD_DIM),
                         lambda i: (i // tiles_per_seq, 0, i % tiles_per_seq, 0)),
            pl.BlockSpec((tm, W_HEADS), lambda i: (i, 0)),
            pl.BlockSpec((tm, D_MODEL), lambda i: (i, ga_col)),
            pl.BlockSpec((tm, D_MODEL), lambda i: (i, gb_col)),
            _resident((W_HEADS, D_MODEL)),
            _resident((W_HEADS, D_MODEL)),
            _resident((D_MODEL, D_MODEL)),
        ],
        out_specs=pl.BlockSpec((tm, D_MODEL), lambda i: (i, 0)),
        out_shape=jax.ShapeDtypeStruct((t, D_MODEL), F32),
        compiler_params=pltpu.CompilerParams(
            dimension_semantics=("arbitrary",),
            vmem_limit_bytes=56 * MIB),
        name="mixout",
    )(x2d, ya4, yb2d, proj2d, proj2d, wa, wb, wo)


MLP_TM = 1024
MLP_TF = 512


def _mlp_kernel(x_ref, g_ref, wu_ref, wd_ref, o_ref, hn_ref):
    f = pl.program_id(1)

    def chunk(hn):
        u = jnp.dot(hn, wu_ref[...], preferred_element_type=F32)
        a = jnp.square(jnp.maximum(u, 0.0)).astype(BF16)
        return jnp.dot(a, wd_ref[...], preferred_element_type=F32)

    @pl.when(f == 0)
    def _():
        x = x_ref[...]
        hn = _rms_scale(x, g_ref[...]).astype(BF16)
        hn_ref[...] = hn
        o_ref[...] = x + chunk(hn)

    @pl.when(f > 0)
    def _():
        o_ref[...] += chunk(hn_ref[...])


def _mlp(x2d, g, wu, wd):
    t = x2d.shape[0]
    tm, tf = MLP_TM, MLP_TF
    return pl.pallas_call(
        _mlp_kernel,
        grid=(t // tm, D_FF // tf),
        in_specs=[
            pl.BlockSpec((tm, D_MODEL), lambda i, f: (i, 0)),
            pl.BlockSpec((1, D_MODEL), lambda i, f: (0, 0)),
            pl.BlockSpec((D_MODEL, tf), lambda i, f: (0, f)),
            pl.BlockSpec((tf, D_MODEL), lambda i, f: (f, 0)),
        ],
        out_specs=pl.BlockSpec((tm, D_MODEL), lambda i, f: (i, 0)),
        out_shape=jax.ShapeDtypeStruct((t, D_MODEL), F32),
        scratch_shapes=[pltpu.VMEM((tm, D_MODEL), BF16)],
        compiler_params=pltpu.CompilerParams(
            dimension_semantics=("arbitrary", "arbitrary"),
            vmem_limit_bytes=56 * MIB),
        name="mlp",
    )(x2d, g, wu, wd)


PLE_TM = 512


def _ple_kernel(x_ref, p_ref, g_ref, wg_ref, wp_ref, o_ref):
    x = x_ref[...]
    hn = _rms_scale(x, g_ref[...]).astype(BF16)
    gate = jnp.dot(hn, wg_ref[...], preferred_element_type=F32)
    pp = jnp.dot(p_ref[...].astype(BF16), wp_ref[...], preferred_element_type=F32)
    o_ref[...] = x + pp * jax.nn.sigmoid(gate)


def _ple(x2d, p2d, g, wg, wp):
    t = x2d.shape[0]
    tm = PLE_TM
    ple_dim = p2d.shape[1]
    return pl.pallas_call(
        _ple_kernel,
        grid=(t // tm,),
        in_specs=[
            pl.BlockSpec((tm, D_MODEL), lambda i: (i, 0)),
            pl.BlockSpec((tm, ple_dim), lambda i: (i, 0)),
            pl.BlockSpec((1, D_MODEL), lambda i: (0, 0)),
            _resident((D_MODEL, D_MODEL)),
            _resident((ple_dim, D_MODEL)),
        ],
        out_specs=pl.BlockSpec((tm, D_MODEL), lambda i: (i, 0)),
        out_shape=jax.ShapeDtypeStruct((t, D_MODEL), F32),
        compiler_params=pltpu.CompilerParams(
            dimension_semantics=("arbitrary",),
            vmem_limit_bytes=40 * MIB),
        name="ple",
    )(x2d, p2d, g, wg, wp)


def _rope_tables(seq):
    half = ROT_DIM // 2
    pos = jnp.arange(seq, dtype=F32)
    inv = ROPE_THETA ** (-jnp.arange(0, ROT_DIM, 2, dtype=F32) / ROT_DIM)
    ang = pos[:, None] * inv[None, :]
    cos, sin = jnp.cos(ang), jnp.sin(ang)
    rest = HEAD_DIM - ROT_DIM
    cos_t = jnp.concatenate([cos, cos, jnp.ones((seq, rest), F32)], axis=-1)
    zeros_h = jnp.zeros((seq, half), F32)
    zeros_r = jnp.zeros((seq, rest), F32)
    sa_t = jnp.concatenate([-sin, zeros_h, zeros_r], axis=-1)
    sb_t = jnp.concatenate([zeros_h, sin, zeros_r], axis=-1)
    return cos_t, sa_t, sb_t


def kernel(x, p, g_mix, w_in, qn_gain, kn_gain, w_branch_a, w_branch_b, w_out,
           g_mlp, w_up, w_down, g_ple, w_ple_gate, w_ple_proj):
    b, s, d = x.shape
    t = b * s
    depth = w_in.shape[0]
    cos_t, sa_t, sb_t = _rope_tables(s)
    x2d = x.reshape(t, d)
    for i in range(depth):
        proj2d, nat2d, kt, ph2d = _inproj(x2d, g_mix[i][None, :], w_in[i].astype(BF16),
                                      qn_gain[i][None, :], kn_gain[i][None, :], cos_t, sa_t, sb_t, s)
        proj3 = proj2d.reshape(b, s, D_PROJ)

        o_near, l_near = _near(nat2d.reshape(b, s, W_HEADS), kt, proj3)
        ya = _far(ph2d.reshape(b, s // FAR_DIL, N_MIXA * PH_SEG), o_near, l_near)
        yb2d = _stickbreak(proj3).reshape(t, W_HEADS)

        x2d = _mixout(x2d, ya, yb2d, proj2d, w_branch_a[i].astype(BF16),
                      w_branch_b[i].astype(BF16), w_out[i].astype(BF16))
        x2d = _mlp(x2d, g_mlp[i][None, :], w_up[i].astype(BF16), w_down[i].astype(BF16))
        x2d = _ple(x2d, p[i].reshape(t, -1), g_ple[i][None, :],
                   w_ple_gate[i].astype(BF16), w_ple_proj[i].astype(BF16))
    return x2d.reshape(b, s, d)
```

```python
import jax
import jax.numpy as jnp
from jax import lax
from jax.experimental import pallas as pl
from jax.experimental.pallas import tpu as pltpu

D_MODEL = 2048
HEAD_DIM = 128
N_HEADS = 8
W_HEADS = N_HEADS * HEAD_DIM
DIL_PATTERNS = ((128, 1), (512, 4), (2048, 16))
BLOCK = 128
ROT_DIM = HEAD_DIM // 4
ROPE_THETA = 500000.0
D_FF = 4 * D_MODEL
EPS = 1e-6
D_IN = 6 * W_HEADS + 2 * D_MODEL
LOG2E = 1.4426950408889634
Q_SCALE = HEAD_DIM ** -0.5 * LOG2E

COL_QA, COL_KA, COL_VA, COL_QB, COL_KB, COL_VB = 0, 1, 2, 3, 4, 5
N_COL_TILES = D_IN // W_HEADS
PROJ_SKIP = 2
D_PROJ = D_IN - PROJ_SKIP * W_HEADS
P_VA, P_QB, P_KB, P_VB = (c - PROJ_SKIP for c in (COL_VA, COL_QB, COL_KB, COL_VB))

F32 = jnp.float32
BF16 = jnp.bfloat16
NEG_INF = float("-inf")

MIB = 1024 * 1024


def _rms_scale(x, gain):
    ms = jnp.mean(x * x, axis=-1, keepdims=True)
    return x * lax.rsqrt(ms + EPS) * gain


INPROJ_TM = 512


FAR_DIL = DIL_PATTERNS[2][1]
PH_ROWS = INPROJ_TM // FAR_DIL
PH_HEAD = FAR_DIL * HEAD_DIM
PH_SEG = N_HEADS * PH_HEAD


ROW_CHUNK = 128
SUB_DIL = 4
ROLL_DN = HEAD_DIM - ROT_DIM // 2
ROLL_UP = ROT_DIM // 2


N_MIXA = 3
HEAD_GROUPS = ((0, 1, 2), (3, 4, 5), (6, 7))


def _inproj_kernel(x_ref, g_ref, w_ref, qn_ref, kn_ref, cos_ref, sin_ref,
                   o_ref, nat_ref, kt_ref, ph_ref, hn_ref, tab_ref, raw_ref, ys_ref, yq_ref):
    j = pl.program_id(1)
    tm = x_ref.shape[0]

    @pl.when(j == 0)
    def _():
        hn_ref[...] = _rms_scale(x_ref[...], g_ref[...]).astype(BF16)
        lane = lax.broadcasted_iota(jnp.int32, (tm, HEAD_DIM), 1)
        sin_dn = jnp.where(lane < ROT_DIM // 2, sin_ref[...], 0.0)
        sin_up = jnp.where(lane >= ROT_DIM // 2, sin_ref[...], 0.0)
        for n, (gain_ref, scale) in enumerate(((qn_ref, Q_SCALE), (kn_ref, 1.0))):
            g = jnp.broadcast_to(gain_ref[...] * scale, (tm, HEAD_DIM))
            tab_ref[3 * n] = cos_ref[...] * g
            tab_ref[3 * n + 1] = sin_dn * pltpu.roll(g, ROLL_DN, 1)
            tab_ref[3 * n + 2] = sin_up * pltpu.roll(g, ROLL_UP, 1)

    def project():
        return jnp.dot(hn_ref[...], w_ref[...], preferred_element_type=F32)

    def finish_mixer_a(tile, head_group):
        tab = 3 * tile
        for h in head_group:
            sl = slice(h * HEAD_DIM, (h + 1) * HEAD_DIM)
            for c in range(tm // ROW_CHUNK):
                rows = slice(c * ROW_CHUNK, (c + 1) * ROW_CHUNK)
                a = raw_ref[tile, rows, sl]
                if tile < N_MIXA - 1:
                    rs = lax.rsqrt(jnp.mean(a * a, axis=-1, keepdims=True) + EPS)
                    a = (a * tab_ref[tab, rows, :]
                         + pltpu.roll(a, ROLL_DN, 1) * tab_ref[tab + 1, rows, :]
                         + pltpu.roll(a, ROLL_UP, 1) * tab_ref[tab + 2, rows, :]) * rs
                    if tile == 0:
                        nat_ref[rows, sl] = a.astype(BF16)
                    else:
                        kt_ref[h, :, rows] = a.T.astype(BF16)
                ys_ref[h, rows, :] = a
            for r_lo in range(SUB_DIL):
                yq_ref[h, r_lo] = ys_ref[h, pl.ds(r_lo, tm // SUB_DIL, stride=SUB_DIL), :]
                for r_hi in range(FAR_DIL // SUB_DIL):
                    r = r_lo + SUB_DIL * r_hi
                    lanes = slice(h * PH_HEAD + r * HEAD_DIM, h * PH_HEAD + (r + 1) * HEAD_DIM)
                    ph_ref[:, lanes] = yq_ref[h, r_lo, pl.ds(r_hi, PH_ROWS, stride=SUB_DIL), :].astype(BF16)

    @pl.when(j == 0)
    def _():
        raw_ref[0] = project()

    assert N_COL_TILES == 1 + N_MIXA * len(HEAD_GROUPS)
    for step in range(1, N_COL_TILES):
        @pl.when(j == step)
        def _(step=step):
            acc = project()
            if step < N_MIXA:
                raw_ref[step] = acc
            if step == COL_QB:
                acc = acc * Q_SCALE
            if step >= PROJ_SKIP:
                o_ref[...] = acc.astype(BF16)
            finish_mixer_a((step - 1) // len(HEAD_GROUPS), HEAD_GROUPS[(step - 1) % len(HEAD_GROUPS)])


def _inproj(x2d, g, w_bf16, qn, kn, cos_t, sin_t, seq):
    t = x2d.shape[0]
    tm = INPROJ_TM
    nseq = seq // tm
    tab_spec = pl.BlockSpec((tm, HEAD_DIM), lambda i, j: (i % nseq, 0))
    vec_spec = pl.BlockSpec((1, HEAD_DIM), lambda i, j: (0, 0))
    return pl.pallas_call(
        _inproj_kernel,
        grid=(t // tm, N_COL_TILES),
        in_specs=[
            pl.BlockSpec((tm, D_MODEL), lambda i, j: (i, 0)),
            pl.BlockSpec((1, D_MODEL), lambda i, j: (0, 0)),
            pl.BlockSpec((D_MODEL, W_HEADS), lambda i, j: (0, j)),
            vec_spec, vec_spec, tab_spec, tab_spec,
        ],
        out_specs=[
            pl.BlockSpec((tm, W_HEADS), lambda i, j: (i, jnp.maximum(j - PROJ_SKIP, 0))),
            pl.BlockSpec((tm, W_HEADS), lambda i, j: (i, 0)),
            pl.BlockSpec((N_HEADS, HEAD_DIM, tm), lambda i, j: (0, 0, i)),
            pl.BlockSpec((PH_ROWS, PH_SEG), lambda i, j: (i, jnp.maximum(j - 1, 0) // len(HEAD_GROUPS))),
        ],
        out_shape=[
            jax.ShapeDtypeStruct((t, D_PROJ), BF16),
            jax.ShapeDtypeStruct((t, W_HEADS), BF16),
            jax.ShapeDtypeStruct((N_HEADS, HEAD_DIM, t), BF16),
            jax.ShapeDtypeStruct((t // FAR_DIL, N_MIXA * PH_SEG), BF16),
        ],
        scratch_shapes=[
            pltpu.VMEM((tm, D_MODEL), BF16),
            pltpu.VMEM((6, tm, HEAD_DIM), F32),
            pltpu.VMEM((N_MIXA, tm, W_HEADS), F32),
            pltpu.VMEM((N_HEADS, tm, HEAD_DIM), F32),
            pltpu.VMEM((N_HEADS, SUB_DIL, tm // SUB_DIL, HEAD_DIM), F32),
        ],
        compiler_params=pltpu.CompilerParams(
            dimension_semantics=("arbitrary", "arbitrary"),
            vmem_limit_bytes=48 * MIB),
        name="inproj",
    )(x2d, g, w_bf16, qn, kn, cos_t, sin_t)


NEAR_QB = 512
NEAR_SUB = 256
NEAR_WIN = DIL_PATTERNS[1][0] + NEAR_SUB


def _near_kernel(q_ref, ktp_ref, kt_ref, vp_ref, v_ref, o_ref, l_ref, ktcat_ref, vcat_ref):
    first = pl.program_id(1) == 0
    ktcat_ref[:, :, :NEAR_QB] = ktp_ref[...]
    ktcat_ref[:, :, NEAR_QB:] = kt_ref[...]
    vcat_ref[:NEAR_QB, :] = vp_ref[...]
    vcat_ref[NEAR_QB:, :] = v_ref[...]

    qi = lax.broadcasted_iota(jnp.int32, (NEAR_SUB, NEAR_WIN), 0)
    col = lax.broadcasted_iota(jnp.int32, (NEAR_SUB, NEAR_WIN), 1)
    dist = qi + (NEAR_WIN - NEAR_SUB) - col
    (w1, _), (w4, d4) = DIL_PATTERNS[0], DIL_PATTERNS[1]
    assert d4 & (d4 - 1) == 0
    in1 = (dist >= 0) & (dist <= w1)
    in4 = (dist >= 0) & (dist <= w4) & ((dist & (d4 - 1)) == 0)
    bias = jnp.where(in1 & in4, 1.0, jnp.where(in1 | in4, 0.0, NEG_INF))

    for sb in range(NEAR_QB // NEAR_SUB):
        rows = slice(sb * NEAR_SUB, (sb + 1) * NEAR_SUB)
        win = slice(sb * NEAR_SUB, sb * NEAR_SUB + NEAR_WIN)
        bias_sb = jnp.where(first & (col < NEAR_QB - sb * NEAR_SUB), NEG_INF, bias)
        for h in range(N_HEADS):
            cols = slice(h * HEAD_DIM, (h + 1) * HEAD_DIM)
            q = q_ref[rows, cols]
            m = den = acc = None
            for c in reversed(range(NEAR_WIN // NEAR_SUB)):
                keys = slice(sb * NEAR_SUB + c * NEAR_SUB, sb * NEAR_SUB + (c + 1) * NEAR_SUB)
                lanes = slice(c * NEAR_SUB, (c + 1) * NEAR_SUB)
                s = jnp.dot(q, ktcat_ref[h, :, keys], preferred_element_type=F32) + bias_sb[:, lanes]
                m_c = jnp.max(s, axis=-1, keepdims=True)
                if m is None:
                    m = m_c
                    e = jnp.exp2(s - m)
                    den = jnp.sum(e, axis=-1, keepdims=True)
                    acc = jnp.dot(e.astype(BF16), vcat_ref[keys, cols], preferred_element_type=F32)
                else:
                    m_new = jnp.maximum(m, m_c)
                    alpha = jnp.exp2(m - m_new)
                    e = jnp.exp2(s - m_new)
                    den = den * alpha + jnp.sum(e, axis=-1, keepdims=True)
                    acc = acc * alpha + jnp.dot(e.astype(BF16), vcat_ref[keys, cols],
                                                preferred_element_type=F32)
                    m = m_new
            o_ref[h, rows, :] = acc / den
            l_ref[h, rows, :] = jnp.broadcast_to(m + jnp.log2(den), (NEAR_SUB, HEAD_DIM))


def _near(nat3, kt, proj3):
    b, s, _ = nat3.shape
    nq = s // NEAR_QB

    def cur(c):
        return lambda bi, qi: (bi, qi, c)

    def prev(c):
        return lambda bi, qi: (bi, jnp.maximum(qi - 1, 0), c)

    kt_blk = (N_HEADS, HEAD_DIM, NEAR_QB)
    blk = (None, NEAR_QB, W_HEADS)
    out_spec = pl.BlockSpec((None, N_HEADS, NEAR_QB, HEAD_DIM), lambda bi, qi: (bi, 0, qi, 0))
    return pl.pallas_call(
        _near_kernel,
        grid=(b, s // NEAR_QB),
        in_specs=[
            pl.BlockSpec(blk, cur(0)),
            pl.BlockSpec(kt_blk, lambda bi, qi: (0, 0, bi * nq + jnp.maximum(qi - 1, 0))),
            pl.BlockSpec(kt_blk, lambda bi, qi: (0, 0, bi * nq + qi)),
            pl.BlockSpec(blk, prev(P_VA)), pl.BlockSpec(blk, cur(P_VA)),
        ],
        out_specs=[out_spec, out_spec],
        out_shape=[jax.ShapeDtypeStruct((b, N_HEADS, s, HEAD_DIM), F32)] * 2,
        scratch_shapes=[pltpu.VMEM((N_HEADS, HEAD_DIM, 2 * NEAR_QB), BF16),
                        pltpu.VMEM((2 * NEAR_QB, W_HEADS), BF16)],
        compiler_params=pltpu.CompilerParams(
            dimension_semantics=("arbitrary", "arbitrary"),
            vmem_limit_bytes=40 * MIB),
        name="dilated_near",
    )(nat3, kt, kt, proj3, proj3)


FAR_ROWS = BLOCK * FAR_DIL


def _far_kernel(q_ref, k_ref, v_ref, on_ref, ln_ref, y_ref, kcat_ref, vcat_ref):
    first = pl.program_id(2) == 0

    @pl.when(first)
    def _():
        kcat_ref[:BLOCK, :] = jnp.zeros((BLOCK, PH_HEAD), BF16)
        vcat_ref[:BLOCK, :] = jnp.zeros((BLOCK, PH_HEAD), BF16)

    @pl.when(jnp.logical_not(first))
    def _():
        kcat_ref[:BLOCK, :] = kcat_ref[BLOCK:, :]
        vcat_ref[:BLOCK, :] = vcat_ref[BLOCK:, :]

    kcat_ref[BLOCK:, :] = k_ref[...]
    vcat_ref[BLOCK:, :] = v_ref[...]
    qi = lax.broadcasted_iota(jnp.int32, (BLOCK, 2 * BLOCK), 0)
    col = lax.broadcasted_iota(jnp.int32, (BLOCK, 2 * BLOCK), 1)
    dist = qi + BLOCK - col
    bias = jnp.where((dist >= 0) & (dist <= BLOCK), 0.0, NEG_INF)
    bias = jnp.where(first & (col < BLOCK), NEG_INF, bias)
    dn = (((1,), (1,)), ((), ()))

    for r in range(FAR_DIL):
        lanes = slice(r * HEAD_DIM, (r + 1) * HEAD_DIM)
        tokens = pl.ds(r, BLOCK, stride=FAR_DIL)
        s = lax.dot_general(q_ref[:, lanes], kcat_ref[:, lanes], dn, preferred_element_type=F32) + bias
        m = jnp.max(s, axis=-1, keepdims=True)
        e = jnp.exp2(s - m)
        den = jnp.sum(e, axis=-1, keepdims=True)
        o = jnp.dot(e.astype(BF16), vcat_ref[:, lanes], preferred_element_type=F32) / den
        lse = m + jnp.log2(den)
        l_near = ln_ref[tokens, :]
        mm = jnp.maximum(l_near, lse)
        w_near = jnp.exp2(l_near - mm)
        w_far = jnp.exp2(lse - mm)
        y_ref[tokens, :] = (on_ref[tokens, :] * w_near + o * w_far) / (w_near + w_far)


def _far(ph3, o_near, l_near):
    b, m, _ = ph3.shape
    s = m * FAR_DIL

    def seg(n):
        return lambda bi, h, mi: (bi, mi, n * N_HEADS + h)

    blk = (None, BLOCK, PH_HEAD)
    tok_spec = pl.BlockSpec((None, None, FAR_ROWS, HEAD_DIM), lambda bi, h, mi: (bi, h, mi, 0))
    return pl.pallas_call(
        _far_kernel,
        grid=(b, N_HEADS, m // BLOCK),
        in_specs=[
            pl.BlockSpec(blk, seg(0)), pl.BlockSpec(blk, seg(1)), pl.BlockSpec(blk, seg(2)),
            tok_spec, tok_spec,
        ],
        out_specs=tok_spec,
        out_shape=jax.ShapeDtypeStruct((b, N_HEADS, s, HEAD_DIM), F32),
        scratch_shapes=[pltpu.VMEM((2 * BLOCK, PH_HEAD), BF16)] * 2,
        compiler_params=pltpu.CompilerParams(
            dimension_semantics=("arbitrary", "arbitrary", "arbitrary"),
            vmem_limit_bytes=32 * MIB),
        name="dilated_far",
    )(ph3, ph3, ph3, o_near, l_near)


SB_BLK = 256
SB_DONE = 160.0


def _sb_block(q, k, v, tot, later, causal):
    dn = (((1,), (1,)), ((), ()))
    z = lax.dot_general(q, k, dn, preferred_element_type=F32)
    sp = jnp.maximum(z, 0.0) + jnp.log2(1.0 + jnp.exp2(-jnp.abs(z)))
    if causal is not None:
        sp = jnp.where(causal, sp, 0.0)
    cs = jnp.dot(sp.astype(BF16), later, preferred_element_type=F32)
    arg = (z - sp) - cs - jnp.concatenate([tot, tot], axis=1)
    if causal is not None:
        arg = jnp.where(causal, arg, NEG_INF)
    pv = jnp.dot(jnp.exp2(arg).astype(BF16), v, preferred_element_type=F32)
    rs = jnp.sum(sp, axis=1, keepdims=True)
    return pv, jnp.broadcast_to(rs, (SB_BLK, HEAD_DIM))


def _sb_first_two(q, k2, v2, later, causal):
    dn = (((1,), (1,)), ((), ()))
    z = lax.dot_general(q, k2, dn, preferred_element_type=F32)
    sp = jnp.maximum(z, 0.0) + jnp.log2(1.0 + jnp.exp2(-jnp.abs(z)))
    sp_prev = sp[:, :SB_BLK]
    sp_diag = jnp.where(causal, sp[:, SB_BLK:], 0.0)
    cs_prev = jnp.dot(sp_prev.astype(BF16), later, preferred_element_type=F32)
    cs_diag = jnp.dot(sp_diag.astype(BF16), later, preferred_element_type=F32)
    rs_diag = jnp.broadcast_to(jnp.sum(sp_diag, axis=1, keepdims=True), (SB_BLK, HEAD_DIM))
    rs_prev = jnp.broadcast_to(jnp.sum(sp_prev, axis=1, keepdims=True), (SB_BLK, HEAD_DIM))
    arg_prev = (z[:, :SB_BLK] - sp_prev) - cs_prev - jnp.concatenate([rs_diag, rs_diag], axis=1)
    arg_diag = jnp.where(causal, (z[:, SB_BLK:] - sp_diag) - cs_diag, NEG_INF)
    a = jnp.exp2(jnp.concatenate([arg_prev, arg_diag], axis=1)).astype(BF16)
    pv = jnp.dot(a, v2, preferred_element_type=F32)
    return pv, rs_diag + rs_prev


def _stickbreak_kernel(q_ref, k_ref, v_ref, o_ref, acc_ref, tot_ref):
    qi = pl.program_id(1)
    row = lax.broadcasted_iota(jnp.int32, (SB_BLK, SB_BLK), 0)
    col = lax.broadcasted_iota(jnp.int32, (SB_BLK, SB_BLK), 1)
    later = (row > col).astype(BF16)
    causal = col < row
    heads = [slice(h * HEAD_DIM, (h + 1) * HEAD_DIM) for h in range(N_HEADS)]

    def key_rows(kb, nblk=1):
        return pl.ds(pl.multiple_of(kb * SB_BLK, SB_BLK), nblk * SB_BLK)

    @pl.when(qi == 0)
    def _():
        rows = key_rows(qi)
        zero = jnp.zeros((SB_BLK, HEAD_DIM), F32)
        for hs in heads:
            pv, rs = _sb_block(q_ref[:, hs], k_ref[rows, hs], v_ref[rows, hs], zero, later, causal)
            acc_ref[:, hs] = pv
            tot_ref[:, hs] = rs

    @pl.when(qi > 0)
    def _():
        rows = key_rows(qi - 1, 2)
        for hs in heads:
            pv, rs = _sb_first_two(q_ref[:, hs], k_ref[rows, hs], v_ref[rows, hs], later, causal)
            acc_ref[:, hs] = pv
            tot_ref[:, hs] = rs

    def earlier_block(kb):
        rows = key_rows(kb)
        for hs in heads:
            pv, rs = _sb_block(q_ref[:, hs], k_ref[rows, hs], v_ref[rows, hs],
                               tot_ref[:, hs], later, None)
            acc_ref[:, hs] += pv
            tot_ref[:, hs] += rs

    def cond(c):
        kb, consumed = c
        return (kb >= 0) & (consumed < SB_DONE)

    def body(c):
        kb, _ = c
        earlier_block(kb)
        return kb - 1, jnp.min(tot_ref[...])

    lax.while_loop(cond, body, (qi - 2, jnp.min(tot_ref[...])))
    o_ref[...] = acc_ref[...].astype(o_ref.dtype)


def _stickbreak(proj3):
    b, s, _ = proj3.shape
    return pl.pallas_call(
        _stickbreak_kernel,
        grid=(b, s // SB_BLK),
        in_specs=[
            pl.BlockSpec((None, SB_BLK, W_HEADS), lambda bi, qi: (bi, qi, P_QB)),
            pl.BlockSpec((None, s, W_HEADS), lambda bi, qi: (bi, 0, P_KB),
                         pipeline_mode=pl.Buffered(1)),
            pl.BlockSpec((None, s, W_HEADS), lambda bi, qi: (bi, 0, P_VB),
                         pipeline_mode=pl.Buffered(1)),
        ],
        out_specs=pl.BlockSpec((None, SB_BLK, W_HEADS), lambda bi, qi: (bi, qi, 0)),
        out_shape=jax.ShapeDtypeStruct((b, s, W_HEADS), BF16),
        scratch_shapes=[pltpu.VMEM((SB_BLK, W_HEADS), F32), pltpu.VMEM((SB_BLK, W_HEADS), F32)],
        compiler_params=pltpu.CompilerParams(
            dimension_semantics=("arbitrary", "arbitrary"),
            vmem_limit_bytes=48 * MIB),
        name="stickbreak",
    )(proj3, proj3, proj3)


MIX_TM = 512


def _mixout_kernel(x_ref, ya_ref, yb_ref, ga_ref, gb_ref, wa_ref, wb_ref, wo_ref, o_ref):
    ya = jnp.concatenate([ya_ref[h] for h in range(N_HEADS)], axis=1).astype(BF16)
    ta = jnp.dot(ya, wa_ref[...], preferred_element_type=F32)
    tb = jnp.dot(yb_ref[...], wb_ref[...], preferred_element_type=F32)
    merged = (jax.nn.sigmoid(ga_ref[...].astype(F32)) * ta
              + jax.nn.sigmoid(gb_ref[...].astype(F32)) * tb)
    o_ref[...] = x_ref[...] + jnp.dot(merged.astype(BF16), wo_ref[...], preferred_element_type=F32)


def _resident(shape):
    return pl.BlockSpec(shape, lambda *_: (0,) * len(shape), pipeline_mode=pl.Buffered(1))


def _mixout(x2d, ya4, yb2d, proj2d, wa, wb, wo):
    t = x2d.shape[0]
    tm = MIX_TM
    tiles_per_seq = ya4.shape[2] // tm
    ga_col = ((P_VB + 1) * W_HEADS) // D_MODEL
    assert ga_col * D_MODEL == (P_VB + 1) * W_HEADS
    gb_col = ga_col + 1
    return pl.pallas_call(
        _mixout_kernel,
        grid=(t // tm,),
        in_specs=[
            pl.BlockSpec((tm, D_MODEL), lambda i: (i, 0)),
            pl.BlockSpec((None, N_HEADS, tm, HEAD_DIM),
                         lambda i: (i // tiles_per_seq, 0, i % tiles_per_seq, 0)),
            pl.BlockSpec((tm, W_HEADS), lambda i: (i, 0)),
            pl.BlockSpec((tm, D_MODEL), lambda i: (i, ga_col)),
            pl.BlockSpec((tm, D_MODEL), lambda i: (i, gb_col)),
            _resident((W_HEADS, D_MODEL)),
            _resident((W_HEADS, D_MODEL)),
            _resident((D_MODEL, D_MODEL)),
        ],
        out_specs=pl.BlockSpec((tm, D_MODEL), lambda i: (i, 0)),
        out_shape=jax.ShapeDtypeStruct((t, D_MODEL), F32),
        compiler_params=pltpu.CompilerParams(
            dimension_semantics=("arbitrary",),
            vmem_limit_bytes=56 * MIB),
        name="mixout",
    )(x2d, ya4, yb2d, proj2d, proj2d, wa, wb, wo)


MLP_TM = 1024
MLP_TF = 512


def _mlp_kernel(x_ref, g_ref, wu_ref, wd_ref, o_ref, hn_ref):
    f = pl.program_id(1)

    def chunk(hn):
        u = jnp.dot(hn, wu_ref[...], preferred_element_type=F32)
        a = jnp.square(jnp.maximum(u, 0.0)).astype(BF16)
        return jnp.dot(a, wd_ref[...], preferred_element_type=F32)

    @pl.when(f == 0)
    def _():
        x = x_ref[...]
        hn = _rms_scale(x, g_ref[...]).astype(BF16)
        hn_ref[...] = hn
        o_ref[...] = x + chunk(hn)

    @pl.when(f > 0)
    def _():
        o_ref[...] += chunk(hn_ref[...])


def _mlp(x2d, g, wu, wd):
    t = x2d.shape[0]
    tm, tf = MLP_TM, MLP_TF
    return pl.pallas_call(
        _mlp_kernel,
        grid=(t // tm, D_FF // tf),
        in_specs=[
            pl.BlockSpec((tm, D_MODEL), lambda i, f: (i, 0)),
            pl.BlockSpec((1, D_MODEL), lambda i, f: (0, 0)),
            pl.BlockSpec((D_MODEL, tf), lambda i, f: (0, f)),
            pl.BlockSpec((tf, D_MODEL), lambda i, f: (f, 0)),
        ],
        out_specs=pl.BlockSpec((tm, D_MODEL), lambda i, f: (i, 0)),
        out_shape=jax.ShapeDtypeStruct((t, D_MODEL), F32),
        scratch_shapes=[pltpu.VMEM((tm, D_MODEL), BF16)],
        compiler_params=pltpu.CompilerParams(
            dimension_semantics=("arbitrary", "arbitrary"),
            vmem_limit_bytes=56 * MIB),
        name="mlp",
    )(x2d, g, wu, wd)


PLE_TM = 512


def _ple_kernel(x_ref, p_ref, g_ref, wg_ref, wp_ref, o_ref, wg_bf_ref, wp_bf_ref):
    @pl.when(pl.program_id(0) == 0)
    def _():
        wg_bf_ref[...] = wg_ref[...].astype(BF16)
        wp_bf_ref[...] = wp_ref[...].astype(BF16)

    x = x_ref[...]
    hn = _rms_scale(x, g_ref[...]).astype(BF16)
    gate = jnp.dot(hn, wg_bf_ref[...], preferred_element_type=F32)
    pp = jnp.dot(p_ref[...].astype(BF16), wp_bf_ref[...], preferred_element_type=F32)
    o_ref[...] = x + pp * jax.nn.sigmoid(gate)


def _ple(x2d, p2d, g, wg, wp):
    t = x2d.shape[0]
    tm = PLE_TM
    ple_dim = p2d.shape[1]
    return pl.pallas_call(
        _ple_kernel,
        grid=(t // tm,),
        in_specs=[
            pl.BlockSpec((tm, D_MODEL), lambda i: (i, 0)),
            pl.BlockSpec((tm, ple_dim), lambda i: (i, 0)),
            pl.BlockSpec((1, D_MODEL), lambda i: (0, 0)),
            _resident((D_MODEL, D_MODEL)),
            _resident((ple_dim, D_MODEL)),
        ],
        out_specs=pl.BlockSpec((tm, D_MODEL), lambda i: (i, 0)),
        out_shape=jax.ShapeDtypeStruct((t, D_MODEL), F32),
        scratch_shapes=[pltpu.VMEM((D_MODEL, D_MODEL), BF16), pltpu.VMEM((ple_dim, D_MODEL), BF16)],
        compiler_params=pltpu.CompilerParams(
            dimension_semantics=("arbitrary",),
            vmem_limit_bytes=48 * MIB),
        name="ple",
    )(x2d, p2d, g, wg, wp)


def _rope_tables(seq):
    pos = jnp.arange(seq, dtype=F32)
    inv = ROPE_THETA ** (-jnp.arange(0, ROT_DIM, 2, dtype=F32) / ROT_DIM)
    ang = pos[:, None] * inv[None, :]
    cos, sin = jnp.cos(ang), jnp.sin(ang)
    rest = HEAD_DIM - ROT_DIM
    cos_t = jnp.concatenate([cos, cos, jnp.ones((seq, rest), F32)], axis=-1)
    sin_t = jnp.concatenate([-sin, sin, jnp.zeros((seq, rest), F32)], axis=-1)
    return cos_t, sin_t


def kernel(x, p, g_mix, w_in, qn_gain, kn_gain, w_branch_a, w_branch_b, w_out,
           g_mlp, w_up, w_down, g_ple, w_ple_gate, w_ple_proj):
    b, s, d = x.shape
    t = b * s
    depth = w_in.shape[0]
    cos_t, sin_t = _rope_tables(s)
    x2d = x.reshape(t, d)
    for i in range(depth):
        proj2d, nat2d, kt, ph2d = _inproj(x2d, g_mix[i][None, :], w_in[i].astype(BF16),
                                             qn_gain[i][None, :], kn_gain[i][None, :], cos_t, sin_t, s)
        proj3 = proj2d.reshape(b, s, D_PROJ)

        o_near, l_near = _near(nat2d.reshape(b, s, W_HEADS), kt, proj3)
        ya = _far(ph2d.reshape(b, s // FAR_DIL, N_MIXA * PH_SEG), o_near, l_near)
        yb2d = _stickbreak(proj3).reshape(t, W_HEADS)

        x2d = _mixout(x2d, ya, yb2d, proj2d, w_branch_a[i].astype(BF16),
                      w_branch_b[i].astype(BF16), w_out[i].astype(BF16))
        x2d = _mlp(x2d, g_mlp[i][None, :], w_up[i].astype(BF16), w_down[i].astype(BF16))
        x2d = _ple(x2d, p[i].reshape(t, -1), g_ple[i][None, :],
                   w_ple_gate[i], w_ple_proj[i])
    return x2d.reshape(b, s, d)
```

```python
import jax
import jax.numpy as jnp
import numpy as np
from jax import lax
from jax.experimental import pallas as pl
from jax.experimental.pallas import tpu as pltpu

D_MODEL = 2048
HEAD_DIM = 128
N_HEADS = 8
W_HEADS = N_HEADS * HEAD_DIM
DIL_PATTERNS = ((128, 1), (512, 4), (2048, 16))
BLOCK = 128
ROT_DIM = HEAD_DIM // 4
ROPE_THETA = 500000.0
D_FF = 4 * D_MODEL
EPS = 1e-6
D_IN = 6 * W_HEADS + 2 * D_MODEL
LOG2E = 1.4426950408889634
Q_SCALE = HEAD_DIM ** -0.5 * LOG2E

COL_QA, COL_KA, COL_VA, COL_QB, COL_KB, COL_VB = 0, 1, 2, 3, 4, 5
N_COL_TILES = D_IN // W_HEADS
PROJ_SKIP = 2
D_PROJ = D_IN - PROJ_SKIP * W_HEADS
P_VA, P_QB, P_KB, P_VB = (c - PROJ_SKIP for c in (COL_VA, COL_QB, COL_KB, COL_VB))

F32 = jnp.float32
BF16 = jnp.bfloat16
NEG_INF = float("-inf")

MIB = 1024 * 1024


def _rms_scale(x, gain):
    ms = jnp.mean(x * x, axis=-1, keepdims=True)
    return x * lax.rsqrt(ms + EPS) * gain


INPROJ_TM = 512


FAR_DIL = DIL_PATTERNS[2][1]
PH_ROWS = INPROJ_TM // FAR_DIL
PH_HEAD = FAR_DIL * HEAD_DIM
PH_SEG = N_HEADS * PH_HEAD


ROW_CHUNK = 128
SUB_DIL = 4
ROLL_DN = HEAD_DIM - ROT_DIM // 2
ROLL_UP = ROT_DIM // 2


N_MIXA = 3
HEAD_GROUPS = ((0, 1, 2), (3, 4, 5), (6, 7))


def _inproj_kernel(x_ref, g_ref, w_ref, qn_ref, kn_ref, cos_ref, sin_ref,
                   o_ref, nat_ref, kt_ref, ph_ref, hn_ref, tab_ref, raw_ref, ys_ref, yq_ref):
    j = pl.program_id(1)
    tm = x_ref.shape[0]

    @pl.when(j == 0)
    def _():
        hn_ref[...] = _rms_scale(x_ref[...], g_ref[...]).astype(BF16)
        lane = lax.broadcasted_iota(jnp.int32, (tm, HEAD_DIM), 1)
        sin_dn = jnp.where(lane < ROT_DIM // 2, sin_ref[...], 0.0)
        sin_up = jnp.where(lane >= ROT_DIM // 2, sin_ref[...], 0.0)
        for n, (gain_ref, scale) in enumerate(((qn_ref, Q_SCALE), (kn_ref, 1.0))):
            g = jnp.broadcast_to(gain_ref[...] * scale, (tm, HEAD_DIM))
            tab_ref[3 * n] = cos_ref[...] * g
            tab_ref[3 * n + 1] = sin_dn * pltpu.roll(g, ROLL_DN, 1)
            tab_ref[3 * n + 2] = sin_up * pltpu.roll(g, ROLL_UP, 1)

    def project():
        return jnp.dot(hn_ref[...], w_ref[...], preferred_element_type=F32)

    def finish_mixer_a(tile, head_group):
        tab = 3 * tile
        for h in head_group:
            sl = slice(h * HEAD_DIM, (h + 1) * HEAD_DIM)
            for c in range(tm // ROW_CHUNK):
                rows = slice(c * ROW_CHUNK, (c + 1) * ROW_CHUNK)
                a = raw_ref[tile, rows, sl]
                if tile < N_MIXA - 1:
                    rs = lax.rsqrt(jnp.mean(a * a, axis=-1, keepdims=True) + EPS)
                    a = (a * tab_ref[tab, rows, :]
                         + pltpu.roll(a, ROLL_DN, 1) * tab_ref[tab + 1, rows, :]
                         + pltpu.roll(a, ROLL_UP, 1) * tab_ref[tab + 2, rows, :]) * rs
                    if tile == 0:
                        nat_ref[rows, sl] = a.astype(BF16)
                    else:
                        kt_ref[h, :, rows] = a.T.astype(BF16)
                ys_ref[h, rows, :] = a
            for r_lo in range(SUB_DIL):
                yq_ref[h, r_lo] = ys_ref[h, pl.ds(r_lo, tm // SUB_DIL, stride=SUB_DIL), :]
                for r_hi in range(FAR_DIL // SUB_DIL):
                    r = r_lo + SUB_DIL * r_hi
                    lanes = slice(h * PH_HEAD + r * HEAD_DIM, h * PH_HEAD + (r + 1) * HEAD_DIM)
                    ph_ref[:, lanes] = yq_ref[h, r_lo, pl.ds(r_hi, PH_ROWS, stride=SUB_DIL), :].astype(BF16)

    @pl.when(j == 0)
    def _():
        raw_ref[0] = project()

    assert N_COL_TILES == 1 + N_MIXA * len(HEAD_GROUPS)
    for step in range(1, N_COL_TILES):
        @pl.when(j == step)
        def _(step=step):
            acc = project()
            if step < N_MIXA:
                raw_ref[step] = acc
            if step == COL_QB:
                acc = acc * Q_SCALE
            if step >= PROJ_SKIP:
                o_ref[...] = acc.astype(BF16)
            finish_mixer_a((step - 1) // len(HEAD_GROUPS), HEAD_GROUPS[(step - 1) % len(HEAD_GROUPS)])


def _inproj(x2d, g, w_bf16, qn, kn, cos_t, sin_t, seq):
    t = x2d.shape[0]
    tm = INPROJ_TM
    nseq = seq // tm
    tab_spec = pl.BlockSpec((tm, HEAD_DIM), lambda i, j: (i % nseq, 0))
    vec_spec = pl.BlockSpec((1, HEAD_DIM), lambda i, j: (0, 0))
    return pl.pallas_call(
        _inproj_kernel,
        grid=(t // tm, N_COL_TILES),
        in_specs=[
            pl.BlockSpec((tm, D_MODEL), lambda i, j: (i, 0)),
            pl.BlockSpec((1, D_MODEL), lambda i, j: (0, 0)),
            pl.BlockSpec((D_MODEL, W_HEADS), lambda i, j: (0, j)),
            vec_spec, vec_spec, tab_spec, tab_spec,
        ],
        out_specs=[
            pl.BlockSpec((tm, W_HEADS), lambda i, j: (i, jnp.maximum(j - PROJ_SKIP, 0))),
            pl.BlockSpec((tm, W_HEADS), lambda i, j: (i, 0)),
            pl.BlockSpec((N_HEADS, HEAD_DIM, tm), lambda i, j: (0, 0, i)),
            pl.BlockSpec((PH_ROWS, PH_SEG), lambda i, j: (i, jnp.maximum(j - 1, 0) // len(HEAD_GROUPS))),
        ],
        out_shape=[
            jax.ShapeDtypeStruct((t, D_PROJ), BF16),
            jax.ShapeDtypeStruct((t, W_HEADS), BF16),
            jax.ShapeDtypeStruct((N_HEADS, HEAD_DIM, t), BF16),
            jax.ShapeDtypeStruct((t // FAR_DIL, N_MIXA * PH_SEG), BF16),
        ],
        scratch_shapes=[
            pltpu.VMEM((tm, D_MODEL), BF16),
            pltpu.VMEM((6, tm, HEAD_DIM), F32),
            pltpu.VMEM((N_MIXA, tm, W_HEADS), F32),
            pltpu.VMEM((N_HEADS, tm, HEAD_DIM), F32),
            pltpu.VMEM((N_HEADS, SUB_DIL, tm // SUB_DIL, HEAD_DIM), F32),
        ],
        compiler_params=pltpu.CompilerParams(
            dimension_semantics=("arbitrary", "arbitrary"),
            vmem_limit_bytes=48 * MIB),
        name="inproj",
    )(x2d, g, w_bf16, qn, kn, cos_t, sin_t)


NEAR_QB = 512
NEAR_SUB = 256
NEAR_WIN = DIL_PATTERNS[1][0] + NEAR_SUB


def _near_kernel(q_ref, ktp_ref, kt_ref, vp_ref, v_ref, o_ref, l_ref, ktcat_ref, vcat_ref):
    first = pl.program_id(1) == 0
    ktcat_ref[:, :, :NEAR_QB] = ktp_ref[...]
    ktcat_ref[:, :, NEAR_QB:] = kt_ref[...]
    vcat_ref[:NEAR_QB, :] = vp_ref[...]
    vcat_ref[NEAR_QB:, :] = v_ref[...]

    qi = lax.broadcasted_iota(jnp.int32, (NEAR_SUB, NEAR_WIN), 0)
    col = lax.broadcasted_iota(jnp.int32, (NEAR_SUB, NEAR_WIN), 1)
    dist = qi + (NEAR_WIN - NEAR_SUB) - col
    (w1, _), (w4, d4) = DIL_PATTERNS[0], DIL_PATTERNS[1]
    assert d4 & (d4 - 1) == 0
    in1 = (dist >= 0) & (dist <= w1)
    in4 = (dist >= 0) & (dist <= w4) & ((dist & (d4 - 1)) == 0)
    bias = jnp.where(in1 & in4, 1.0, jnp.where(in1 | in4, 0.0, NEG_INF))

    for sb in range(NEAR_QB // NEAR_SUB):
        rows = slice(sb * NEAR_SUB, (sb + 1) * NEAR_SUB)
        win = slice(sb * NEAR_SUB, sb * NEAR_SUB + NEAR_WIN)
        bias_sb = jnp.where(first & (col < NEAR_QB - sb * NEAR_SUB), NEG_INF, bias)
        for h in range(N_HEADS):
            cols = slice(h * HEAD_DIM, (h + 1) * HEAD_DIM)
            q = q_ref[rows, cols]
            m = den = acc = None
            for c in reversed(range(NEAR_WIN // NEAR_SUB)):
                keys = slice(sb * NEAR_SUB + c * NEAR_SUB, sb * NEAR_SUB + (c + 1) * NEAR_SUB)
                lanes = slice(c * NEAR_SUB, (c + 1) * NEAR_SUB)
                s = jnp.dot(q, ktcat_ref[h, :, keys], preferred_element_type=F32) + bias_sb[:, lanes]
                m_c = jnp.max(s, axis=-1, keepdims=True)
                if m is None:
                    m = m_c
                    e = jnp.exp2(s - m)
                    den = jnp.sum(e, axis=-1, keepdims=True)
                    acc = jnp.dot(e.astype(BF16), vcat_ref[keys, cols], preferred_element_type=F32)
                else:
                    m_new = jnp.maximum(m, m_c)
                    alpha = jnp.exp2(m - m_new)
                    e = jnp.exp2(s - m_new)
                    den = den * alpha + jnp.sum(e, axis=-1, keepdims=True)
                    acc = acc * alpha + jnp.dot(e.astype(BF16), vcat_ref[keys, cols],
                                                preferred_element_type=F32)
                    m = m_new
            o_ref[h, rows, :] = acc / den
            l_ref[h, rows, :] = jnp.broadcast_to(m + jnp.log2(den), (NEAR_SUB, HEAD_DIM))


def _near(nat3, kt, proj3):
    b, s, _ = nat3.shape
    nq = s // NEAR_QB

    def cur(c):
        return lambda bi, qi: (bi, qi, c)

    def prev(c):
        return lambda bi, qi: (bi, jnp.maximum(qi - 1, 0), c)

    kt_blk = (N_HEADS, HEAD_DIM, NEAR_QB)
    blk = (None, NEAR_QB, W_HEADS)
    out_spec = pl.BlockSpec((None, N_HEADS, NEAR_QB, HEAD_DIM), lambda bi, qi: (bi, 0, qi, 0))
    return pl.pallas_call(
        _near_kernel,
        grid=(b, s // NEAR_QB),
        in_specs=[
            pl.BlockSpec(blk, cur(0)),
            pl.BlockSpec(kt_blk, lambda bi, qi: (0, 0, bi * nq + jnp.maximum(qi - 1, 0))),
            pl.BlockSpec(kt_blk, lambda bi, qi: (0, 0, bi * nq + qi)),
            pl.BlockSpec(blk, prev(P_VA)), pl.BlockSpec(blk, cur(P_VA)),
        ],
        out_specs=[out_spec, out_spec],
        out_shape=[jax.ShapeDtypeStruct((b, N_HEADS, s, HEAD_DIM), F32)] * 2,
        scratch_shapes=[pltpu.VMEM((N_HEADS, HEAD_DIM, 2 * NEAR_QB), BF16),
                        pltpu.VMEM((2 * NEAR_QB, W_HEADS), BF16)],
        compiler_params=pltpu.CompilerParams(
            dimension_semantics=("arbitrary", "arbitrary"),
            vmem_limit_bytes=40 * MIB),
        name="dilated_near",
    )(nat3, kt, kt, proj3, proj3)


FAR_ROWS = BLOCK * FAR_DIL


def _far_kernel(q_ref, k_ref, v_ref, on_ref, ln_ref, y_ref, kcat_ref, vcat_ref):
    first = pl.program_id(2) == 0

    @pl.when(first)
    def _():
        kcat_ref[:BLOCK, :] = jnp.zeros((BLOCK, PH_HEAD), BF16)
        vcat_ref[:BLOCK, :] = jnp.zeros((BLOCK, PH_HEAD), BF16)

    @pl.when(jnp.logical_not(first))
    def _():
        kcat_ref[:BLOCK, :] = kcat_ref[BLOCK:, :]
        vcat_ref[:BLOCK, :] = vcat_ref[BLOCK:, :]

    kcat_ref[BLOCK:, :] = k_ref[...]
    vcat_ref[BLOCK:, :] = v_ref[...]
    qi = lax.broadcasted_iota(jnp.int32, (BLOCK, 2 * BLOCK), 0)
    col = lax.broadcasted_iota(jnp.int32, (BLOCK, 2 * BLOCK), 1)
    dist = qi + BLOCK - col
    bias = jnp.where((dist >= 0) & (dist <= BLOCK), 0.0, NEG_INF)
    bias = jnp.where(first & (col < BLOCK), NEG_INF, bias)
    dn = (((1,), (1,)), ((), ()))

    for r in range(FAR_DIL):
        lanes = slice(r * HEAD_DIM, (r + 1) * HEAD_DIM)
        tokens = pl.ds(r, BLOCK, stride=FAR_DIL)
        s = lax.dot_general(q_ref[:, lanes], kcat_ref[:, lanes], dn, preferred_element_type=F32) + bias
        m = jnp.max(s, axis=-1, keepdims=True)
        e = jnp.exp2(s - m)
        den = jnp.sum(e, axis=-1, keepdims=True)
        o = jnp.dot(e.astype(BF16), vcat_ref[:, lanes], preferred_element_type=F32) / den
        lse = m + jnp.log2(den)
        l_near = ln_ref[tokens, :]
        mm = jnp.maximum(l_near, lse)
        w_near = jnp.exp2(l_near - mm)
        w_far = jnp.exp2(lse - mm)
        y_ref[tokens, :] = (on_ref[tokens, :] * w_near + o * w_far) / (w_near + w_far)


def _far(ph3, o_near, l_near):
    b, m, _ = ph3.shape
    s = m * FAR_DIL

    def seg(n):
        return lambda bi, h, mi: (bi, mi, n * N_HEADS + h)

    blk = (None, BLOCK, PH_HEAD)
    tok_spec = pl.BlockSpec((None, None, FAR_ROWS, HEAD_DIM), lambda bi, h, mi: (bi, h, mi, 0))
    return pl.pallas_call(
        _far_kernel,
        grid=(b, N_HEADS, m // BLOCK),
        in_specs=[
            pl.BlockSpec(blk, seg(0)), pl.BlockSpec(blk, seg(1)), pl.BlockSpec(blk, seg(2)),
            tok_spec, tok_spec,
        ],
        out_specs=tok_spec,
        out_shape=jax.ShapeDtypeStruct((b, N_HEADS, s, HEAD_DIM), F32),
        scratch_shapes=[pltpu.VMEM((2 * BLOCK, PH_HEAD), BF16)] * 2,
        compiler_params=pltpu.CompilerParams(
            dimension_semantics=("arbitrary", "arbitrary", "arbitrary"),
            vmem_limit_bytes=32 * MIB),
        name="dilated_far",
    )(ph3, ph3, ph3, o_near, l_near)


SB_BLK = 256
SB_DONE = 160.0


def _sb_block(q, k, v, tot, later, causal):
    dn = (((1,), (1,)), ((), ()))
    z = lax.dot_general(q, k, dn, preferred_element_type=F32)
    sp = jnp.maximum(z, 0.0) + jnp.log2(1.0 + jnp.exp2(-jnp.abs(z)))
    if causal is not None:
        sp = jnp.where(causal, sp, 0.0)
    cs = jnp.dot(sp.astype(BF16), later, preferred_element_type=F32)
    arg = (z - sp) - cs - jnp.concatenate([tot, tot], axis=1)
    if causal is not None:
        arg = jnp.where(causal, arg, NEG_INF)
    pv = jnp.dot(jnp.exp2(arg).astype(BF16), v, preferred_element_type=F32)
    rs = jnp.sum(sp, axis=1, keepdims=True)
    return pv, jnp.broadcast_to(rs, (SB_BLK, HEAD_DIM))


def _sb_first_two(q, k2, v2, later, causal):
    dn = (((1,), (1,)), ((), ()))
    z = lax.dot_general(q, k2, dn, preferred_element_type=F32)
    sp = jnp.maximum(z, 0.0) + jnp.log2(1.0 + jnp.exp2(-jnp.abs(z)))
    sp_prev = sp[:, :SB_BLK]
    sp_diag = jnp.where(causal, sp[:, SB_BLK:], 0.0)
    cs_prev = jnp.dot(sp_prev.astype(BF16), later, preferred_element_type=F32)
    cs_diag = jnp.dot(sp_diag.astype(BF16), later, preferred_element_type=F32)
    rs_diag = jnp.broadcast_to(jnp.sum(sp_diag, axis=1, keepdims=True), (SB_BLK, HEAD_DIM))
    rs_prev = jnp.broadcast_to(jnp.sum(sp_prev, axis=1, keepdims=True), (SB_BLK, HEAD_DIM))
    arg_prev = (z[:, :SB_BLK] - sp_prev) - cs_prev - jnp.concatenate([rs_diag, rs_diag], axis=1)
    arg_diag = jnp.where(causal, (z[:, SB_BLK:] - sp_diag) - cs_diag, NEG_INF)
    a = jnp.exp2(jnp.concatenate([arg_prev, arg_diag], axis=1)).astype(BF16)
    pv = jnp.dot(a, v2, preferred_element_type=F32)
    return pv, rs_diag + rs_prev


def _stickbreak_kernel(q_ref, k_ref, v_ref, o_ref, acc_ref, tot_ref):
    qi = pl.program_id(1)
    row = lax.broadcasted_iota(jnp.int32, (SB_BLK, SB_BLK), 0)
    col = lax.broadcasted_iota(jnp.int32, (SB_BLK, SB_BLK), 1)
    later = (row > col).astype(BF16)
    causal = col < row
    heads = [slice(h * HEAD_DIM, (h + 1) * HEAD_DIM) for h in range(N_HEADS)]

    def key_rows(kb, nblk=1):
        return pl.ds(pl.multiple_of(kb * SB_BLK, SB_BLK), nblk * SB_BLK)

    @pl.when(qi == 0)
    def _():
        rows = key_rows(qi)
        zero = jnp.zeros((SB_BLK, HEAD_DIM), F32)
        for hs in heads:
            pv, rs = _sb_block(q_ref[:, hs], k_ref[rows, hs], v_ref[rows, hs], zero, later, causal)
            acc_ref[:, hs] = pv
            tot_ref[:, hs] = rs

    @pl.when(qi > 0)
    def _():
        rows = key_rows(qi - 1, 2)
        for hs in heads:
            pv, rs = _sb_first_two(q_ref[:, hs], k_ref[rows, hs], v_ref[rows, hs], later, causal)
            acc_ref[:, hs] = pv
            tot_ref[:, hs] = rs

    def earlier_block(kb):
        rows = key_rows(kb)
        for hs in heads:
            pv, rs = _sb_block(q_ref[:, hs], k_ref[rows, hs], v_ref[rows, hs],
                               tot_ref[:, hs], later, None)
            acc_ref[:, hs] += pv
            tot_ref[:, hs] += rs

    def cond(c):
        kb, consumed = c
        return (kb >= 0) & (consumed < SB_DONE)

    def body(c):
        kb, _ = c
        earlier_block(kb)
        return kb - 1, jnp.min(tot_ref[...])

    lax.while_loop(cond, body, (qi - 2, jnp.min(tot_ref[...])))
    o_ref[...] = acc_ref[...].astype(o_ref.dtype)


def _stickbreak(proj3):
    b, s, _ = proj3.shape
    return pl.pallas_call(
        _stickbreak_kernel,
        grid=(b, s // SB_BLK),
        in_specs=[
            pl.BlockSpec((None, SB_BLK, W_HEADS), lambda bi, qi: (bi, qi, P_QB)),
            pl.BlockSpec((None, s, W_HEADS), lambda bi, qi: (bi, 0, P_KB),
                         pipeline_mode=pl.Buffered(1)),
            pl.BlockSpec((None, s, W_HEADS), lambda bi, qi: (bi, 0, P_VB),
                         pipeline_mode=pl.Buffered(1)),
        ],
        out_specs=pl.BlockSpec((None, SB_BLK, W_HEADS), lambda bi, qi: (bi, qi, 0)),
        out_shape=jax.ShapeDtypeStruct((b, s, W_HEADS), BF16),
        scratch_shapes=[pltpu.VMEM((SB_BLK, W_HEADS), F32), pltpu.VMEM((SB_BLK, W_HEADS), F32)],
        compiler_params=pltpu.CompilerParams(
            dimension_semantics=("arbitrary", "arbitrary"),
            vmem_limit_bytes=48 * MIB),
        name="stickbreak",
    )(proj3, proj3, proj3)


MIX_TM = 512


def _mixout_kernel(x_ref, ya_ref, yb_ref, ga_ref, gb_ref, wa_ref, wb_ref, wo_ref, o_ref):
    ya = jnp.concatenate([ya_ref[h] for h in range(N_HEADS)], axis=1).astype(BF16)
    ta = jnp.dot(ya, wa_ref[...], preferred_element_type=F32)
    tb = jnp.dot(yb_ref[...], wb_ref[...], preferred_element_type=F32)
    merged = (jax.nn.sigmoid(ga_ref[...].astype(F32)) * ta
              + jax.nn.sigmoid(gb_ref[...].astype(F32)) * tb)
    o_ref[...] = x_ref[...] + jnp.dot(merged.astype(BF16), wo_ref[...], preferred_element_type=F32)


def _resident(shape):
    return pl.BlockSpec(shape, lambda *_: (0,) * len(shape), pipeline_mode=pl.Buffered(1))


def _mixout(x2d, ya4, yb2d, proj2d, wa, wb, wo):
    t = x2d.shape[0]
    tm = MIX_TM
    tiles_per_seq = ya4.shape[2] // tm
    ga_col = ((P_VB + 1) * W_HEADS) // D_MODEL
    assert ga_col * D_MODEL == (P_VB + 1) * W_HEADS
    gb_col = ga_col + 1
    return pl.pallas_call(
        _mixout_kernel,
        grid=(t // tm,),
        in_specs=[
            pl.BlockSpec((tm, D_MODEL), lambda i: (i, 0)),
            pl.BlockSpec((None, N_HEADS, tm, HEAD_DIM),
                         lambda i: (i // tiles_per_seq, 0, i % tiles_per_seq, 0)),
            pl.BlockSpec((tm, W_HEADS), lambda i: (i, 0)),
            pl.BlockSpec((tm, D_MODEL), lambda i: (i, ga_col)),
            pl.BlockSpec((tm, D_MODEL), lambda i: (i, gb_col)),
            _resident((W_HEADS, D_MODEL)),
            _resident((W_HEADS, D_MODEL)),
            _resident((D_MODEL, D_MODEL)),
        ],
        out_specs=pl.BlockSpec((tm, D_MODEL), lambda i: (i, 0)),
        out_shape=jax.ShapeDtypeStruct((t, D_MODEL), F32),
        compiler_params=pltpu.CompilerParams(
            dimension_semantics=("arbitrary",),
            vmem_limit_bytes=56 * MIB),
        name="mixout",
    )(x2d, ya4, yb2d, proj2d, proj2d, wa, wb, wo)


MLP_TM = 1024
MLP_TF = 512


def _mlp_kernel(x_ref, g_ref, wu_ref, wd_ref, o_ref, hn_ref):
    f = pl.program_id(1)

    def chunk(hn):
        u = jnp.dot(hn, wu_ref[...], preferred_element_type=F32)
        a = jnp.square(jnp.maximum(u, 0.0)).astype(BF16)
        return jnp.dot(a, wd_ref[...], preferred_element_type=F32)

    @pl.when(f == 0)
    def _():
        x = x_ref[...]
        hn = _rms_scale(x, g_ref[...]).astype(BF16)
        hn_ref[...] = hn
        o_ref[...] = x + chunk(hn)

    @pl.when(f > 0)
    def _():
        o_ref[...] += chunk(hn_ref[...])


def _mlp(x2d, g, wu, wd):
    t = x2d.shape[0]
    tm, tf = MLP_TM, MLP_TF
    return pl.pallas_call(
        _mlp_kernel,
        grid=(t // tm, D_FF // tf),
        in_specs=[
            pl.BlockSpec((tm, D_MODEL), lambda i, f: (i, 0)),
            pl.BlockSpec((1, D_MODEL), lambda i, f: (0, 0)),
            pl.BlockSpec((D_MODEL, tf), lambda i, f: (0, f)),
            pl.BlockSpec((tf, D_MODEL), lambda i, f: (f, 0)),
        ],
        out_specs=pl.BlockSpec((tm, D_MODEL), lambda i, f: (i, 0)),
        out_shape=jax.ShapeDtypeStruct((t, D_MODEL), F32),
        scratch_shapes=[pltpu.VMEM((tm, D_MODEL), BF16)],
        compiler_params=pltpu.CompilerParams(
            dimension_semantics=("arbitrary", "arbitrary"),
            vmem_limit_bytes=56 * MIB),
        name="mlp",
    )(x2d, g, wu, wd)


PLE_TM = 512


def _ple_kernel(x_ref, p_ref, g_ref, wg_ref, wp_ref, o_ref, wg_bf_ref, wp_bf_ref):
    @pl.when(pl.program_id(0) == 0)
    def _():
        wg_bf_ref[...] = wg_ref[...].astype(BF16)
        wp_bf_ref[...] = wp_ref[...].astype(BF16)

    x = x_ref[...]
    hn = _rms_scale(x, g_ref[...]).astype(BF16)
    gate = jnp.dot(hn, wg_bf_ref[...], preferred_element_type=F32)
    pp = jnp.dot(p_ref[...].astype(BF16), wp_bf_ref[...], preferred_element_type=F32)
    o_ref[...] = x + pp * jax.nn.sigmoid(gate)


def _ple(x2d, p2d, g, wg, wp):
    t = x2d.shape[0]
    tm = PLE_TM
    ple_dim = p2d.shape[1]
    return pl.pallas_call(
        _ple_kernel,
        grid=(t // tm,),
        in_specs=[
            pl.BlockSpec((tm, D_MODEL), lambda i: (i, 0)),
            pl.BlockSpec((tm, ple_dim), lambda i: (i, 0)),
            pl.BlockSpec((1, D_MODEL), lambda i: (0, 0)),
            _resident((D_MODEL, D_MODEL)),
            _resident((ple_dim, D_MODEL)),
        ],
        out_specs=pl.BlockSpec((tm, D_MODEL), lambda i: (i, 0)),
        out_shape=jax.ShapeDtypeStruct((t, D_MODEL), F32),
        scratch_shapes=[pltpu.VMEM((D_MODEL, D_MODEL), BF16), pltpu.VMEM((ple_dim, D_MODEL), BF16)],
        compiler_params=pltpu.CompilerParams(
            dimension_semantics=("arbitrary",),
            vmem_limit_bytes=48 * MIB),
        name="ple",
    )(x2d, p2d, g, wg, wp)


def _rope_tables(seq):
    pos = np.arange(seq, dtype=np.float64)
    inv = ROPE_THETA ** (-np.arange(0, ROT_DIM, 2, dtype=np.float64) / ROT_DIM)
    ang = pos[:, None] * inv[None, :]
    cos, sin = np.cos(ang), np.sin(ang)
    rest = HEAD_DIM - ROT_DIM
    cos_t = np.concatenate([cos, cos, np.ones((seq, rest))], axis=-1)
    sin_t = np.concatenate([-sin, sin, np.zeros((seq, rest))], axis=-1)
    return jnp.asarray(cos_t, F32), jnp.asarray(sin_t, F32)


def kernel(x, p, g_mix, w_in, qn_gain, kn_gain, w_branch_a, w_branch_b, w_out,
           g_mlp, w_up, w_down, g_ple, w_ple_gate, w_ple_proj):
    b, s, d = x.shape
    t = b * s
    depth = w_in.shape[0]
    cos_t, sin_t = _rope_tables(s)
    x2d = x.reshape(t, d)
    for i in range(depth):
        proj2d, nat2d, kt, ph2d = _inproj(x2d, g_mix[i][None, :], w_in[i].astype(BF16),
                                             qn_gain[i][None, :], kn_gain[i][None, :], cos_t, sin_t, s)
        proj3 = proj2d.reshape(b, s, D_PROJ)

        o_near, l_near = _near(nat2d.reshape(b, s, W_HEADS), kt, proj3)
        ya = _far(ph2d.reshape(b, s // FAR_DIL, N_MIXA * PH_SEG), o_near, l_near)
        yb2d = _stickbreak(proj3).reshape(t, W_HEADS)

        x2d = _mixout(x2d, ya, yb2d, proj2d, w_branch_a[i].astype(BF16),
                      w_branch_b[i].astype(BF16), w_out[i].astype(BF16))
        x2d = _mlp(x2d, g_mlp[i][None, :], w_up[i].astype(BF16), w_down[i].astype(BF16))
        x2d = _ple(x2d, p[i].reshape(t, -1), g_ple[i][None, :],
                   w_ple_gate[i], w_ple_proj[i])
    return x2d.reshape(b, s, d)
```

```python
import jax
import jax.numpy as jnp
import numpy as np
from jax import lax
from jax.experimental import pallas as pl
from jax.experimental.pallas import tpu as pltpu

D_MODEL = 2048
HEAD_DIM = 128
N_HEADS = 8
W_HEADS = N_HEADS * HEAD_DIM
DIL_PATTERNS = ((128, 1), (512, 4), (2048, 16))
BLOCK = 128
ROT_DIM = HEAD_DIM // 4
ROPE_THETA = 500000.0
D_FF = 4 * D_MODEL
EPS = 1e-6
D_IN = 6 * W_HEADS + 2 * D_MODEL
LOG2E = 1.4426950408889634
Q_SCALE = HEAD_DIM ** -0.5 * LOG2E

COL_QA, COL_KA, COL_VA, COL_QB, COL_KB, COL_VB = 0, 1, 2, 3, 4, 5
N_COL_TILES = D_IN // W_HEADS
PROJ_SKIP = 2
D_PROJ = D_IN - PROJ_SKIP * W_HEADS
P_VA, P_QB, P_KB, P_VB = (c - PROJ_SKIP for c in (COL_VA, COL_QB, COL_KB, COL_VB))

F32 = jnp.float32
BF16 = jnp.bfloat16
NEG_INF = float("-inf")

MIB = 1024 * 1024


def _rms_scale(x, gain):
    ms = jnp.mean(x * x, axis=-1, keepdims=True)
    return x * lax.rsqrt(ms + EPS) * gain


INPROJ_TM = 512


FAR_DIL = DIL_PATTERNS[2][1]
PH_ROWS = INPROJ_TM // FAR_DIL
PH_HEAD = FAR_DIL * HEAD_DIM
PH_SEG = N_HEADS * PH_HEAD


ROW_CHUNK = 128
SUB_DIL = 4
ROLL_DN = HEAD_DIM - ROT_DIM // 2
ROLL_UP = ROT_DIM // 2


N_MIXA = 3
HEAD_GROUPS = ((0, 1, 2), (3, 4, 5), (6, 7))


def _inproj_kernel(x_ref, g_ref, w_ref, qn_ref, kn_ref, cos_ref, sin_ref,
                   o_ref, nat_ref, kt_ref, ph_ref, hn_ref, tab_ref, raw_ref, ys_ref, yq_ref):
    j = pl.program_id(1)
    tm = x_ref.shape[0]

    @pl.when(j == 0)
    def _():
        hn_ref[...] = _rms_scale(x_ref[...], g_ref[...]).astype(BF16)
        lane = lax.broadcasted_iota(jnp.int32, (tm, HEAD_DIM), 1)
        sin_dn = jnp.where(lane < ROT_DIM // 2, sin_ref[...], 0.0)
        sin_up = jnp.where(lane >= ROT_DIM // 2, sin_ref[...], 0.0)
        for n, (gain_ref, scale) in enumerate(((qn_ref, Q_SCALE), (kn_ref, 1.0))):
            g = jnp.broadcast_to(gain_ref[...] * scale, (tm, HEAD_DIM))
            tab_ref[3 * n] = cos_ref[...] * g
            tab_ref[3 * n + 1] = sin_dn * pltpu.roll(g, ROLL_DN, 1)
            tab_ref[3 * n + 2] = sin_up * pltpu.roll(g, ROLL_UP, 1)

    def project():
        return jnp.dot(hn_ref[...], w_ref[...], preferred_element_type=F32)

    def finish_mixer_a(tile, head_group):
        tab = 3 * tile
        for h in head_group:
            sl = slice(h * HEAD_DIM, (h + 1) * HEAD_DIM)
            for c in range(tm // ROW_CHUNK):
                rows = slice(c * ROW_CHUNK, (c + 1) * ROW_CHUNK)
                a = raw_ref[tile, rows, sl]
                if tile < N_MIXA - 1:
                    rs = lax.rsqrt(jnp.mean(a * a, axis=-1, keepdims=True) + EPS)
                    a = (a * tab_ref[tab, rows, :]
                         + pltpu.roll(a, ROLL_DN, 1) * tab_ref[tab + 1, rows, :]
                         + pltpu.roll(a, ROLL_UP, 1) * tab_ref[tab + 2, rows, :]) * rs
                    if tile == 0:
                        nat_ref[rows, sl] = a.astype(BF16)
                    else:
                        kt_ref[h, :, rows] = a.T.astype(BF16)
                ys_ref[h, rows, :] = a
            for r_lo in range(SUB_DIL):
                yq_ref[h, r_lo] = ys_ref[h, pl.ds(r_lo, tm // SUB_DIL, stride=SUB_DIL), :]
                for r_hi in range(FAR_DIL // SUB_DIL):
                    r = r_lo + SUB_DIL * r_hi
                    lanes = slice(h * PH_HEAD + r * HEAD_DIM, h * PH_HEAD + (r + 1) * HEAD_DIM)
                    ph_ref[:, lanes] = yq_ref[h, r_lo, pl.ds(r_hi, PH_ROWS, stride=SUB_DIL), :].astype(BF16)

    @pl.when(j == 0)
    def _():
        raw_ref[0] = project()

    assert N_COL_TILES == 1 + N_MIXA * len(HEAD_GROUPS)
    for step in range(1, N_COL_TILES):
        @pl.when(j == step)
        def _(step=step):
            acc = project()
            if step < N_MIXA:
                raw_ref[step] = acc
            if step == COL_QB:
                acc = acc * Q_SCALE
            if step >= PROJ_SKIP:
                o_ref[...] = acc.astype(BF16)
            finish_mixer_a((step - 1) // len(HEAD_GROUPS), HEAD_GROUPS[(step - 1) % len(HEAD_GROUPS)])


def _inproj(x2d, g, w_bf16, qn, kn, cos_t, sin_t, seq):
    t = x2d.shape[0]
    tm = INPROJ_TM
    nseq = seq // tm
    tab_spec = pl.BlockSpec((tm, HEAD_DIM), lambda i, j: (i % nseq, 0))
    vec_spec = pl.BlockSpec((1, HEAD_DIM), lambda i, j: (0, 0))
    return pl.pallas_call(
        _inproj_kernel,
        grid=(t // tm, N_COL_TILES),
        in_specs=[
            pl.BlockSpec((tm, D_MODEL), lambda i, j: (i, 0)),
            pl.BlockSpec((1, D_MODEL), lambda i, j: (0, 0)),
            pl.BlockSpec((D_MODEL, W_HEADS), lambda i, j: (0, j)),
            vec_spec, vec_spec, tab_spec, tab_spec,
        ],
        out_specs=[
            pl.BlockSpec((tm, W_HEADS), lambda i, j: (i, jnp.maximum(j - PROJ_SKIP, 0))),
            pl.BlockSpec((tm, W_HEADS), lambda i, j: (i, 0)),
            pl.BlockSpec((None, N_HEADS, HEAD_DIM, tm), lambda i, j: (i, 0, 0, 0)),
            pl.BlockSpec((PH_ROWS, PH_SEG), lambda i, j: (i, jnp.maximum(j - 1, 0) // len(HEAD_GROUPS))),
        ],
        out_shape=[
            jax.ShapeDtypeStruct((t, D_PROJ), BF16),
            jax.ShapeDtypeStruct((t, W_HEADS), BF16),
            jax.ShapeDtypeStruct((t // tm, N_HEADS, HEAD_DIM, tm), BF16),
            jax.ShapeDtypeStruct((t // FAR_DIL, N_MIXA * PH_SEG), BF16),
        ],
        scratch_shapes=[
            pltpu.VMEM((tm, D_MODEL), BF16),
            pltpu.VMEM((6, tm, HEAD_DIM), F32),
            pltpu.VMEM((N_MIXA, tm, W_HEADS), F32),
            pltpu.VMEM((N_HEADS, tm, HEAD_DIM), F32),
            pltpu.VMEM((N_HEADS, SUB_DIL, tm // SUB_DIL, HEAD_DIM), F32),
        ],
        compiler_params=pltpu.CompilerParams(
            dimension_semantics=("arbitrary", "arbitrary"),
            vmem_limit_bytes=48 * MIB),
        name="inproj",
    )(x2d, g, w_bf16, qn, kn, cos_t, sin_t)


NEAR_QB = 512
NEAR_SUB = 256
NEAR_WIN = DIL_PATTERNS[1][0] + NEAR_SUB


def _near_kernel(q_ref, ktp_ref, kt_ref, vp_ref, v_ref, o_ref, l_ref, ktcat_ref, vcat_ref):
    first = pl.program_id(1) == 0
    ktcat_ref[:, :, :NEAR_QB] = ktp_ref[...]
    ktcat_ref[:, :, NEAR_QB:] = kt_ref[...]
    vcat_ref[:NEAR_QB, :] = vp_ref[...]
    vcat_ref[NEAR_QB:, :] = v_ref[...]

    qi = lax.broadcasted_iota(jnp.int32, (NEAR_SUB, NEAR_WIN), 0)
    col = lax.broadcasted_iota(jnp.int32, (NEAR_SUB, NEAR_WIN), 1)
    dist = qi + (NEAR_WIN - NEAR_SUB) - col
    (w1, _), (w4, d4) = DIL_PATTERNS[0], DIL_PATTERNS[1]
    assert d4 & (d4 - 1) == 0
    in1 = (dist >= 0) & (dist <= w1)
    in4 = (dist >= 0) & (dist <= w4) & ((dist & (d4 - 1)) == 0)
    bias = jnp.where(in1 & in4, 1.0, jnp.where(in1 | in4, 0.0, NEG_INF))

    for sb in range(NEAR_QB // NEAR_SUB):
        rows = slice(sb * NEAR_SUB, (sb + 1) * NEAR_SUB)
        win = slice(sb * NEAR_SUB, sb * NEAR_SUB + NEAR_WIN)
        bias_sb = jnp.where(first & (col < NEAR_QB - sb * NEAR_SUB), NEG_INF, bias)
        for h in range(N_HEADS):
            cols = slice(h * HEAD_DIM, (h + 1) * HEAD_DIM)
            q = q_ref[rows, cols]
            m = den = acc = None
            for c in reversed(range(NEAR_WIN // NEAR_SUB)):
                keys = slice(sb * NEAR_SUB + c * NEAR_SUB, sb * NEAR_SUB + (c + 1) * NEAR_SUB)
                lanes = slice(c * NEAR_SUB, (c + 1) * NEAR_SUB)
                s = jnp.dot(q, ktcat_ref[h, :, keys], preferred_element_type=F32) + bias_sb[:, lanes]
                m_c = jnp.max(s, axis=-1, keepdims=True)
                if m is None:
                    m = m_c
                    e = jnp.exp2(s - m)
                    den = jnp.sum(e, axis=-1, keepdims=True)
                    acc = jnp.dot(e.astype(BF16), vcat_ref[keys, cols], preferred_element_type=F32)
                else:
                    m_new = jnp.maximum(m, m_c)
                    alpha = jnp.exp2(m - m_new)
                    e = jnp.exp2(s - m_new)
                    den = den * alpha + jnp.sum(e, axis=-1, keepdims=True)
                    acc = acc * alpha + jnp.dot(e.astype(BF16), vcat_ref[keys, cols],
                                                preferred_element_type=F32)
                    m = m_new
            o_ref[h, rows, :] = acc / den
            l_ref[h, rows, :] = jnp.broadcast_to(m + jnp.log2(den), (NEAR_SUB, HEAD_DIM))


def _near(nat3, kt, proj3):
    b, s, _ = nat3.shape
    nq = s // NEAR_QB

    def cur(c):
        return lambda bi, qi: (bi, qi, c)

    def prev(c):
        return lambda bi, qi: (bi, jnp.maximum(qi - 1, 0), c)

    assert kt.shape[1:] == (N_HEADS, HEAD_DIM, NEAR_QB)
    kt_blk = (None, N_HEADS, HEAD_DIM, NEAR_QB)
    blk = (None, NEAR_QB, W_HEADS)
    out_spec = pl.BlockSpec((None, N_HEADS, NEAR_QB, HEAD_DIM), lambda bi, qi: (bi, 0, qi, 0))
    return pl.pallas_call(
        _near_kernel,
        grid=(b, s // NEAR_QB),
        in_specs=[
            pl.BlockSpec(blk, cur(0)),
            pl.BlockSpec(kt_blk, lambda bi, qi: (bi * nq + jnp.maximum(qi - 1, 0), 0, 0, 0)),
            pl.BlockSpec(kt_blk, lambda bi, qi: (bi * nq + qi, 0, 0, 0)),
            pl.BlockSpec(blk, prev(P_VA)), pl.BlockSpec(blk, cur(P_VA)),
        ],
        out_specs=[out_spec, out_spec],
        out_shape=[jax.ShapeDtypeStruct((b, N_HEADS, s, HEAD_DIM), F32)] * 2,
        scratch_shapes=[pltpu.VMEM((N_HEADS, HEAD_DIM, 2 * NEAR_QB), BF16),
                        pltpu.VMEM((2 * NEAR_QB, W_HEADS), BF16)],
        compiler_params=pltpu.CompilerParams(
            dimension_semantics=("arbitrary", "arbitrary"),
            vmem_limit_bytes=40 * MIB),
        name="dilated_near",
    )(nat3, kt, kt, proj3, proj3)


FAR_ROWS = BLOCK * FAR_DIL


def _far_kernel(q_ref, k_ref, v_ref, on_ref, ln_ref, y_ref, kcat_ref, vcat_ref):
    first = pl.program_id(2) == 0

    @pl.when(first)
    def _():
        kcat_ref[:BLOCK, :] = jnp.zeros((BLOCK, PH_HEAD), BF16)
        vcat_ref[:BLOCK, :] = jnp.zeros((BLOCK, PH_HEAD), BF16)

    @pl.when(jnp.logical_not(first))
    def _():
        kcat_ref[:BLOCK, :] = kcat_ref[BLOCK:, :]
        vcat_ref[:BLOCK, :] = vcat_ref[BLOCK:, :]

    kcat_ref[BLOCK:, :] = k_ref[...]
    vcat_ref[BLOCK:, :] = v_ref[...]
    qi = lax.broadcasted_iota(jnp.int32, (BLOCK, 2 * BLOCK), 0)
    col = lax.broadcasted_iota(jnp.int32, (BLOCK, 2 * BLOCK), 1)
    dist = qi + BLOCK - col
    bias = jnp.where((dist >= 0) & (dist <= BLOCK), 0.0, NEG_INF)
    bias = jnp.where(first & (col < BLOCK), NEG_INF, bias)
    dn = (((1,), (1,)), ((), ()))

    for r in range(FAR_DIL):
        lanes = slice(r * HEAD_DIM, (r + 1) * HEAD_DIM)
        tokens = pl.ds(r, BLOCK, stride=FAR_DIL)
        s = lax.dot_general(q_ref[:, lanes], kcat_ref[:, lanes], dn, preferred_element_type=F32) + bias
        m = jnp.max(s, axis=-1, keepdims=True)
        e = jnp.exp2(s - m)
        den = jnp.sum(e, axis=-1, keepdims=True)
        o = jnp.dot(e.astype(BF16), vcat_ref[:, lanes], preferred_element_type=F32) / den
        lse = m + jnp.log2(den)
        l_near = ln_ref[tokens, :]
        mm = jnp.maximum(l_near, lse)
        w_near = jnp.exp2(l_near - mm)
        w_far = jnp.exp2(lse - mm)
        y_ref[tokens, :] = (on_ref[tokens, :] * w_near + o * w_far) / (w_near + w_far)


def _far(ph3, o_near, l_near):
    b, m, _ = ph3.shape
    s = m * FAR_DIL

    def seg(n):
        return lambda bi, h, mi: (bi, mi, n * N_HEADS + h)

    blk = (None, BLOCK, PH_HEAD)
    tok_spec = pl.BlockSpec((None, None, FAR_ROWS, HEAD_DIM), lambda bi, h, mi: (bi, h, mi, 0))
    return pl.pallas_call(
        _far_kernel,
        grid=(b, N_HEADS, m // BLOCK),
        in_specs=[
            pl.BlockSpec(blk, seg(0)), pl.BlockSpec(blk, seg(1)), pl.BlockSpec(blk, seg(2)),
            tok_spec, tok_spec,
        ],
        out_specs=tok_spec,
        out_shape=jax.ShapeDtypeStruct((b, N_HEADS, s, HEAD_DIM), F32),
        scratch_shapes=[pltpu.VMEM((2 * BLOCK, PH_HEAD), BF16)] * 2,
        compiler_params=pltpu.CompilerParams(
            dimension_semantics=("arbitrary", "arbitrary", "arbitrary"),
            vmem_limit_bytes=32 * MIB),
        name="dilated_far",
    )(ph3, ph3, ph3, o_near, l_near)


SB_BLK = 256
SB_DONE = 160.0


def _sb_block(q, k, v, tot, later, causal):
    dn = (((1,), (1,)), ((), ()))
    z = lax.dot_general(q, k, dn, preferred_element_type=F32)
    sp = jnp.maximum(z, 0.0) + jnp.log2(1.0 + jnp.exp2(-jnp.abs(z)))
    if causal is not None:
        sp = jnp.where(causal, sp, 0.0)
    cs = jnp.dot(sp.astype(BF16), later, preferred_element_type=F32)
    arg = (z - sp) - cs - jnp.concatenate([tot, tot], axis=1)
    if causal is not None:
        arg = jnp.where(causal, arg, NEG_INF)
    pv = jnp.dot(jnp.exp2(arg).astype(BF16), v, preferred_element_type=F32)
    rs = jnp.sum(sp, axis=1, keepdims=True)
    return pv, jnp.broadcast_to(rs, (SB_BLK, HEAD_DIM))


def _sb_first_two(q, k2, v2, later, causal):
    dn = (((1,), (1,)), ((), ()))
    z = lax.dot_general(q, k2, dn, preferred_element_type=F32)
    sp = jnp.maximum(z, 0.0) + jnp.log2(1.0 + jnp.exp2(-jnp.abs(z)))
    sp_prev = sp[:, :SB_BLK]
    sp_diag = jnp.where(causal, sp[:, SB_BLK:], 0.0)
    cs_prev = jnp.dot(sp_prev.astype(BF16), later, preferred_element_type=F32)
    cs_diag = jnp.dot(sp_diag.astype(BF16), later, preferred_element_type=F32)
    rs_diag = jnp.broadcast_to(jnp.sum(sp_diag, axis=1, keepdims=True), (SB_BLK, HEAD_DIM))
    rs_prev = jnp.broadcast_to(jnp.sum(sp_prev, axis=1, keepdims=True), (SB_BLK, HEAD_DIM))
    arg_prev = (z[:, :SB_BLK] - sp_prev) - cs_prev - jnp.concatenate([rs_diag, rs_diag], axis=1)
    arg_diag = jnp.where(causal, (z[:, SB_BLK:] - sp_diag) - cs_diag, NEG_INF)
    a = jnp.exp2(jnp.concatenate([arg_prev, arg_diag], axis=1)).astype(BF16)
    pv = jnp.dot(a, v2, preferred_element_type=F32)
    return pv, rs_diag + rs_prev


def _stickbreak_kernel(q_ref, k_ref, v_ref, o_ref, acc_ref, tot_ref):
    qi = pl.program_id(1)
    row = lax.broadcasted_iota(jnp.int32, (SB_BLK, SB_BLK), 0)
    col = lax.broadcasted_iota(jnp.int32, (SB_BLK, SB_BLK), 1)
    later = (row > col).astype(BF16)
    causal = col < row
    heads = [slice(h * HEAD_DIM, (h + 1) * HEAD_DIM) for h in range(N_HEADS)]

    def key_rows(kb, nblk=1):
        return pl.ds(pl.multiple_of(kb * SB_BLK, SB_BLK), nblk * SB_BLK)

    @pl.when(qi == 0)
    def _():
        rows = key_rows(qi)
        zero = jnp.zeros((SB_BLK, HEAD_DIM), F32)
        for hs in heads:
            pv, rs = _sb_block(q_ref[:, hs], k_ref[rows, hs], v_ref[rows, hs], zero, later, causal)
            acc_ref[:, hs] = pv
            tot_ref[:, hs] = rs

    @pl.when(qi > 0)
    def _():
        rows = key_rows(qi - 1, 2)
        for hs in heads:
            pv, rs = _sb_first_two(q_ref[:, hs], k_ref[rows, hs], v_ref[rows, hs], later, causal)
            acc_ref[:, hs] = pv
            tot_ref[:, hs] = rs

    def earlier_block(kb):
        rows = key_rows(kb)
        for hs in heads:
            pv, rs = _sb_block(q_ref[:, hs], k_ref[rows, hs], v_ref[rows, hs],
                               tot_ref[:, hs], later, None)
            acc_ref[:, hs] += pv
            tot_ref[:, hs] += rs

    def cond(c):
        kb, consumed = c
        return (kb >= 0) & (consumed < SB_DONE)

    def body(c):
        kb, _ = c
        earlier_block(kb)
        return kb - 1, jnp.min(tot_ref[...])

    lax.while_loop(cond, body, (qi - 2, jnp.min(tot_ref[...])))
    o_ref[...] = acc_ref[...].astype(o_ref.dtype)


def _stickbreak(proj3):
    b, s, _ = proj3.shape
    return pl.pallas_call(
        _stickbreak_kernel,
        grid=(b, s // SB_BLK),
        in_specs=[
            pl.BlockSpec((None, SB_BLK, W_HEADS), lambda bi, qi: (bi, qi, P_QB)),
            pl.BlockSpec((None, s, W_HEADS), lambda bi, qi: (bi, 0, P_KB),
                         pipeline_mode=pl.Buffered(1)),
            pl.BlockSpec((None, s, W_HEADS), lambda bi, qi: (bi, 0, P_VB),
                         pipeline_mode=pl.Buffered(1)),
        ],
        out_specs=pl.BlockSpec((None, SB_BLK, W_HEADS), lambda bi, qi: (bi, qi, 0)),
        out_shape=jax.ShapeDtypeStruct((b, s, W_HEADS), BF16),
        scratch_shapes=[pltpu.VMEM((SB_BLK, W_HEADS), F32), pltpu.VMEM((SB_BLK, W_HEADS), F32)],
        compiler_params=pltpu.CompilerParams(
            dimension_semantics=("arbitrary", "arbitrary"),
            vmem_limit_bytes=48 * MIB),
        name="stickbreak",
    )(proj3, proj3, proj3)


MIX_TM = 512


def _mixout_kernel(x_ref, ya_ref, yb_ref, ga_ref, gb_ref, wa_ref, wb_ref, wo_ref, o_ref):
    ya = jnp.concatenate([ya_ref[h] for h in range(N_HEADS)], axis=1).astype(BF16)
    ta = jnp.dot(ya, wa_ref[...], preferred_element_type=F32)
    tb = jnp.dot(yb_ref[...], wb_ref[...], preferred_element_type=F32)
    merged = (jax.nn.sigmoid(ga_ref[...].astype(F32)) * ta
              + jax.nn.sigmoid(gb_ref[...].astype(F32)) * tb)
    o_ref[...] = x_ref[...] + jnp.dot(merged.astype(BF16), wo_ref[...], preferred_element_type=F32)


def _resident(shape):
    return pl.BlockSpec(shape, lambda *_: (0,) * len(shape), pipeline_mode=pl.Buffered(1))


def _mixout(x2d, ya4, yb2d, proj2d, wa, wb, wo):
    t = x2d.shape[0]
    tm = MIX_TM
    tiles_per_seq = ya4.shape[2] // tm
    ga_col = ((P_VB + 1) * W_HEADS) // D_MODEL
    assert ga_col * D_MODEL == (P_VB + 1) * W_HEADS
    gb_col = ga_col + 1
    return pl.pallas_call(
        _mixout_kernel,
        grid=(t // tm,),
        in_specs=[
            pl.BlockSpec((tm, D_MODEL), lambda i: (i, 0)),
            pl.BlockSpec((None, N_HEADS, tm, HEAD_DIM),
                         lambda i: (i // tiles_per_seq, 0, i % tiles_per_seq, 0)),
            pl.BlockSpec((tm, W_HEADS), lambda i: (i, 0)),
            pl.BlockSpec((tm, D_MODEL), lambda i: (i, ga_col)),
            pl.BlockSpec((tm, D_MODEL), lambda i: (i, gb_col)),
            _resident((W_HEADS, D_MODEL)),
            _resident((W_HEADS, D_MODEL)),
            _resident((D_MODEL, D_MODEL)),
        ],
        out_specs=pl.BlockSpec((tm, D_MODEL), lambda i: (i, 0)),
        out_shape=jax.ShapeDtypeStruct((t, D_MODEL), F32),
        compiler_params=pltpu.CompilerParams(
            dimension_semantics=("arbitrary",),
            vmem_limit_bytes=56 * MIB),
        name="mixout",
    )(x2d, ya4, yb2d, proj2d, proj2d, wa, wb, wo)


MLP_TM = 1024
MLP_TF = 512


def _mlp_kernel(x_ref, g_ref, wu_ref, wd_ref, o_ref, hn_ref):
    f = pl.program_id(1)

    def chunk(hn):
        u = jnp.dot(hn, wu_ref[...], preferred_element_type=F32)
        a = jnp.square(jnp.maximum(u, 0.0)).astype(BF16)
        return jnp.dot(a, wd_ref[...], preferred_element_type=F32)

    @pl.when(f == 0)
    def _():
        x = x_ref[...]
        hn = _rms_scale(x, g_ref[...]).astype(BF16)
        hn_ref[...] = hn
        o_ref[...] = x + chunk(hn)

    @pl.when(f > 0)
    def _():
        o_ref[...] += chunk(hn_ref[...])


def _mlp(x2d, g, wu, wd):
    t = x2d.shape[0]
    tm, tf = MLP_TM, MLP_TF
    return pl.pallas_call(
        _mlp_kernel,
        grid=(t // tm, D_FF // tf),
        in_specs=[
            pl.BlockSpec((tm, D_MODEL), lambda i, f: (i, 0)),
            pl.BlockSpec((1, D_MODEL), lambda i, f: (0, 0)),
            pl.BlockSpec((D_MODEL, tf), lambda i, f: (0, f)),
            pl.BlockSpec((tf, D_MODEL), lambda i, f: (f, 0)),
        ],
        out_specs=pl.BlockSpec((tm, D_MODEL), lambda i, f: (i, 0)),
        out_shape=jax.ShapeDtypeStruct((t, D_MODEL), F32),
        scratch_shapes=[pltpu.VMEM((tm, D_MODEL), BF16)],
        compiler_params=pltpu.CompilerParams(
            dimension_semantics=("arbitrary", "arbitrary"),
            vmem_limit_bytes=56 * MIB),
        name="mlp",
    )(x2d, g, wu, wd)


PLE_TM = 512


def _ple_kernel(x_ref, p_ref, g_ref, wg_ref, wp_ref, o_ref, wg_bf_ref, wp_bf_ref):
    @pl.when(pl.program_id(0) == 0)
    def _():
        wg_bf_ref[...] = wg_ref[...].astype(BF16)
        wp_bf_ref[...] = wp_ref[...].astype(BF16)

    x = x_ref[...]
    hn = _rms_scale(x, g_ref[...]).astype(BF16)
    gate = jnp.dot(hn, wg_bf_ref[...], preferred_element_type=F32)
    pp = jnp.dot(p_ref[...].astype(BF16), wp_bf_ref[...], preferred_element_type=F32)
    o_ref[...] = x + pp * jax.nn.sigmoid(gate)


def _ple(x2d, p2d, g, wg, wp):
    t = x2d.shape[0]
    tm = PLE_TM
    ple_dim = p2d.shape[1]
    return pl.pallas_call(
        _ple_kernel,
        grid=(t // tm,),
        in_specs=[
            pl.BlockSpec((tm, D_MODEL), lambda i: (i, 0)),
            pl.BlockSpec((tm, ple_dim), lambda i: (i, 0)),
            pl.BlockSpec((1, D_MODEL), lambda i: (0, 0)),
            _resident((D_MODEL, D_MODEL)),
            _resident((ple_dim, D_MODEL)),
        ],
        out_specs=pl.BlockSpec((tm, D_MODEL), lambda i: (i, 0)),
        out_shape=jax.ShapeDtypeStruct((t, D_MODEL), F32),
        scratch_shapes=[pltpu.VMEM((D_MODEL, D_MODEL), BF16), pltpu.VMEM((ple_dim, D_MODEL), BF16)],
        compiler_params=pltpu.CompilerParams(
            dimension_semantics=("arbitrary",),
            vmem_limit_bytes=48 * MIB),
        name="ple",
    )(x2d, p2d, g, wg, wp)


def _rope_tables(seq):
    pos = np.arange(seq, dtype=np.float64)
    inv = ROPE_THETA ** (-np.arange(0, ROT_DIM, 2, dtype=np.float64) / ROT_DIM)
    ang = pos[:, None] * inv[None, :]
    cos, sin = np.cos(ang), np.sin(ang)
    rest = HEAD_DIM - ROT_DIM
    cos_t = np.concatenate([cos, cos, np.ones((seq, rest))], axis=-1)
    sin_t = np.concatenate([-sin, sin, np.zeros((seq, rest))], axis=-1)
    return jnp.asarray(cos_t, F32), jnp.asarray(sin_t, F32)


def kernel(x, p, g_mix, w_in, qn_gain, kn_gain, w_branch_a, w_branch_b, w_out,
           g_mlp, w_up, w_down, g_ple, w_ple_gate, w_ple_proj):
    b, s, d = x.shape
    t = b * s
    depth = w_in.shape[0]
    cos_t, sin_t = _rope_tables(s)
    x2d = x.reshape(t, d)
    for i in range(depth):
        proj2d, nat2d, kt, ph2d = _inproj(x2d, g_mix[i][None, :], w_in[i].astype(BF16),
                                             qn_gain[i][None, :], kn_gain[i][None, :], cos_t, sin_t, s)
        proj3 = proj2d.reshape(b, s, D_PROJ)

        o_near, l_near = _near(nat2d.reshape(b, s, W_HEADS), kt, proj3)
        ya = _far(ph2d.reshape(b, s // FAR_DIL, N_MIXA * PH_SEG), o_near, l_near)
        yb2d = _stickbreak(proj3).reshape(t, W_HEADS)

        x2d = _mixout(x2d, ya, yb2d, proj2d, w_branch_a[i].astype(BF16),
                      w_branch_b[i].astype(BF16), w_out[i].astype(BF16))
        x2d = _mlp(x2d, g_mlp[i][None, :], w_up[i].astype(BF16), w_down[i].astype(BF16))
        x2d = _ple(x2d, p[i].reshape(t, -1), g_ple[i][None, :],
                   w_ple_gate[i], w_ple_proj[i])
    return x2d.reshape(b, s, d)
```

```python
import jax
import jax.numpy as jnp
import numpy as np
from jax import lax
from jax.experimental import pallas as pl
from jax.experimental.pallas import tpu as pltpu

D_MODEL = 2048
HEAD_DIM = 128
N_HEADS = 8
W_HEADS = N_HEADS * HEAD_DIM
DIL_PATTERNS = ((128, 1), (512, 4), (2048, 16))
BLOCK = 128
ROT_DIM = HEAD_DIM // 4
ROPE_THETA = 500000.0
D_FF = 4 * D_MODEL
EPS = 1e-6
D_IN = 6 * W_HEADS + 2 * D_MODEL
LOG2E = 1.4426950408889634
Q_SCALE = HEAD_DIM ** -0.5 * LOG2E

COL_QA, COL_KA, COL_VA, COL_QB, COL_KB, COL_VB = 0, 1, 2, 3, 4, 5
N_COL_TILES = D_IN // W_HEADS
PROJ_SKIP = 2
P_VA, P_QB, P_GATES = 0, 1, 2
D_PROJ = D_IN - (PROJ_SKIP + 2) * W_HEADS


def _proj_tile(j):
    return jnp.where(j <= COL_QB, jnp.maximum(j - PROJ_SKIP, 0),
                     jnp.where(j <= COL_VB, P_QB, j - PROJ_SKIP - 2))

F32 = jnp.float32
BF16 = jnp.bfloat16
NEG_INF = float("-inf")

MIB = 1024 * 1024


def _rms_scale(x, gain):
    ms = jnp.mean(x * x, axis=-1, keepdims=True)
    return x * lax.rsqrt(ms + EPS) * gain


INPROJ_TM = 512


FAR_DIL = DIL_PATTERNS[2][1]
PH_ROWS = INPROJ_TM // FAR_DIL
PH_HEAD = FAR_DIL * HEAD_DIM
PH_SEG = N_HEADS * PH_HEAD


ROW_CHUNK = 128
SUB_DIL = 4
ROLL_DN = HEAD_DIM - ROT_DIM // 2
ROLL_UP = ROT_DIM // 2


N_MIXA = 3
HEAD_GROUPS = ((0, 1, 2), (3, 4, 5), (6, 7))


def _inproj_kernel(x_ref, g_ref, w_ref, qn_ref, kn_ref, cos_ref, sin_ref,
                   o_ref, nat_ref, kt_ref, ph_ref, kb_ref, vb_ref,
                   hn_ref, tab_ref, raw_ref, ys_ref, yq_ref):
    j = pl.program_id(1)
    tm = x_ref.shape[0]

    @pl.when(j == 0)
    def _():
        hn_ref[...] = _rms_scale(x_ref[...], g_ref[...]).astype(BF16)
        lane = lax.broadcasted_iota(jnp.int32, (tm, HEAD_DIM), 1)
        sin_dn = jnp.where(lane < ROT_DIM // 2, sin_ref[...], 0.0)
        sin_up = jnp.where(lane >= ROT_DIM // 2, sin_ref[...], 0.0)
        for n, (gain_ref, scale) in enumerate(((qn_ref, Q_SCALE), (kn_ref, 1.0))):
            g = jnp.broadcast_to(gain_ref[...] * scale, (tm, HEAD_DIM))
            tab_ref[3 * n] = cos_ref[...] * g
            tab_ref[3 * n + 1] = sin_dn * pltpu.roll(g, ROLL_DN, 1)
            tab_ref[3 * n + 2] = sin_up * pltpu.roll(g, ROLL_UP, 1)

    def project():
        return jnp.dot(hn_ref[...], w_ref[...], preferred_element_type=F32)

    def finish_mixer_a(tile, head_group):
        tab = 3 * tile
        for h in head_group:
            sl = slice(h * HEAD_DIM, (h + 1) * HEAD_DIM)
            for c in range(tm // ROW_CHUNK):
                rows = slice(c * ROW_CHUNK, (c + 1) * ROW_CHUNK)
                a = raw_ref[tile, rows, sl]
                if tile < N_MIXA - 1:
                    rs = lax.rsqrt(jnp.mean(a * a, axis=-1, keepdims=True) + EPS)
                    a = (a * tab_ref[tab, rows, :]
                         + pltpu.roll(a, ROLL_DN, 1) * tab_ref[tab + 1, rows, :]
                         + pltpu.roll(a, ROLL_UP, 1) * tab_ref[tab + 2, rows, :]) * rs
                    if tile == 0:
                        nat_ref[rows, sl] = a.astype(BF16)
                    else:
                        kt_ref[h, :, rows] = a.T.astype(BF16)
                ys_ref[h, rows, :] = a
            for r_lo in range(SUB_DIL):
                yq_ref[h, r_lo] = ys_ref[h, pl.ds(r_lo, tm // SUB_DIL, stride=SUB_DIL), :]
                for r_hi in range(FAR_DIL // SUB_DIL):
                    r = r_lo + SUB_DIL * r_hi
                    lanes = slice(h * PH_HEAD + r * HEAD_DIM, h * PH_HEAD + (r + 1) * HEAD_DIM)
                    ph_ref[:, lanes] = yq_ref[h, r_lo, pl.ds(r_hi, PH_ROWS, stride=SUB_DIL), :].astype(BF16)

    @pl.when(j == 0)
    def _():
        raw_ref[0] = project()

    assert N_COL_TILES == 1 + N_MIXA * len(HEAD_GROUPS)
    for step in range(1, N_COL_TILES):
        @pl.when(j == step)
        def _(step=step):
            acc = project()
            if step < N_MIXA:
                raw_ref[step] = acc
            if step == COL_QB:
                acc = acc * Q_SCALE
            if step == COL_KB:
                kb_ref[...] = acc.astype(BF16)
            elif step == COL_VB:
                vb_ref[...] = acc.astype(BF16)
            elif step >= PROJ_SKIP:
                o_ref[...] = acc.astype(BF16)
            finish_mixer_a((step - 1) // len(HEAD_GROUPS), HEAD_GROUPS[(step - 1) % len(HEAD_GROUPS)])


def _inproj(x2d, g, w_bf16, qn, kn, cos_t, sin_t, seq):
    t = x2d.shape[0]
    tm = INPROJ_TM
    nseq = seq // tm
    tab_spec = pl.BlockSpec((tm, HEAD_DIM), lambda i, j: (i % nseq, 0))
    vec_spec = pl.BlockSpec((1, HEAD_DIM), lambda i, j: (0, 0))
    return pl.pallas_call(
        _inproj_kernel,
        grid=(t // tm, N_COL_TILES),
        in_specs=[
            pl.BlockSpec((tm, D_MODEL), lambda i, j: (i, 0)),
            pl.BlockSpec((1, D_MODEL), lambda i, j: (0, 0)),
            pl.BlockSpec((D_MODEL, W_HEADS), lambda i, j: (0, j)),
            vec_spec, vec_spec, tab_spec, tab_spec,
        ],
        out_specs=[
            pl.BlockSpec((tm, W_HEADS), lambda i, j: (i, _proj_tile(j))),
            pl.BlockSpec((tm, W_HEADS), lambda i, j: (i, 0)),
            pl.BlockSpec((None, N_HEADS, HEAD_DIM, tm), lambda i, j: (i, 0, 0, 0)),
            pl.BlockSpec((PH_ROWS, PH_SEG), lambda i, j: (i, jnp.maximum(j - 1, 0) // len(HEAD_GROUPS))),
            pl.BlockSpec((tm, W_HEADS), lambda i, j: (i, 0)),
            pl.BlockSpec((tm, W_HEADS), lambda i, j: (i, 0)),
        ],
        out_shape=[
            jax.ShapeDtypeStruct((t, D_PROJ), BF16),
            jax.ShapeDtypeStruct((t, W_HEADS), BF16),
            jax.ShapeDtypeStruct((t // tm, N_HEADS, HEAD_DIM, tm), BF16),
            jax.ShapeDtypeStruct((t // FAR_DIL, N_MIXA * PH_SEG), BF16),
            jax.ShapeDtypeStruct((t, W_HEADS), BF16),
            jax.ShapeDtypeStruct((t, W_HEADS), BF16),
        ],
        scratch_shapes=[
            pltpu.VMEM((tm, D_MODEL), BF16),
            pltpu.VMEM((6, tm, HEAD_DIM), F32),
            pltpu.VMEM((N_MIXA, tm, W_HEADS), F32),
            pltpu.VMEM((N_HEADS, tm, HEAD_DIM), F32),
            pltpu.VMEM((N_HEADS, SUB_DIL, tm // SUB_DIL, HEAD_DIM), F32),
        ],
        compiler_params=pltpu.CompilerParams(
            dimension_semantics=("arbitrary", "arbitrary"),
            vmem_limit_bytes=48 * MIB),
        name="inproj",
    )(x2d, g, w_bf16, qn, kn, cos_t, sin_t)


NEAR_QB = 512
NEAR_SUB = 256
NEAR_WIN = DIL_PATTERNS[1][0] + NEAR_SUB


def _near_kernel(q_ref, ktp_ref, kt_ref, vp_ref, v_ref, o_ref, l_ref, ktcat_ref, vcat_ref):
    first = pl.program_id(1) == 0
    ktcat_ref[:, :, :NEAR_QB] = ktp_ref[...]
    ktcat_ref[:, :, NEAR_QB:] = kt_ref[...]
    vcat_ref[:NEAR_QB, :] = vp_ref[...]
    vcat_ref[NEAR_QB:, :] = v_ref[...]

    qi = lax.broadcasted_iota(jnp.int32, (NEAR_SUB, NEAR_WIN), 0)
    col = lax.broadcasted_iota(jnp.int32, (NEAR_SUB, NEAR_WIN), 1)
    dist = qi + (NEAR_WIN - NEAR_SUB) - col
    (w1, _), (w4, d4) = DIL_PATTERNS[0], DIL_PATTERNS[1]
    assert d4 & (d4 - 1) == 0
    in1 = (dist >= 0) & (dist <= w1)
    in4 = (dist >= 0) & (dist <= w4) & ((dist & (d4 - 1)) == 0)
    bias = jnp.where(in1 & in4, 1.0, jnp.where(in1 | in4, 0.0, NEG_INF))

    for sb in range(NEAR_QB // NEAR_SUB):
        rows = slice(sb * NEAR_SUB, (sb + 1) * NEAR_SUB)
        win = slice(sb * NEAR_SUB, sb * NEAR_SUB + NEAR_WIN)
        bias_sb = jnp.where(first & (col < NEAR_QB - sb * NEAR_SUB), NEG_INF, bias)
        for h in range(N_HEADS):
            cols = slice(h * HEAD_DIM, (h + 1) * HEAD_DIM)
            q = q_ref[rows, cols]
            m = den = acc = None
            for c in reversed(range(NEAR_WIN // NEAR_SUB)):
                keys = slice(sb * NEAR_SUB + c * NEAR_SUB, sb * NEAR_SUB + (c + 1) * NEAR_SUB)
                lanes = slice(c * NEAR_SUB, (c + 1) * NEAR_SUB)
                s = jnp.dot(q, ktcat_ref[h, :, keys], preferred_element_type=F32) + bias_sb[:, lanes]
                m_c = jnp.max(s, axis=-1, keepdims=True)
                if m is None:
                    m = m_c
                    e = jnp.exp2(s - m)
                    den = jnp.sum(e, axis=-1, keepdims=True)
                    acc = jnp.dot(e.astype(BF16), vcat_ref[keys, cols], preferred_element_type=F32)
                else:
                    m_new = jnp.maximum(m, m_c)
                    alpha = jnp.exp2(m - m_new)
                    e = jnp.exp2(s - m_new)
                    den = den * alpha + jnp.sum(e, axis=-1, keepdims=True)
                    acc = acc * alpha + jnp.dot(e.astype(BF16), vcat_ref[keys, cols],
                                                preferred_element_type=F32)
                    m = m_new
            o_ref[h, rows, :] = acc / den
            l_ref[h, rows, :] = jnp.broadcast_to(m + jnp.log2(den), (NEAR_SUB, HEAD_DIM))


def _near(nat3, kt, proj3):
    b, s, _ = nat3.shape
    nq = s // NEAR_QB

    def cur(c):
        return lambda bi, qi: (bi, qi, c)

    def prev(c):
        return lambda bi, qi: (bi, jnp.maximum(qi - 1, 0), c)

    assert kt.shape[1:] == (N_HEADS, HEAD_DIM, NEAR_QB)
    kt_blk = (None, N_HEADS, HEAD_DIM, NEAR_QB)
    blk = (None, NEAR_QB, W_HEADS)
    out_spec = pl.BlockSpec((None, N_HEADS, NEAR_QB, HEAD_DIM), lambda bi, qi: (bi, 0, qi, 0))
    return pl.pallas_call(
        _near_kernel,
        grid=(b, s // NEAR_QB),
        in_specs=[
            pl.BlockSpec(blk, cur(0)),
            pl.BlockSpec(kt_blk, lambda bi, qi: (bi * nq + jnp.maximum(qi - 1, 0), 0, 0, 0)),
            pl.BlockSpec(kt_blk, lambda bi, qi: (bi * nq + qi, 0, 0, 0)),
            pl.BlockSpec(blk, prev(P_VA)), pl.BlockSpec(blk, cur(P_VA)),
        ],
        out_specs=[out_spec, out_spec],
        out_shape=[jax.ShapeDtypeStruct((b, N_HEADS, s, HEAD_DIM), F32)] * 2,
        scratch_shapes=[pltpu.VMEM((N_HEADS, HEAD_DIM, 2 * NEAR_QB), BF16),
                        pltpu.VMEM((2 * NEAR_QB, W_HEADS), BF16)],
        compiler_params=pltpu.CompilerParams(
            dimension_semantics=("arbitrary", "arbitrary"),
            vmem_limit_bytes=40 * MIB),
        name="dilated_near",
    )(nat3, kt, kt, proj3, proj3)


FAR_ROWS = BLOCK * FAR_DIL


def _far_kernel(q_ref, k_ref, v_ref, on_ref, ln_ref, y_ref, kcat_ref, vcat_ref):
    first = pl.program_id(2) == 0

    @pl.when(first)
    def _():
        kcat_ref[:BLOCK, :] = jnp.zeros((BLOCK, PH_HEAD), BF16)
        vcat_ref[:BLOCK, :] = jnp.zeros((BLOCK, PH_HEAD), BF16)

    @pl.when(jnp.logical_not(first))
    def _():
        kcat_ref[:BLOCK, :] = kcat_ref[BLOCK:, :]
        vcat_ref[:BLOCK, :] = vcat_ref[BLOCK:, :]

    kcat_ref[BLOCK:, :] = k_ref[...]
    vcat_ref[BLOCK:, :] = v_ref[...]
    qi = lax.broadcasted_iota(jnp.int32, (BLOCK, 2 * BLOCK), 0)
    col = lax.broadcasted_iota(jnp.int32, (BLOCK, 2 * BLOCK), 1)
    dist = qi + BLOCK - col
    bias = jnp.where((dist >= 0) & (dist <= BLOCK), 0.0, NEG_INF)
    bias = jnp.where(first & (col < BLOCK), NEG_INF, bias)
    dn = (((1,), (1,)), ((), ()))

    for r in range(FAR_DIL):
        lanes = slice(r * HEAD_DIM, (r + 1) * HEAD_DIM)
        tokens = pl.ds(r, BLOCK, stride=FAR_DIL)
        s = lax.dot_general(q_ref[:, lanes], kcat_ref[:, lanes], dn, preferred_element_type=F32) + bias
        m = jnp.max(s, axis=-1, keepdims=True)
        e = jnp.exp2(s - m)
        den = jnp.sum(e, axis=-1, keepdims=True)
        o = jnp.dot(e.astype(BF16), vcat_ref[:, lanes], preferred_element_type=F32) / den
        lse = m + jnp.log2(den)
        l_near = ln_ref[tokens, :]
        mm = jnp.maximum(l_near, lse)
        w_near = jnp.exp2(l_near - mm)
        w_far = jnp.exp2(lse - mm)
        y_ref[tokens, :] = (on_ref[tokens, :] * w_near + o * w_far) / (w_near + w_far)


def _far(ph3, o_near, l_near):
    b, m, _ = ph3.shape
    s = m * FAR_DIL

    def seg(n):
        return lambda bi, h, mi: (bi, mi, n * N_HEADS + h)

    blk = (None, BLOCK, PH_HEAD)
    tok_spec = pl.BlockSpec((None, None, FAR_ROWS, HEAD_DIM), lambda bi, h, mi: (bi, h, mi, 0))
    return pl.pallas_call(
        _far_kernel,
        grid=(b, N_HEADS, m // BLOCK),
        in_specs=[
            pl.BlockSpec(blk, seg(0)), pl.BlockSpec(blk, seg(1)), pl.BlockSpec(blk, seg(2)),
            tok_spec, tok_spec,
        ],
        out_specs=tok_spec,
        out_shape=jax.ShapeDtypeStruct((b, N_HEADS, s, HEAD_DIM), F32),
        scratch_shapes=[pltpu.VMEM((2 * BLOCK, PH_HEAD), BF16)] * 2,
        compiler_params=pltpu.CompilerParams(
            dimension_semantics=("arbitrary", "arbitrary", "arbitrary"),
            vmem_limit_bytes=32 * MIB),
        name="dilated_far",
    )(ph3, ph3, ph3, o_near, l_near)


SB_BLK = 256
SB_DONE = 160.0


def _sb_block(q, k, v, tot, later, causal):
    dn = (((1,), (1,)), ((), ()))
    z = lax.dot_general(q, k, dn, preferred_element_type=F32)
    sp = jnp.maximum(z, 0.0) + jnp.log2(1.0 + jnp.exp2(-jnp.abs(z)))
    if causal is not None:
        sp = jnp.where(causal, sp, 0.0)
    cs = jnp.dot(sp.astype(BF16), later, preferred_element_type=F32)
    arg = (z - sp) - cs - jnp.concatenate([tot, tot], axis=1)
    if causal is not None:
        arg = jnp.where(causal, arg, NEG_INF)
    pv = jnp.dot(jnp.exp2(arg).astype(BF16), v, preferred_element_type=F32)
    rs = jnp.sum(sp, axis=1, keepdims=True)
    return pv, jnp.broadcast_to(rs, (SB_BLK, HEAD_DIM))


def _sb_first_two(q, k2, v2, later, causal):
    dn = (((1,), (1,)), ((), ()))
    z = lax.dot_general(q, k2, dn, preferred_element_type=F32)
    sp = jnp.maximum(z, 0.0) + jnp.log2(1.0 + jnp.exp2(-jnp.abs(z)))
    sp_prev = sp[:, :SB_BLK]
    sp_diag = jnp.where(causal, sp[:, SB_BLK:], 0.0)
    cs_prev = jnp.dot(sp_prev.astype(BF16), later, preferred_element_type=F32)
    cs_diag = jnp.dot(sp_diag.astype(BF16), later, preferred_element_type=F32)
    rs_diag = jnp.broadcast_to(jnp.sum(sp_diag, axis=1, keepdims=True), (SB_BLK, HEAD_DIM))
    rs_prev = jnp.broadcast_to(jnp.sum(sp_prev, axis=1, keepdims=True), (SB_BLK, HEAD_DIM))
    arg_prev = (z[:, :SB_BLK] - sp_prev) - cs_prev - jnp.concatenate([rs_diag, rs_diag], axis=1)
    arg_diag = jnp.where(causal, (z[:, SB_BLK:] - sp_diag) - cs_diag, NEG_INF)
    a = jnp.exp2(jnp.concatenate([arg_prev, arg_diag], axis=1)).astype(BF16)
    pv = jnp.dot(a, v2, preferred_element_type=F32)
    return pv, rs_diag + rs_prev


def _stickbreak_kernel(q_ref, k_ref, v_ref, o_ref, acc_ref, tot_ref):
    qi = pl.program_id(1)
    row = lax.broadcasted_iota(jnp.int32, (SB_BLK, SB_BLK), 0)
    col = lax.broadcasted_iota(jnp.int32, (SB_BLK, SB_BLK), 1)
    later = (row > col).astype(BF16)
    causal = col < row
    heads = [slice(h * HEAD_DIM, (h + 1) * HEAD_DIM) for h in range(N_HEADS)]

    def key_rows(kb, nblk=1):
        return pl.ds(pl.multiple_of(kb * SB_BLK, SB_BLK), nblk * SB_BLK)

    @pl.when(qi == 0)
    def _():
        rows = key_rows(qi)
        zero = jnp.zeros((SB_BLK, HEAD_DIM), F32)
        for hs in heads:
            pv, rs = _sb_block(q_ref[:, hs], k_ref[rows, hs], v_ref[rows, hs], zero, later, causal)
            acc_ref[:, hs] = pv
            tot_ref[:, hs] = rs

    @pl.when(qi > 0)
    def _():
        rows = key_rows(qi - 1, 2)
        for hs in heads:
            pv, rs = _sb_first_two(q_ref[:, hs], k_ref[rows, hs], v_ref[rows, hs], later, causal)
            acc_ref[:, hs] = pv
            tot_ref[:, hs] = rs

    def earlier_block(kb):
        rows = key_rows(kb)
        for hs in heads:
            pv, rs = _sb_block(q_ref[:, hs], k_ref[rows, hs], v_ref[rows, hs],
                               tot_ref[:, hs], later, None)
            acc_ref[:, hs] += pv
            tot_ref[:, hs] += rs

    def cond(c):
        kb, consumed = c
        return (kb >= 0) & (consumed < SB_DONE)

    def body(c):
        kb, _ = c
        earlier_block(kb)
        return kb - 1, jnp.min(tot_ref[...])

    lax.while_loop(cond, body, (qi - 2, jnp.min(tot_ref[...])))
    o_ref[...] = acc_ref[...].astype(o_ref.dtype)


def _stickbreak(proj3, kb3, vb3):
    b, s, _ = proj3.shape
    return pl.pallas_call(
        _stickbreak_kernel,
        grid=(b, s // SB_BLK),
        in_specs=[
            pl.BlockSpec((None, SB_BLK, W_HEADS), lambda bi, qi: (bi, qi, P_QB)),
            pl.BlockSpec((None, s, W_HEADS), lambda bi, qi: (bi, 0, 0), pipeline_mode=pl.Buffered(1)),
            pl.BlockSpec((None, s, W_HEADS), lambda bi, qi: (bi, 0, 0), pipeline_mode=pl.Buffered(1)),
        ],
        out_specs=pl.BlockSpec((None, SB_BLK, W_HEADS), lambda bi, qi: (bi, qi, 0)),
        out_shape=jax.ShapeDtypeStruct((b, s, W_HEADS), BF16),
        scratch_shapes=[pltpu.VMEM((SB_BLK, W_HEADS), F32), pltpu.VMEM((SB_BLK, W_HEADS), F32)],
        compiler_params=pltpu.CompilerParams(
            dimension_semantics=("arbitrary", "arbitrary"),
            vmem_limit_bytes=48 * MIB),
        name="stickbreak",
    )(proj3, kb3, vb3)


MIX_TM = 512


def _mixout_kernel(x_ref, ya_ref, yb_ref, ga_ref, gb_ref, wa_ref, wb_ref, wo_ref, o_ref):
    ya = jnp.concatenate([ya_ref[h] for h in range(N_HEADS)], axis=1).astype(BF16)
    ta = jnp.dot(ya, wa_ref[...], preferred_element_type=F32)
    tb = jnp.dot(yb_ref[...], wb_ref[...], preferred_element_type=F32)
    merged = (jax.nn.sigmoid(ga_ref[...].astype(F32)) * ta
              + jax.nn.sigmoid(gb_ref[...].astype(F32)) * tb)
    o_ref[...] = x_ref[...] + jnp.dot(merged.astype(BF16), wo_ref[...], preferred_element_type=F32)


def _resident(shape):
    return pl.BlockSpec(shape, lambda *_: (0,) * len(shape), pipeline_mode=pl.Buffered(1))


def _mixout(x2d, ya4, yb2d, proj2d, wa, wb, wo):
    t = x2d.shape[0]
    tm = MIX_TM
    tiles_per_seq = ya4.shape[2] // tm
    ga_col = (P_GATES * W_HEADS) // D_MODEL
    assert ga_col * D_MODEL == P_GATES * W_HEADS
    gb_col = ga_col + 1
    return pl.pallas_call(
        _mixout_kernel,
        grid=(t // tm,),
        in_specs=[
            pl.BlockSpec((tm, D_MODEL), lambda i: (i, 0)),
            pl.BlockSpec((None, N_HEADS, tm, HEAD_DIM),
                         lambda i: (i // tiles_per_seq, 0, i % tiles_per_seq, 0)),
            pl.BlockSpec((tm, W_HEADS), lambda i: (i, 0)),
            pl.BlockSpec((tm, D_MODEL), lambda i: (i, ga_col)),
            pl.BlockSpec((tm, D_MODEL), lambda i: (i, gb_col)),
            _resident((W_HEADS, D_MODEL)),
            _resident((W_HEADS, D_MODEL)),
            _resident((D_MODEL, D_MODEL)),
        ],
        out_specs=pl.BlockSpec((tm, D_MODEL), lambda i: (i, 0)),
        out_shape=jax.ShapeDtypeStruct((t, D_MODEL), F32),
        compiler_params=pltpu.CompilerParams(
            dimension_semantics=("arbitrary",),
            vmem_limit_bytes=56 * MIB),
        name="mixout",
    )(x2d, ya4, yb2d, proj2d, proj2d, wa, wb, wo)


MLP_TM = 1024
MLP_TF = 512


def _mlp_kernel(x_ref, g_ref, wu_ref, wd_ref, o_ref, hn_ref):
    f = pl.program_id(1)

    def chunk(hn):
        u = jnp.dot(hn, wu_ref[...], preferred_element_type=F32)
        a = jnp.square(jnp.maximum(u, 0.0)).astype(BF16)
        return jnp.dot(a, wd_ref[...], preferred_element_type=F32)

    @pl.when(f == 0)
    def _():
        x = x_ref[...]
        hn = _rms_scale(x, g_ref[...]).astype(BF16)
        hn_ref[...] = hn
        o_ref[...] = x + chunk(hn)

    @pl.when(f > 0)
    def _():
        o_ref[...] += chunk(hn_ref[...])


def _mlp(x2d, g, wu, wd):
    t = x2d.shape[0]
    tm, tf = MLP_TM, MLP_TF
    return pl.pallas_call(
        _mlp_kernel,
        grid=(t // tm, D_FF // tf),
        in_specs=[
            pl.BlockSpec((tm, D_MODEL), lambda i, f: (i, 0)),
            pl.BlockSpec((1, D_MODEL), lambda i, f: (0, 0)),
            pl.BlockSpec((D_MODEL, tf), lambda i, f: (0, f)),
            pl.BlockSpec((tf, D_MODEL), lambda i, f: (f, 0)),
        ],
        out_specs=pl.BlockSpec((tm, D_MODEL), lambda i, f: (i, 0)),
        out_shape=jax.ShapeDtypeStruct((t, D_MODEL), F32),
        scratch_shapes=[pltpu.VMEM((tm, D_MODEL), BF16)],
        compiler_params=pltpu.CompilerParams(
            dimension_semantics=("arbitrary", "arbitrary"),
            vmem_limit_bytes=56 * MIB),
        name="mlp",
    )(x2d, g, wu, wd)


PLE_TM = 512


def _ple_kernel(x_ref, p_ref, g_ref, wg_ref, wp_ref, o_ref, wg_bf_ref, wp_bf_ref):
    @pl.when(pl.program_id(0) == 0)
    def _():
        wg_bf_ref[...] = wg_ref[...].astype(BF16)
        wp_bf_ref[...] = wp_ref[...].astype(BF16)

    x = x_ref[...]
    hn = _rms_scale(x, g_ref[...]).astype(BF16)
    gate = jnp.dot(hn, wg_bf_ref[...], preferred_element_type=F32)
    pp = jnp.dot(p_ref[...].astype(BF16), wp_bf_ref[...], preferred_element_type=F32)
    o_ref[...] = x + pp * jax.nn.sigmoid(gate)


def _ple(x2d, p2d, g, wg, wp):
    t = x2d.shape[0]
    tm = PLE_TM
    ple_dim = p2d.shape[1]
    return pl.pallas_call(
        _ple_kernel,
        grid=(t // tm,),
        in_specs=[
            pl.BlockSpec((tm, D_MODEL), lambda i: (i, 0)),
            pl.BlockSpec((tm, ple_dim), lambda i: (i, 0)),
            pl.BlockSpec((1, D_MODEL), lambda i: (0, 0)),
            _resident((D_MODEL, D_MODEL)),
            _resident((ple_dim, D_MODEL)),
        ],
        out_specs=pl.BlockSpec((tm, D_MODEL), lambda i: (i, 0)),
        out_shape=jax.ShapeDtypeStruct((t, D_MODEL), F32),
        scratch_shapes=[pltpu.VMEM((D_MODEL, D_MODEL), BF16), pltpu.VMEM((ple_dim, D_MODEL), BF16)],
        compiler_params=pltpu.CompilerParams(
            dimension_semantics=("arbitrary",),
            vmem_limit_bytes=48 * MIB),
        name="ple",
    )(x2d, p2d, g, wg, wp)


def _rope_tables(seq):
    pos = np.arange(seq, dtype=np.float64)
    inv = ROPE_THETA ** (-np.arange(0, ROT_DIM, 2, dtype=np.float64) / ROT_DIM)
    ang = pos[:, None] * inv[None, :]
    cos, sin = np.cos(ang), np.sin(ang)
    rest = HEAD_DIM - ROT_DIM
    cos_t = np.concatenate([cos, cos, np.ones((seq, rest))], axis=-1)
    sin_t = np.concatenate([-sin, sin, np.zeros((seq, rest))], axis=-1)
    return jnp.asarray(cos_t, F32), jnp.asarray(sin_t, F32)


def kernel(x, p, g_mix, w_in, qn_gain, kn_gain, w_branch_a, w_branch_b, w_out,
           g_mlp, w_up, w_down, g_ple, w_ple_gate, w_ple_proj):
    b, s, d = x.shape
    t = b * s
    depth = w_in.shape[0]
    cos_t, sin_t = _rope_tables(s)
    x2d = x.reshape(t, d)
    for i in range(depth):
        proj2d, nat2d, kt, ph2d, kb2d, vb2d = _inproj(x2d, g_mix[i][None, :], w_in[i].astype(BF16),
                                             qn_gain[i][None, :], kn_gain[i][None, :], cos_t, sin_t, s)
        proj3 = proj2d.reshape(b, s, D_PROJ)

        o_near, l_near = _near(nat2d.reshape(b, s, W_HEADS), kt, proj3)
        ya = _far(ph2d.reshape(b, s // FAR_DIL, N_MIXA * PH_SEG), o_near, l_near)
        yb2d = _stickbreak(proj3, kb2d.reshape(b, s, W_HEADS),
                           vb2d.reshape(b, s, W_HEADS)).reshape(t, W_HEADS)

        x2d = _mixout(x2d, ya, yb2d, proj2d, w_branch_a[i].astype(BF16),
                      w_branch_b[i].astype(BF16), w_out[i].astype(BF16))
        x2d = _mlp(x2d, g_mlp[i][None, :], w_up[i].astype(BF16), w_down[i].astype(BF16))
        x2d = _ple(x2d, p[i].reshape(t, -1), g_ple[i][None, :],
                   w_ple_gate[i], w_ple_proj[i])
    return x2d.reshape(b, s, d)
```

```python
import jax
import jax.numpy as jnp
import numpy as np
from jax import lax
from jax.experimental import pallas as pl
from jax.experimental.pallas import tpu as pltpu

D_MODEL = 2048
HEAD_DIM = 128
N_HEADS = 8
W_HEADS = N_HEADS * HEAD_DIM
DIL_PATTERNS = ((128, 1), (512, 4), (2048, 16))
BLOCK = 128
ROT_DIM = HEAD_DIM // 4
ROPE_THETA = 500000.0
D_FF = 4 * D_MODEL
EPS = 1e-6
D_IN = 6 * W_HEADS + 2 * D_MODEL
LOG2E = 1.4426950408889634
Q_SCALE = HEAD_DIM ** -0.5 * LOG2E

COL_QA, COL_KA, COL_VA, COL_QB, COL_KB, COL_VB = 0, 1, 2, 3, 4, 5
N_COL_TILES = D_IN // W_HEADS
PROJ_SKIP = 2
D_PROJ = D_IN - PROJ_SKIP * W_HEADS
P_VA, P_QB, P_KB, P_VB = (c - PROJ_SKIP for c in (COL_VA, COL_QB, COL_KB, COL_VB))

F32 = jnp.float32
BF16 = jnp.bfloat16
NEG_INF = float("-inf")

MIB = 1024 * 1024


def _rms_scale(x, gain):
    ms = jnp.mean(x * x, axis=-1, keepdims=True)
    return x * lax.rsqrt(ms + EPS) * gain


INPROJ_TM = 512


FAR_DIL = DIL_PATTERNS[2][1]
PH_ROWS = INPROJ_TM // FAR_DIL
PH_HEAD = FAR_DIL * HEAD_DIM
PH_SEG = N_HEADS * PH_HEAD


ROW_CHUNK = 128
SUB_DIL = 4
ROLL_DN = HEAD_DIM - ROT_DIM // 2
ROLL_UP = ROT_DIM // 2


N_MIXA = 3
HEAD_GROUPS = ((0, 1, 2), (3, 4, 5), (6, 7))


def _inproj_kernel(x_ref, g_ref, w_ref, qn_ref, kn_ref, cos_ref, sin_ref,
                   o_ref, nat_ref, kt_ref, ph_ref, hn_ref, tab_ref, raw_ref, ys_ref, yq_ref):
    j = pl.program_id(1)
    tm = x_ref.shape[0]

    @pl.when(j == 0)
    def _():
        hn_ref[...] = _rms_scale(x_ref[...], g_ref[...]).astype(BF16)
        lane = lax.broadcasted_iota(jnp.int32, (tm, HEAD_DIM), 1)
        seq_tiles = cos_ref.shape[0] // tm
        pos = pl.ds(pl.multiple_of((pl.program_id(0) % seq_tiles) * tm, tm), tm)
        cos = cos_ref[pos, :]
        sin = sin_ref[pos, :]
        sin_dn = jnp.where(lane < ROT_DIM // 2, sin, 0.0)
        sin_up = jnp.where(lane >= ROT_DIM // 2, sin, 0.0)
        for n, (gain_ref, scale) in enumerate(((qn_ref, Q_SCALE), (kn_ref, 1.0))):
            g = jnp.broadcast_to(gain_ref[...] * scale, (tm, HEAD_DIM))
            tab_ref[3 * n] = cos * g
            tab_ref[3 * n + 1] = sin_dn * pltpu.roll(g, ROLL_DN, 1)
            tab_ref[3 * n + 2] = sin_up * pltpu.roll(g, ROLL_UP, 1)

    def project():
        return jnp.dot(hn_ref[...], w_ref[...], preferred_element_type=F32)

    def finish_mixer_a(tile, head_group):
        tab = 3 * tile
        for h in head_group:
            sl = slice(h * HEAD_DIM, (h + 1) * HEAD_DIM)
            for c in range(tm // ROW_CHUNK):
                rows = slice(c * ROW_CHUNK, (c + 1) * ROW_CHUNK)
                a = raw_ref[tile, rows, sl]
                if tile < N_MIXA - 1:
                    rs = lax.rsqrt(jnp.mean(a * a, axis=-1, keepdims=True) + EPS)
                    a = (a * tab_ref[tab, rows, :]
                         + pltpu.roll(a, ROLL_DN, 1) * tab_ref[tab + 1, rows, :]
                         + pltpu.roll(a, ROLL_UP, 1) * tab_ref[tab + 2, rows, :]) * rs
                    if tile == 0:
                        nat_ref[rows, sl] = a.astype(BF16)
                    else:
                        kt_ref[h, :, rows] = a.T.astype(BF16)
                ys_ref[h, rows, :] = a
            for r_lo in range(SUB_DIL):
                yq_ref[h, r_lo] = ys_ref[h, pl.ds(r_lo, tm // SUB_DIL, stride=SUB_DIL), :]
                for r_hi in range(FAR_DIL // SUB_DIL):
                    r = r_lo + SUB_DIL * r_hi
                    lanes = slice(h * PH_HEAD + r * HEAD_DIM, h * PH_HEAD + (r + 1) * HEAD_DIM)
                    ph_ref[:, lanes] = yq_ref[h, r_lo, pl.ds(r_hi, PH_ROWS, stride=SUB_DIL), :].astype(BF16)

    @pl.when(j == 0)
    def _():
        raw_ref[0] = project()

    assert N_COL_TILES == 1 + N_MIXA * len(HEAD_GROUPS)
    for step in range(1, N_COL_TILES):
        @pl.when(j == step)
        def _(step=step):
            acc = project()
            if step < N_MIXA:
                raw_ref[step] = acc
            if step == COL_QB:
                acc = acc * Q_SCALE
            if step >= PROJ_SKIP:
                o_ref[...] = acc.astype(BF16)
            finish_mixer_a((step - 1) // len(HEAD_GROUPS), HEAD_GROUPS[(step - 1) % len(HEAD_GROUPS)])


def _inproj(x2d, g, w_bf16, qn, kn, cos_t, sin_t, seq):
    t = x2d.shape[0]
    tm = INPROJ_TM
    nseq = seq // tm
    del nseq
    tab_spec = pl.BlockSpec((seq, HEAD_DIM), lambda i, j: (0, 0), pipeline_mode=pl.Buffered(1))
    vec_spec = pl.BlockSpec((1, HEAD_DIM), lambda i, j: (0, 0))
    return pl.pallas_call(
        _inproj_kernel,
        grid=(t // tm, N_COL_TILES),
        in_specs=[
            pl.BlockSpec((tm, D_MODEL), lambda i, j: (i, 0)),
            pl.BlockSpec((1, D_MODEL), lambda i, j: (0, 0)),
            pl.BlockSpec((D_MODEL, W_HEADS), lambda i, j: (0, j)),
            vec_spec, vec_spec, tab_spec, tab_spec,
        ],
        out_specs=[
            pl.BlockSpec((tm, W_HEADS), lambda i, j: (i, jnp.maximum(j - PROJ_SKIP, 0))),
            pl.BlockSpec((tm, W_HEADS), lambda i, j: (i, 0)),
            pl.BlockSpec((None, N_HEADS, HEAD_DIM, tm), lambda i, j: (i, 0, 0, 0)),
            pl.BlockSpec((PH_ROWS, PH_SEG), lambda i, j: (i, jnp.maximum(j - 1, 0) // len(HEAD_GROUPS))),
        ],
        out_shape=[
            jax.ShapeDtypeStruct((t, D_PROJ), BF16),
            jax.ShapeDtypeStruct((t, W_HEADS), BF16),
            jax.ShapeDtypeStruct((t // tm, N_HEADS, HEAD_DIM, tm), BF16),
            jax.ShapeDtypeStruct((t // FAR_DIL, N_MIXA * PH_SEG), BF16),
        ],
        scratch_shapes=[
            pltpu.VMEM((tm, D_MODEL), BF16),
            pltpu.VMEM((6, tm, HEAD_DIM), F32),
            pltpu.VMEM((N_MIXA, tm, W_HEADS), F32),
            pltpu.VMEM((N_HEADS, tm, HEAD_DIM), F32),
            pltpu.VMEM((N_HEADS, SUB_DIL, tm // SUB_DIL, HEAD_DIM), F32),
        ],
        compiler_params=pltpu.CompilerParams(
            dimension_semantics=("arbitrary", "arbitrary"),
            vmem_limit_bytes=54 * MIB),
        name="inproj",
    )(x2d, g, w_bf16, qn, kn, cos_t, sin_t)


NEAR_QB = 512
NEAR_SUB = 256
NEAR_WIN = DIL_PATTERNS[1][0] + NEAR_SUB


def _near_kernel(q_ref, ktp_ref, kt_ref, vp_ref, v_ref, o_ref, l_ref, ktcat_ref, vcat_ref):
    first = pl.program_id(1) == 0
    ktcat_ref[:, :, :NEAR_QB] = ktp_ref[...]
    ktcat_ref[:, :, NEAR_QB:] = kt_ref[...]
    vcat_ref[:NEAR_QB, :] = vp_ref[...]
    vcat_ref[NEAR_QB:, :] = v_ref[...]

    qi = lax.broadcasted_iota(jnp.int32, (NEAR_SUB, NEAR_WIN), 0)
    col = lax.broadcasted_iota(jnp.int32, (NEAR_SUB, NEAR_WIN), 1)
    dist = qi + (NEAR_WIN - NEAR_SUB) - col
    (w1, _), (w4, d4) = DIL_PATTERNS[0], DIL_PATTERNS[1]
    assert d4 & (d4 - 1) == 0
    in1 = (dist >= 0) & (dist <= w1)
    in4 = (dist >= 0) & (dist <= w4) & ((dist & (d4 - 1)) == 0)
    bias = jnp.where(in1 & in4, 1.0, jnp.where(in1 | in4, 0.0, NEG_INF))

    for sb in range(NEAR_QB // NEAR_SUB):
        rows = slice(sb * NEAR_SUB, (sb + 1) * NEAR_SUB)
        win = slice(sb * NEAR_SUB, sb * NEAR_SUB + NEAR_WIN)
        bias_sb = jnp.where(first & (col < NEAR_QB - sb * NEAR_SUB), NEG_INF, bias)
        for h in range(N_HEADS):
            cols = slice(h * HEAD_DIM, (h + 1) * HEAD_DIM)
            q = q_ref[rows, cols]
            m = den = acc = None
            for c in reversed(range(NEAR_WIN // NEAR_SUB)):
                keys = slice(sb * NEAR_SUB + c * NEAR_SUB, sb * NEAR_SUB + (c + 1) * NEAR_SUB)
                lanes = slice(c * NEAR_SUB, (c + 1) * NEAR_SUB)
                s = jnp.dot(q, ktcat_ref[h, :, keys], preferred_element_type=F32) + bias_sb[:, lanes]
                m_c = jnp.max(s, axis=-1, keepdims=True)
                if m is None:
                    m = m_c
                    e = jnp.exp2(s - m)
                    den = jnp.sum(e, axis=-1, keepdims=True)
                    acc = jnp.dot(e.astype(BF16), vcat_ref[keys, cols], preferred_element_type=F32)
                else:
                    m_new = jnp.maximum(m, m_c)
                    alpha = jnp.exp2(m - m_new)
                    e = jnp.exp2(s - m_new)
                    den = den * alpha + jnp.sum(e, axis=-1, keepdims=True)
                    acc = acc * alpha + jnp.dot(e.astype(BF16), vcat_ref[keys, cols],
                                                preferred_element_type=F32)
                    m = m_new
            o_ref[h, rows, :] = acc / den
            l_ref[h, rows, :] = jnp.broadcast_to(m + jnp.log2(den), (NEAR_SUB, HEAD_DIM))


def _near(nat3, kt, proj3):
    b, s, _ = nat3.shape
    nq = s // NEAR_QB

    def cur(c):
        return lambda bi, qi: (bi, qi, c)

    def prev(c):
        return lambda bi, qi: (bi, jnp.maximum(qi - 1, 0), c)

    assert kt.shape[1:] == (N_HEADS, HEAD_DIM, NEAR_QB)
    kt_blk = (None, N_HEADS, HEAD_DIM, NEAR_QB)
    blk = (None, NEAR_QB, W_HEADS)
    out_spec = pl.BlockSpec((None, N_HEADS, NEAR_QB, HEAD_DIM), lambda bi, qi: (bi, 0, qi, 0))
    return pl.pallas_call(
        _near_kernel,
        grid=(b, s // NEAR_QB),
        in_specs=[
            pl.BlockSpec(blk, cur(0)),
            pl.BlockSpec(kt_blk, lambda bi, qi: (bi * nq + jnp.maximum(qi - 1, 0), 0, 0, 0)),
            pl.BlockSpec(kt_blk, lambda bi, qi: (bi * nq + qi, 0, 0, 0)),
            pl.BlockSpec(blk, prev(P_VA)), pl.BlockSpec(blk, cur(P_VA)),
        ],
        out_specs=[out_spec, out_spec],
        out_shape=[jax.ShapeDtypeStruct((b, N_HEADS, s, HEAD_DIM), F32)] * 2,
        scratch_shapes=[pltpu.VMEM((N_HEADS, HEAD_DIM, 2 * NEAR_QB), BF16),
                        pltpu.VMEM((2 * NEAR_QB, W_HEADS), BF16)],
        compiler_params=pltpu.CompilerParams(
            dimension_semantics=("arbitrary", "arbitrary"),
            vmem_limit_bytes=40 * MIB),
        name="dilated_near",
    )(nat3, kt, kt, proj3, proj3)


FAR_ROWS = BLOCK * FAR_DIL


def _far_kernel(q_ref, k_ref, v_ref, on_ref, ln_ref, y_ref, kcat_ref, vcat_ref):
    first = pl.program_id(2) == 0

    @pl.when(first)
    def _():
        kcat_ref[:BLOCK, :] = jnp.zeros((BLOCK, PH_HEAD), BF16)
        vcat_ref[:BLOCK, :] = jnp.zeros((BLOCK, PH_HEAD), BF16)

    @pl.when(jnp.logical_not(first))
    def _():
        kcat_ref[:BLOCK, :] = kcat_ref[BLOCK:, :]
        vcat_ref[:BLOCK, :] = vcat_ref[BLOCK:, :]

    kcat_ref[BLOCK:, :] = k_ref[...]
    vcat_ref[BLOCK:, :] = v_ref[...]
    qi = lax.broadcasted_iota(jnp.int32, (BLOCK, 2 * BLOCK), 0)
    col = lax.broadcasted_iota(jnp.int32, (BLOCK, 2 * BLOCK), 1)
    dist = qi + BLOCK - col
    bias = jnp.where((dist >= 0) & (dist <= BLOCK), 0.0, NEG_INF)
    bias = jnp.where(first & (col < BLOCK), NEG_INF, bias)
    dn = (((1,), (1,)), ((), ()))

    for r in range(FAR_DIL):
        lanes = slice(r * HEAD_DIM, (r + 1) * HEAD_DIM)
        tokens = pl.ds(r, BLOCK, stride=FAR_DIL)
        s = lax.dot_general(q_ref[:, lanes], kcat_ref[:, lanes], dn, preferred_element_type=F32) + bias
        m = jnp.max(s, axis=-1, keepdims=True)
        e = jnp.exp2(s - m)
        den = jnp.sum(e, axis=-1, keepdims=True)
        o = jnp.dot(e.astype(BF16), vcat_ref[:, lanes], preferred_element_type=F32) / den
        lse = m + jnp.log2(den)
        l_near = ln_ref[tokens, :]
        mm = jnp.maximum(l_near, lse)
        w_near = jnp.exp2(l_near - mm)
        w_far = jnp.exp2(lse - mm)
        y_ref[tokens, :] = (on_ref[tokens, :] * w_near + o * w_far) / (w_near + w_far)


def _far(ph3, o_near, l_near):
    b, m, _ = ph3.shape
    s = m * FAR_DIL

    def seg(n):
        return lambda bi, h, mi: (bi, mi, n * N_HEADS + h)

    blk = (None, BLOCK, PH_HEAD)
    tok_spec = pl.BlockSpec((None, None, FAR_ROWS, HEAD_DIM), lambda bi, h, mi: (bi, h, mi, 0))
    return pl.pallas_call(
        _far_kernel,
        grid=(b, N_HEADS, m // BLOCK),
        in_specs=[
            pl.BlockSpec(blk, seg(0)), pl.BlockSpec(blk, seg(1)), pl.BlockSpec(blk, seg(2)),
            tok_spec, tok_spec,
        ],
        out_specs=tok_spec,
        out_shape=jax.ShapeDtypeStruct((b, N_HEADS, s, HEAD_DIM), F32),
        scratch_shapes=[pltpu.VMEM((2 * BLOCK, PH_HEAD), BF16)] * 2,
        compiler_params=pltpu.CompilerParams(
            dimension_semantics=("arbitrary", "arbitrary", "arbitrary"),
            vmem_limit_bytes=32 * MIB),
        name="dilated_far",
    )(ph3, ph3, ph3, o_near, l_near)


SB_BLK = 256
SB_DONE = 160.0


def _sb_block(q, k, v, tot, later, causal):
    dn = (((1,), (1,)), ((), ()))
    z = lax.dot_general(q, k, dn, preferred_element_type=F32)
    sp = jnp.maximum(z, 0.0) + jnp.log2(1.0 + jnp.exp2(-jnp.abs(z)))
    if causal is not None:
        sp = jnp.where(causal, sp, 0.0)
    cs = jnp.dot(sp.astype(BF16), later, preferred_element_type=F32)
    arg = (z - sp) - cs - jnp.concatenate([tot, tot], axis=1)
    if causal is not None:
        arg = jnp.where(causal, arg, NEG_INF)
    pv = jnp.dot(jnp.exp2(arg).astype(BF16), v, preferred_element_type=F32)
    rs = jnp.sum(sp, axis=1, keepdims=True)
    return pv, jnp.broadcast_to(rs, (SB_BLK, HEAD_DIM))


def _sb_first_two(q, k2, v2, later, causal):
    dn = (((1,), (1,)), ((), ()))
    z = lax.dot_general(q, k2, dn, preferred_element_type=F32)
    sp = jnp.maximum(z, 0.0) + jnp.log2(1.0 + jnp.exp2(-jnp.abs(z)))
    sp_prev = sp[:, :SB_BLK]
    sp_diag = jnp.where(causal, sp[:, SB_BLK:], 0.0)
    cs_prev = jnp.dot(sp_prev.astype(BF16), later, preferred_element_type=F32)
    cs_diag = jnp.dot(sp_diag.astype(BF16), later, preferred_element_type=F32)
    rs_diag = jnp.broadcast_to(jnp.sum(sp_diag, axis=1, keepdims=True), (SB_BLK, HEAD_DIM))
    rs_prev = jnp.broadcast_to(jnp.sum(sp_prev, axis=1, keepdims=True), (SB_BLK, HEAD_DIM))
    arg_prev = (z[:, :SB_BLK] - sp_prev) - cs_prev - jnp.concatenate([rs_diag, rs_diag], axis=1)
    arg_diag = jnp.where(causal, (z[:, SB_BLK:] - sp_diag) - cs_diag, NEG_INF)
    a = jnp.exp2(jnp.concatenate([arg_prev, arg_diag], axis=1)).astype(BF16)
    pv = jnp.dot(a, v2, preferred_element_type=F32)
    return pv, rs_diag + rs_prev


def _stickbreak_kernel(q_ref, k_ref, v_ref, o_ref, acc_ref, tot_ref):
    qi = pl.program_id(1)
    row = lax.broadcasted_iota(jnp.int32, (SB_BLK, SB_BLK), 0)
    col = lax.broadcasted_iota(jnp.int32, (SB_BLK, SB_BLK), 1)
    later = (row > col).astype(BF16)
    causal = col < row
    heads = [slice(h * HEAD_DIM, (h + 1) * HEAD_DIM) for h in range(N_HEADS)]

    def key_rows(kb, nblk=1):
        return pl.ds(pl.multiple_of(kb * SB_BLK, SB_BLK), nblk * SB_BLK)

    @pl.when(qi == 0)
    def _():
        rows = key_rows(qi)
        zero = jnp.zeros((SB_BLK, HEAD_DIM), F32)
        for hs in heads:
            pv, rs = _sb_block(q_ref[:, hs], k_ref[rows, hs], v_ref[rows, hs], zero, later, causal)
            acc_ref[:, hs] = pv
            tot_ref[:, hs] = rs

    @pl.when(qi > 0)
    def _():
        rows = key_rows(qi - 1, 2)
        for hs in heads:
            pv, rs = _sb_first_two(q_ref[:, hs], k_ref[rows, hs], v_ref[rows, hs], later, causal)
            acc_ref[:, hs] = pv
            tot_ref[:, hs] = rs

    def earlier_block(kb):
        rows = key_rows(kb)
        for hs in heads:
            pv, rs = _sb_block(q_ref[:, hs], k_ref[rows, hs], v_ref[rows, hs],
                               tot_ref[:, hs], later, None)
            acc_ref[:, hs] += pv
            tot_ref[:, hs] += rs

    def cond(c):
        kb, consumed = c
        return (kb >= 0) & (consumed < SB_DONE)

    def body(c):
        kb, _ = c
        earlier_block(kb)
        return kb - 1, jnp.min(tot_ref[...])

    lax.while_loop(cond, body, (qi - 2, jnp.min(tot_ref[...])))
    o_ref[...] = acc_ref[...].astype(o_ref.dtype)


def _stickbreak(proj3):
    b, s, _ = proj3.shape
    return pl.pallas_call(
        _stickbreak_kernel,
        grid=(b, s // SB_BLK),
        in_specs=[
            pl.BlockSpec((None, SB_BLK, W_HEADS), lambda bi, qi: (bi, qi, P_QB)),
            pl.BlockSpec((None, s, W_HEADS), lambda bi, qi: (bi, 0, P_KB),
                         pipeline_mode=pl.Buffered(1)),
            pl.BlockSpec((None, s, W_HEADS), lambda bi, qi: (bi, 0, P_VB),
                         pipeline_mode=pl.Buffered(1)),
        ],
        out_specs=pl.BlockSpec((None, SB_BLK, W_HEADS), lambda bi, qi: (bi, qi, 0)),
        out_shape=jax.ShapeDtypeStruct((b, s, W_HEADS), BF16),
        scratch_shapes=[pltpu.VMEM((SB_BLK, W_HEADS), F32), pltpu.VMEM((SB_BLK, W_HEADS), F32)],
        compiler_params=pltpu.CompilerParams(
            dimension_semantics=("arbitrary", "arbitrary"),
            vmem_limit_bytes=48 * MIB),
        name="stickbreak",
    )(proj3, proj3, proj3)


MIX_TM = 512


def _mixout_kernel(x_ref, ya_ref, yb_ref, ga_ref, gb_ref, wa_ref, wb_ref, wo_ref, o_ref):
    ya = jnp.concatenate([ya_ref[h] for h in range(N_HEADS)], axis=1).astype(BF16)
    ta = jnp.dot(ya, wa_ref[...], preferred_element_type=F32)
    tb = jnp.dot(yb_ref[...], wb_ref[...], preferred_element_type=F32)
    merged = (jax.nn.sigmoid(ga_ref[...].astype(F32)) * ta
              + jax.nn.sigmoid(gb_ref[...].astype(F32)) * tb)
    o_ref[...] = x_ref[...] + jnp.dot(merged.astype(BF16), wo_ref[...], preferred_element_type=F32)


def _resident(shape):
    return pl.BlockSpec(shape, lambda *_: (0,) * len(shape), pipeline_mode=pl.Buffered(1))


def _mixout(x2d, ya4, yb2d, proj2d, wa, wb, wo):
    t = x2d.shape[0]
    tm = MIX_TM
    tiles_per_seq = ya4.shape[2] // tm
    ga_col = ((P_VB + 1) * W_HEADS) // D_MODEL
    assert ga_col * D_MODEL == (P_VB + 1) * W_HEADS
    gb_col = ga_col + 1
    return pl.pallas_call(
        _mixout_kernel,
        grid=(t // tm,),
        in_specs=[
            pl.BlockSpec((tm, D_MODEL), lambda i: (i, 0)),
            pl.BlockSpec((None, N_HEADS, tm, HEAD_DIM),
                         lambda i: (i // tiles_per_seq, 0, i % tiles_per_seq, 0)),
            pl.BlockSpec((tm, W_HEADS), lambda i: (i, 0)),
            pl.BlockSpec((tm, D_MODEL), lambda i: (i, ga_col)),
            pl.BlockSpec((tm, D_MODEL), lambda i: (i, gb_col)),
            _resident((W_HEADS, D_MODEL)),
            _resident((W_HEADS, D_MODEL)),
            _resident((D_MODEL, D_MODEL)),
        ],
        out_specs=pl.BlockSpec((tm, D_MODEL), lambda i: (i, 0)),
        out_shape=jax.ShapeDtypeStruct((t, D_MODEL), F32),
        compiler_params=pltpu.CompilerParams(
            dimension_semantics=("arbitrary",),
            vmem_limit_bytes=56 * MIB),
        name="mixout",
    )(x2d, ya4, yb2d, proj2d, proj2d, wa, wb, wo)


MLP_TM = 1024
MLP_TF = 512


def _mlp_kernel(x_ref, g_ref, wu_ref, wd_ref, o_ref, hn_ref):
    f = pl.program_id(1)

    def chunk(hn):
        u = jnp.dot(hn, wu_ref[...], preferred_element_type=F32)
        a = jnp.square(jnp.maximum(u, 0.0)).astype(BF16)
        return jnp.dot(a, wd_ref[...], preferred_element_type=F32)

    @pl.when(f == 0)
    def _():
        x = x_ref[...]
        hn = _rms_scale(x, g_ref[...]).astype(BF16)
        hn_ref[...] = hn
        o_ref[...] = x + chunk(hn)

    @pl.when(f > 0)
    def _():
        o_ref[...] += chunk(hn_ref[...])


def _mlp(x2d, g, wu, wd):
    t = x2d.shape[0]
    tm, tf = MLP_TM, MLP_TF
    return pl.pallas_call(
        _mlp_kernel,
        grid=(t // tm, D_FF // tf),
        in_specs=[
            pl.BlockSpec((tm, D_MODEL), lambda i, f: (i, 0)),
            pl.BlockSpec((1, D_MODEL), lambda i, f: (0, 0)),
            pl.BlockSpec((D_MODEL, tf), lambda i, f: (0, f)),
            pl.BlockSpec((tf, D_MODEL), lambda i, f: (f, 0)),
        ],
        out_specs=pl.BlockSpec((tm, D_MODEL), lambda i, f: (i, 0)),
        out_shape=jax.ShapeDtypeStruct((t, D_MODEL), F32),
        scratch_shapes=[pltpu.VMEM((tm, D_MODEL), BF16)],
        compiler_params=pltpu.CompilerParams(
            dimension_semantics=("arbitrary", "arbitrary"),
            vmem_limit_bytes=56 * MIB),
        name="mlp",
    )(x2d, g, wu, wd)


PLE_TM = 512


def _ple_kernel(x_ref, p_ref, g_ref, wg_ref, wp_ref, o_ref, wg_bf_ref, wp_bf_ref):
    @pl.when(pl.program_id(0) == 0)
    def _():
        wg_bf_ref[...] = wg_ref[...].astype(BF16)
        wp_bf_ref[...] = wp_ref[...].astype(BF16)

    x = x_ref[...]
    hn = _rms_scale(x, g_ref[...]).astype(BF16)
    gate = jnp.dot(hn, wg_bf_ref[...], preferred_element_type=F32)
    pp = jnp.dot(p_ref[...].astype(BF16), wp_bf_ref[...], preferred_element_type=F32)
    o_ref[...] = x + pp * jax.nn.sigmoid(gate)


def _ple(x2d, p2d, g, wg, wp):
    t = x2d.shape[0]
    tm = PLE_TM
    ple_dim = p2d.shape[1]
    return pl.pallas_call(
        _ple_kernel,
        grid=(t // tm,),
        in_specs=[
            pl.BlockSpec((tm, D_MODEL), lambda i: (i, 0)),
            pl.BlockSpec((tm, ple_dim), lambda i: (i, 0)),
            pl.BlockSpec((1, D_MODEL), lambda i: (0, 0)),
            _resident((D_MODEL, D_MODEL)),
            _resident((ple_dim, D_MODEL)),
        ],
        out_specs=pl.BlockSpec((tm, D_MODEL), lambda i: (i, 0)),
        out_shape=jax.ShapeDtypeStruct((t, D_MODEL), F32),
        scratch_shapes=[pltpu.VMEM((D_MODEL, D_MODEL), BF16), pltpu.VMEM((ple_dim, D_MODEL), BF16)],
        compiler_params=pltpu.CompilerParams(
            dimension_semantics=("arbitrary",),
            vmem_limit_bytes=48 * MIB),
        name="ple",
    )(x2d, p2d, g, wg, wp)


def _rope_tables(seq):
    pos = np.arange(seq, dtype=np.float64)
    inv = ROPE_THETA ** (-np.arange(0, ROT_DIM, 2, dtype=np.float64) / ROT_DIM)
    ang = pos[:, None] * inv[None, :]
    cos, sin = np.cos(ang), np.sin(ang)
    rest = HEAD_DIM - ROT_DIM
    cos_t = np.concatenate([cos, cos, np.ones((seq, rest))], axis=-1)
    sin_t = np.concatenate([-sin, sin, np.zeros((seq, rest))], axis=-1)
    return jnp.asarray(cos_t, F32), jnp.asarray(sin_t, F32)


def kernel(x, p, g_mix, w_in, qn_gain, kn_gain, w_branch_a, w_branch_b, w_out,
           g_mlp, w_up, w_down, g_ple, w_ple_gate, w_ple_proj):
    b, s, d = x.shape
    t = b * s
    depth = w_in.shape[0]
    cos_t, sin_t = _rope_tables(s)
    x2d = x.reshape(t, d)
    for i in range(depth):
        proj2d, nat2d, kt, ph2d = _inproj(x2d, g_mix[i][None, :], w_in[i].astype(BF16),
                                             qn_gain[i][None, :], kn_gain[i][None, :], cos_t, sin_t, s)
        proj3 = proj2d.reshape(b, s, D_PROJ)

        o_near, l_near = _near(nat2d.reshape(b, s, W_HEADS), kt, proj3)
        ya = _far(ph2d.reshape(b, s // FAR_DIL, N_MIXA * PH_SEG), o_near, l_near)
        yb2d = _stickbreak(proj3).reshape(t, W_HEADS)

        x2d = _mixout(x2d, ya, yb2d, proj2d, w_branch_a[i].astype(BF16),
                      w_branch_b[i].astype(BF16), w_out[i].astype(BF16))
        x2d = _mlp(x2d, g_mlp[i][None, :], w_up[i].astype(BF16), w_down[i].astype(BF16))
        x2d = _ple(x2d, p[i].reshape(t, -1), g_ple[i][None, :],
                   w_ple_gate[i], w_ple_proj[i])
    return x2d.reshape(b, s, d)
```
